```python
import math, functools
import jax, jax.numpy as jnp
from jax import lax
import numpy as np

D_MODEL = 1024
BATCH = 8
SEQ = 2048
DEPTH = 2
DEC_BATCH = 32
DEC_SEQ = 8
PAST_LEN = 16384
PAGE_SIZE = 128

RET_HEADS = 8
RET_DK = D_MODEL // 2 // RET_HEADS
RET_DV = 2 * RET_DK
RET_CHUNK = 128
MOBA_HEADS = 8
MOBA_HD = D_MODEL // 2 // MOBA_HEADS
MOBA_BLOCK = 256
MOBA_TOPK = 3
MOBA_QCHUNK = 128
ROPE_THETA = 10000.0
D_INNER = 2 * D_MODEL
SSM_HEAD_DIM = 64
SSM_HEADS = D_INNER // SSM_HEAD_DIM
SSM_GROUPS = 8
D_STATE = 128
CONV_W = 4
CONV_DIM = D_INNER + 2 * SSM_GROUPS * D_STATE
SSD_CHUNK = 128
MOE_GROUPS = 4
MOE_EXPERTS_PER_GROUP = 4
N_EXPERTS = MOE_GROUPS * MOE_EXPERTS_PER_GROUP
MOE_TOPK = 2
EXPERT_FF = D_MODEL // 2
MOE_TOKEN_BLOCK = 1024
NORM_EPS = 1e-6
EVEN_IN = 2 * RET_HEADS * RET_DK + 2 * RET_HEADS * RET_DV + 3 * MOBA_HEADS * MOBA_HD
EVEN_OUT = RET_HEADS * RET_DV + MOBA_HEADS * MOBA_HD
ODD_IN = D_INNER + CONV_DIM + SSM_HEADS
F32 = jnp.float32

kernel_name = "hybrid_retention_moba_ssd_hmoe_step"


def _split(x, sizes):
    idx, acc = [], 0
    for s in sizes[:-1]:
        acc += s
        idx.append(acc)
    return jnp.split(x, idx, axis=-1)


def rmsnorm(x, g=None):
    xf = x.astype(F32)
    y = xf * lax.rsqrt(jnp.mean(xf * xf, axis=-1, keepdims=True) + NORM_EPS)
    if g is not None:
        y = y * g.astype(F32)
    return y.astype(x.dtype)


def rope(x, pos):
    d = x.shape[-1]
    half = d // 2
    inv = jnp.power(ROPE_THETA, -jnp.arange(half, dtype=F32) * 2.0 / d)
    ang = pos[:, None] * inv[None, :]
    cos = jnp.cos(ang)[:, None, :]
    sin = jnp.sin(ang)[:, None, :]
    xf = x.astype(F32)
    x1, x2 = xf[..., :half], xf[..., half:]
    return jnp.concatenate([x1 * cos - x2 * sin, x2 * cos + x1 * sin], axis=-1).astype(x.dtype)


def scan_chunks(step, s0, xs, chunk):
    n = xs[0].shape[1] // chunk
    xs_c = tuple(jnp.moveaxis(a.reshape(a.shape[0], n, chunk, *a.shape[2:]), 1, 0) for a in xs)

    def body(s, inp):
        y, s_new = step(*inp, s)
        return s_new, y

    s_fin, ys = lax.scan(body, s0, xs_c)
    ys = jnp.moveaxis(ys, 0, 1)
    return ys.reshape(ys.shape[0], n * chunk, *ys.shape[3:]), s_fin


def retention_log_decay():
    return jnp.log1p(-jnp.exp2(-5.0 - jnp.arange(RET_HEADS, dtype=F32)))


def retention_chunk(q, k, v, s0, log_g):
    l = q.shape[1]
    i = jnp.arange(l, dtype=F32)
    diff = i[:, None] - i[None, :]
    dmat = jnp.where(diff[None] >= 0, jnp.exp(jnp.maximum(diff, 0.0)[None] * log_g[:, None, None]), 0.0)
    qf, kf, vf = q.astype(F32), k.astype(F32), v.astype(F32)
    scores = jnp.einsum('blhk,bmhk->bhlm', qf, kf) * dmat
    inter = jnp.einsum('blhk,bhkv->blhv', qf, s0) * jnp.exp((i + 1.0)[:, None] * log_g)[None, :, :, None]
    o = jnp.einsum('bhlm,bmhv->blhv', scores, vf) + inter
    k_dec = kf * jnp.exp((l - 1.0 - i)[:, None] * log_g)[None, :, :, None]
    s_new = s0 * jnp.exp(l * log_g)[None, :, None, None] + jnp.einsum('blhk,blhv->bhkv', k_dec, vf)
    return o, s_new


def ssd_chunk(x, dt, bm, cm, s0, a):
    b, l, h, p = x.shape
    g, n = bm.shape[2], bm.shape[3]
    r = h // g
    cs = jnp.cumsum(dt * a, axis=1)
    causal = jnp.tril(jnp.ones((l, l), dtype=bool))
    seg = cs[:, :, None, :] - cs[:, None, :, :]
    decay = jnp.exp(jnp.where(causal[None, :, :, None], seg, -jnp.inf))
    bf, cf = bm.astype(F32), cm.astype(F32)
    cb = jnp.einsum('bign,bjgn->bijg', cf, bf)
    wts = decay.reshape(b, l, l, g, r) * cb[..., None]
    xdt = (x.astype(F32) * dt[..., None]).reshape(b, l, g, r, p)
    y_intra = jnp.einsum('bijgr,bjgrp->bigrp', wts, xdt)
    s0g = s0.reshape(b, g, r, p, n)
    y_inter = jnp.einsum('bign,bgrpn->bigrp', cf, s0g) * jnp.exp(cs).reshape(b, l, g, r)[..., None]
    tail = jnp.exp(cs[:, -1:, :] - cs).reshape(b, l, g, r)[..., None]
    s_new = s0 * jnp.exp(cs[:, -1])[:, :, None, None] + jnp.einsum('bjgrp,bjgn->bgrpn', xdt * tail, bf).reshape(b, h, p, n)
    return (y_intra + y_inter).reshape(b, l, h, p), s_new


def moba_prompt(q, k, v):
    bsz, seq, nh, hd = q.shape
    nb = -(-seq // MOBA_BLOCK)
    pad = nb * MOBA_BLOCK - seq
    kp = jnp.pad(k, ((0, 0), (0, pad), (0, 0), (0, 0)))
    vp = jnp.pad(v, ((0, 0), (0, pad), (0, 0), (0, 0)))
    kb = kp.reshape(bsz, nb, MOBA_BLOCK, nh, hd)
    vb = vp.reshape(bsz, nb, MOBA_BLOCK, nh, hd)
    kbar = jnp.mean(kb.astype(F32), axis=2)
    qf = q.astype(F32)
    gate = jnp.einsum('bshd,bnhd->bshn', qf, kbar)
    qblk = jnp.arange(seq) // MOBA_BLOCK
    fully_past = jnp.arange(nb)[None, :] < qblk[:, None]
    gate = jnp.where(fully_past[None, :, None, :], gate, -jnp.inf)
    n_sel = max(1, min(MOBA_TOPK, nb - 1))
    _, sel = lax.top_k(gate, n_sel)
    kbt = kb.transpose(0, 3, 1, 2, 4)
    vbt = vb.transpose(0, 3, 1, 2, 4)
    nqc = seq // MOBA_QCHUNK
    h_idx = jnp.arange(nh)[None, :, None]
    scale = hd ** -0.5
    n_s = n_sel * MOBA_BLOCK

    def one_chunk(args):
        b, c, q_c, sel_c = args
        start = c * MOBA_QCHUNK
        qb = start // MOBA_BLOCK
        own0 = qb * MOBA_BLOCK
        k_own = lax.dynamic_slice_in_dim(kp[b], own0, MOBA_BLOCK, axis=0).astype(F32)
        v_own = lax.dynamic_slice_in_dim(vp[b], own0, MOBA_BLOCK, axis=0).astype(F32)
        qpos = start + jnp.arange(MOBA_QCHUNK)
        kpos = own0 + jnp.arange(MOBA_BLOCK)
        s_own = jnp.einsum('qhd,khd->qhk', q_c, k_own) * scale
        s_own = jnp.where((kpos[None, :] <= qpos[:, None])[:, None, :], s_own, -jnp.inf)
        k_sel = kbt[b][h_idx, sel_c].astype(F32)
        v_sel = vbt[b][h_idx, sel_c].astype(F32)
        s_sel = jnp.einsum('qhd,qhjkd->qhjk', q_c, k_sel) * scale
        s_sel = jnp.where((sel_c < qb)[..., None], s_sel, -jnp.inf)
        s = jnp.concatenate([s_sel.reshape(MOBA_QCHUNK, nh, n_s), s_own], axis=-1)
        p = jax.nn.softmax(s, axis=-1)
        o = jnp.einsum('qhjk,qhjkd->qhd', p[..., :n_s].reshape(s_sel.shape), v_sel)
        return o + jnp.einsum('qhk,khd->qhd', p[..., n_s:], v_own)

    b_ids = jnp.repeat(jnp.arange(bsz), nqc)
    c_ids = jnp.tile(jnp.arange(nqc), bsz)
    out = lax.map(one_chunk, (b_ids, c_ids, qf.reshape(bsz * nqc, MOBA_QCHUNK, nh, hd),
                              sel.reshape(bsz * nqc, MOBA_QCHUNK, nh, n_sel)))
    return out.reshape(bsz, seq, nh, hd).astype(q.dtype)


def moba_sample(q, k_new, v_new, pool_k, pool_v, page_table):
    db, t, nh, hd = q.shape
    ppb = MOBA_BLOCK // PAGE_SIZE
    n_pages = page_table.shape[1]
    n_full = n_pages // ppb
    n_own_pages = n_pages - n_full * ppb
    scale = hd ** -0.5
    qf = q.astype(F32)
    k_own, v_own = k_new.astype(F32), v_new.astype(F32)
    if n_own_pages > 0:
        own_pt = page_table[:, n_full * ppb:]
        k_own = jnp.concatenate([pool_k[own_pt].reshape(db, -1, nh, hd).astype(F32), k_own], axis=1)
        v_own = jnp.concatenate([pool_v[own_pt].reshape(db, -1, nh, hd).astype(F32), v_own], axis=1)
    n_op = n_own_pages * PAGE_SIZE
    kpos = jnp.arange(n_op + t)
    qpos = n_op + jnp.arange(t)
    s_own = jnp.einsum('bthd,bkhd->bthk', qf, k_own) * scale
    s_own = jnp.where((kpos[None, :] <= qpos[:, None])[None, :, None, :], s_own, -jnp.inf)
    if n_full > 0:
        n_sel = min(MOBA_TOPK, n_full)
        k_past = pool_k[page_table[:, :n_full * ppb]].astype(F32).reshape(db, n_full, MOBA_BLOCK, nh, hd)
        kbar = jnp.mean(k_past, axis=2)
        gate = jnp.einsum('bthd,bnhd->bthn', qf, kbar)
        _, sel = lax.top_k(gate, n_sel)
        pages = page_table[jnp.arange(db)[:, None, None, None, None], sel[..., None] * ppb + jnp.arange(ppb)]
        h_idx = jnp.arange(nh)[None, None, :, None, None]
        k_sel = pool_k[pages, :, h_idx, :].astype(F32).reshape(db, t, nh, n_sel, MOBA_BLOCK, hd)
        v_sel = pool_v[pages, :, h_idx, :].astype(F32).reshape(db, t, nh, n_sel, MOBA_BLOCK, hd)
        s_sel = jnp.einsum('bthd,bthjkd->bthjk', qf, k_sel) * scale
        n_s = n_sel * MOBA_BLOCK
        s = jnp.concatenate([s_sel.reshape(db, t, nh, n_s), s_own], axis=-1)
        p = jax.nn.softmax(s, axis=-1)
        o = jnp.einsum('bthjk,bthjkd->bthd', p[..., :n_s].reshape(s_sel.shape), v_sel)
        o = o + jnp.einsum('bthk,bkhd->bthd', p[..., n_s:], v_own)
    else:
        o = jnp.einsum('bthk,bkhd->bthd', jax.nn.softmax(s_own, axis=-1), v_own)
    return o.astype(q.dtype)


def even_mixer(xn, pos, ret_s0, moba_fn, w_in, w_out, q_g, k_g):
    bsz, t, _ = xn.shape
    rq, rk, rv, rg, mq, mk, mv = _split(xn @ w_in, [RET_HEADS * RET_DK, RET_HEADS * RET_DK,
                                                   RET_HEADS * RET_DV, RET_HEADS * RET_DV,
                                                   MOBA_HEADS * MOBA_HD, MOBA_HEADS * MOBA_HD,
                                                   MOBA_HEADS * MOBA_HD])
    rq = rope(rq.reshape(bsz, t, RET_HEADS, RET_DK), pos)
    rk = rope(rk.reshape(bsz, t, RET_HEADS, RET_DK), pos) * (RET_DK ** -0.5)
    rv = rv.reshape(bsz, t, RET_HEADS, RET_DV)
    step = functools.partial(retention_chunk, log_g=retention_log_decay())
    ro, ret_state = scan_chunks(step, ret_s0.astype(F32), (rq, rk, rv), min(RET_CHUNK, t))
    ro = jax.nn.silu(rg.astype(F32)) * rmsnorm(ro).reshape(bsz, t, RET_HEADS * RET_DV)
    mq = rope(rmsnorm(mq.reshape(bsz, t, MOBA_HEADS, MOBA_HD), q_g), pos)
    mk = rope(rmsnorm(mk.reshape(bsz, t, MOBA_HEADS, MOBA_HD), k_g), pos)
    mv = mv.reshape(bsz, t, MOBA_HEADS, MOBA_HD)
    mo = moba_fn(mq, mk, mv).reshape(bsz, t, MOBA_HEADS * MOBA_HD)
    y = jnp.concatenate([ro.astype(xn.dtype), mo.astype(xn.dtype)], axis=-1) @ w_out
    return y, ret_state, mk, mv


def odd_mixer(xn, conv_buf, ssm_s0, w_in, conv_w, conv_b, dt_bias, a_log, d_skip, norm_g, w_out):
    bsz, t, _ = xn.shape
    z, xbc, dt = _split(xn @ w_in, [D_INNER, CONV_DIM, SSM_HEADS])
    xbc_full = jnp.concatenate([conv_buf.astype(xbc.dtype), xbc], axis=1)
    new_buf = xbc_full[:, -(CONV_W - 1):]
    xc = lax.conv_general_dilated(xbc_full, conv_w[:, None, :].astype(xbc_full.dtype), window_strides=(1,),
                                  padding='VALID', dimension_numbers=('NWC', 'WIO', 'NWC'),
                                  feature_group_count=CONV_DIM)
    xc = jax.nn.silu(xc + conv_b)
    xs, bm, cm = _split(xc, [D_INNER, SSM_GROUPS * D_STATE, SSM_GROUPS * D_STATE])
    xs = xs.reshape(bsz, t, SSM_HEADS, SSM_HEAD_DIM)
    bm = bm.reshape(bsz, t, SSM_GROUPS, D_STATE)
    cm = cm.reshape(bsz, t, SSM_GROUPS, D_STATE)
    dt = jax.nn.softplus(dt.astype(F32) + dt_bias.astype(F32))
    a = -jnp.exp(a_log.astype(F32))
    step = functools.partial(ssd_chunk, a=a)
    y, s_fin = scan_chunks(step, ssm_s0.astype(F32), (xs, dt, bm, cm), min(SSD_CHUNK, t))
    y = y + d_skip.astype(F32)[None, None, :, None] * xs.astype(F32)
    yz = y.reshape(bsz, t, D_INNER) * jax.nn.silu(z.astype(F32))
    yz = rmsnorm(yz.reshape(bsz, t, SSM_GROUPS, D_INNER // SSM_GROUPS), norm_g.reshape(SSM_GROUPS, -1))
    return yz.reshape(bsz, t, D_INNER).astype(xn.dtype) @ w_out, new_buf, s_fin


def hier_moe(xn, w_rg, w_re, w_g, w_u, w_d):
    shp = xn.shape
    tok = xn.reshape(-1, shp[-1])
    n = tok.shape[0]
    blk = MOE_TOKEN_BLOCK if n % MOE_TOKEN_BLOCK == 0 else n

    def block_fn(t):
        lg = (t @ w_rg).astype(F32)
        pg = jax.nn.softmax(lg, axis=-1)
        gi = jnp.argmax(lg, axis=-1)
        wg = jnp.take_along_axis(pg, gi[:, None], axis=-1)
        le = (t @ w_re).astype(F32).reshape(-1, MOE_GROUPS, MOE_EXPERTS_PER_GROUP)
        le = jnp.take_along_axis(le, gi[:, None, None], axis=1)[:, 0]
        pe = jax.nn.softmax(le, axis=-1)
        top_p, top_i = lax.top_k(pe, MOE_TOPK)
        top_p = top_p / jnp.sum(top_p, axis=-1, keepdims=True)
        local = jnp.sum(jax.nn.one_hot(top_i, MOE_EXPERTS_PER_GROUP, dtype=F32) * top_p[..., None], axis=1)
        gate = (jax.nn.one_hot(gi, MOE_GROUPS, dtype=F32)[:, :, None] * local[:, None, :] * wg[:, :, None])
        gate = gate.reshape(-1, N_EXPERTS)
        h = jax.nn.silu(jnp.einsum('nd,edf->nef', t, w_g)) * jnp.einsum('nd,edf->nef', t, w_u)
        h = h * gate[..., None].astype(h.dtype)
        return jnp.einsum('nef,efd->nd', h, w_d)

    out = lax.map(block_fn, tok.reshape(n // blk, blk, shp[-1]))
    return out.reshape(shp)


def setup_inputs(seed: int = 0) -> dict:
    key = jax.random.key(seed)
    ks = iter(jax.random.split(key, 40))
    n_even = (DEPTH + 1) // 2
    n_odd = DEPTH // 2
    n_pages = PAST_LEN // PAGE_SIZE
    n_pool = (DEC_BATCH * n_pages * 5) // 4

    def nrm(shape, scale):
        return jax.random.normal(next(ks), shape, F32) * scale

    x_prompt = nrm((BATCH, SEQ, D_MODEL), 1.0)
    x_sample = nrm((DEC_BATCH, DEC_SEQ, D_MODEL), 1.0)
    state_ret = nrm((n_even, DEC_BATCH, RET_HEADS, RET_DK, RET_DV), 0.3)
    cache_k = nrm((n_even, n_pool, PAGE_SIZE, MOBA_HEADS, MOBA_HD), 1.0)
    cache_v = nrm((n_even, n_pool, PAGE_SIZE, MOBA_HEADS, MOBA_HD), 1.0)
    page_table = jax.random.permutation(next(ks), n_pool)[:DEC_BATCH * n_pages].reshape(DEC_BATCH, n_pages).astype(jnp.int32)
    state_ssm = nrm((n_odd, DEC_BATCH, SSM_HEADS, SSM_HEAD_DIM, D_STATE), 0.3)
    state_conv = nrm((n_odd, DEC_BATCH, CONV_W - 1, CONV_DIM), 1.0)
    norm_mix = 1.0 + nrm((DEPTH, D_MODEL), 0.02)
    norm_ffn = 1.0 + nrm((DEPTH, D_MODEL), 0.02)
    w_in_even = nrm((n_even, D_MODEL, EVEN_IN), D_MODEL ** -0.5)
    w_out_even = nrm((n_even, EVEN_OUT, D_MODEL), EVEN_OUT ** -0.5)
    q_norm = 1.0 + nrm((n_even, MOBA_HD), 0.02)
    k_norm = 1.0 + nrm((n_even, MOBA_HD), 0.02)
    w_in_odd = nrm((n_odd, D_MODEL, ODD_IN), D_MODEL ** -0.5)
    conv_w = nrm((n_odd, CONV_W, CONV_DIM), 0.5)
    conv_b = nrm((n_odd, CONV_DIM), 0.02)
    dt0 = jnp.exp(jax.random.uniform(next(ks), (n_odd, SSM_HEADS), F32, math.log(1e-3), math.log(1e-1)))
    dt_bias = dt0 + jnp.log(-jnp.expm1(-dt0))
    a_log = jnp.log(jax.random.uniform(next(ks), (n_odd, SSM_HEADS), F32, 1.0, 16.0))
    d_skip = 1.0 + nrm((n_odd, SSM_HEADS), 0.1)
    ssd_norm = 1.0 + nrm((n_odd, D_INNER), 0.02)
    w_out_odd = nrm((n_odd, D_INNER, D_MODEL), D_INNER ** -0.5)
    w_router_group = nrm((DEPTH, D_MODEL, MOE_GROUPS), D_MODEL ** -0.5)
    w_router_expert = nrm((DEPTH, D_MODEL, N_EXPERTS), D_MODEL ** -0.5)
    w_expert_gate = nrm((DEPTH, N_EXPERTS, D_MODEL, EXPERT_FF), D_MODEL ** -0.5)
    w_expert_up = nrm((DEPTH, N_EXPERTS, D_MODEL, EXPERT_FF), D_MODEL ** -0.5)
    w_expert_down = nrm((DEPTH, N_EXPERTS, EXPERT_FF, D_MODEL), EXPERT_FF ** -0.5)
    return {"x_prompt": x_prompt, "x_sample": x_sample, "state_ret": state_ret, "cache_k": cache_k,
            "cache_v": cache_v, "page_table": page_table, "state_ssm": state_ssm, "state_conv": state_conv,
            "norm_mix": norm_mix, "norm_ffn": norm_ffn, "w_in_even": w_in_even, "w_out_even": w_out_even,
            "q_norm": q_norm, "k_norm": k_norm, "w_in_odd": w_in_odd, "conv_w": conv_w, "conv_b": conv_b,
            "dt_bias": dt_bias, "a_log": a_log, "d_skip": d_skip, "ssd_norm": ssd_norm, "w_out_odd": w_out_odd,
            "w_router_group": w_router_group, "w_router_expert": w_router_expert,
            "w_expert_gate": w_expert_gate, "w_expert_up": w_expert_up, "w_expert_down": w_expert_down}


def reference(x_prompt, x_sample, state_ret, cache_k, cache_v, page_table, state_ssm, state_conv,
              norm_mix, norm_ffn, w_in_even, w_out_even, q_norm, k_norm, w_in_odd, conv_w, conv_b,
              dt_bias, a_log, d_skip, ssd_norm, w_out_odd, w_router_group, w_router_expert,
              w_expert_gate, w_expert_up, w_expert_down):
    bp, seq, _ = x_prompt.shape
    bs, dec_seq, _ = x_sample.shape
    past_len = page_table.shape[1] * PAGE_SIZE
    pos_p = jnp.arange(seq, dtype=F32)
    pos_s = past_len + jnp.arange(dec_seq, dtype=F32)
    hp, hs = x_prompt, x_sample
    ret_p, ret_s, k_p, v_p, k_s, v_s = [], [], [], [], [], []
    ssm_p, ssm_s, conv_p, conv_s = [], [], [], []
    for layer in range(DEPTH):
        if layer % 2 == 0:
            e = layer // 2
            mix = functools.partial(even_mixer, w_in=w_in_even[e], w_out=w_out_even[e], q_g=q_norm[e], k_g=k_norm[e])
            yp, rs1, kk1, vv1 = mix(rmsnorm(hp, norm_mix[layer]), pos_p,
                                    jnp.zeros((bp, RET_HEADS, RET_DK, RET_DV), F32), moba_prompt)
            paged = functools.partial(moba_sample, pool_k=cache_k[e], pool_v=cache_v[e], page_table=page_table)
            ys, rs2, kk2, vv2 = mix(rmsnorm(hs, norm_mix[layer]), pos_s, state_ret[e], paged)
            ret_p.append(rs1); ret_s.append(rs2)
            k_p.append(kk1); v_p.append(vv1); k_s.append(kk2); v_s.append(vv2)
        else:
            o = layer // 2
            mix = functools.partial(odd_mixer, w_in=w_in_odd[o], conv_w=conv_w[o], conv_b=conv_b[o],
                                    dt_bias=dt_bias[o], a_log=a_log[o], d_skip=d_skip[o],
                                    norm_g=ssd_norm[o], w_out=w_out_odd[o])
            yp, cb1, ss1 = mix(rmsnorm(hp, norm_mix[layer]), jnp.zeros((bp, CONV_W - 1, CONV_DIM), x_prompt.dtype),
                               jnp.zeros((bp, SSM_HEADS, SSM_HEAD_DIM, D_STATE), F32))
            ys, cb2, ss2 = mix(rmsnorm(hs, norm_mix[layer]), state_conv[o], state_ssm[o])
            ssm_p.append(ss1); ssm_s.append(ss2); conv_p.append(cb1); conv_s.append(cb2)
        hp = hp + yp
        hs = hs + ys
        ffn = functools.partial(hier_moe, w_rg=w_router_group[layer], w_re=w_router_expert[layer],
                                w_g=w_expert_gate[layer], w_u=w_expert_up[layer], w_d=w_expert_down[layer])
        hp = hp + ffn(rmsnorm(hp, norm_ffn[layer]))
        hs = hs + ffn(rmsnorm(hs, norm_ffn[layer]))
    return (hp, hs, jnp.stack(ret_p), jnp.stack(ret_s), jnp.stack(k_p), jnp.stack(v_p), jnp.stack(k_s),
            jnp.stack(v_s), jnp.stack(ssm_p), jnp.stack(ssm_s), jnp.stack(conv_p), jnp.stack(conv_s))
```

```python
import functools
import math

import jax
import jax.numpy as jnp
from jax import lax
from jax.experimental import pallas as pl
from jax.experimental.pallas import tpu as pltpu

F32 = jnp.float32
BF16 = jnp.bfloat16
I32 = jnp.int32

D_MODEL = 1024
PAGE_SIZE = 128
RET_HEADS = 8
RET_DK = 64
RET_DV = 128
MOBA_HEADS = 8
MOBA_HD = 64
MOBA_BLOCK = 256
MOBA_TOPK = 3
ROPE_THETA = 10000.0
D_INNER = 2048
SSM_HEADS = 32
SSM_GROUPS = 8
D_STATE = 128
CONV_W = 4
CONV_DIM = 4096
MOE_GROUPS = 4
MOE_EPG = 4
N_EXPERTS = 16
EXPERT_FF = 512
NORM_EPS = 1e-6
N_CLASSES = MOE_GROUPS * 6
LANES = 128
ROWS = 128
MOE_TILE = 256
VMEM_LIMIT = 56 * 1024 * 1024

_NEG_INF = float("-inf")


def _cparams(*sem):
    return pltpu.CompilerParams(dimension_semantics=sem, vmem_limit_bytes=VMEM_LIMIT)


def _dot(a, b):
    return jnp.dot(a.astype(BF16), b.astype(BF16), preferred_element_type=F32)


def _dot_nt(a, b):
    return lax.dot_general(a.astype(BF16), b.astype(BF16), (((1,), (1,)), ((), ())), preferred_element_type=F32)


def _split2(x):
    hi = x.astype(BF16)
    lo = (x - hi.astype(F32)).astype(BF16)
    return hi, lo


def _split3(x):
    hi = x.astype(BF16)
    r = x - hi.astype(F32)
    mid = r.astype(BF16)
    lo = (r - mid.astype(F32)).astype(BF16)
    return hi, mid, lo


def _dot3(a, b):
    ah, al = _split2(a)
    bh, bl = _split2(b)
    return (jnp.dot(ah, bh, preferred_element_type=F32) + jnp.dot(ah, bl, preferred_element_type=F32)
            + jnp.dot(al, bh, preferred_element_type=F32))


def _dot3_nt(a, b):
    dn = (((1,), (1,)), ((), ()))
    ah, al = _split2(a)
    bh, bl = _split2(b)
    return (lax.dot_general(ah, bh, dn, preferred_element_type=F32) + lax.dot_general(ah, bl, dn, preferred_element_type=F32)
            + lax.dot_general(al, bh, dn, preferred_element_type=F32))


def _dot01(m01, x):
    hi, mid, lo = _split3(x)
    return (jnp.dot(m01, hi, preferred_element_type=F32) + jnp.dot(m01, mid, preferred_element_type=F32)
            + jnp.dot(m01, lo, preferred_element_type=F32))


def _rmsnorm(x, g=None):
    y = x * lax.rsqrt(jnp.mean(x * x, axis=-1, keepdims=True) + NORM_EPS)
    return y if g is None else y * g


def _lane_iota(shape):
    return lax.broadcasted_iota(I32, shape, len(shape) - 1)


def _row_iota(shape):
    return lax.broadcasted_iota(I32, shape, len(shape) - 2)


def _even_inproj_kernel(x_ref, g_ref, w_ref, cos_ref, sin_ref, qg_ref, kg_ref, bd_ref,
                        rq_ref, rk_ref, rv_ref, rg_ref, mq_ref, mk_ref, mv_ref):
    xb = _rmsnorm(x_ref[...], g_ref[...]).astype(BF16)
    cos = cos_ref[...]
    sin = sin_ref[...]
    first_half = (_lane_iota(cos.shape) % MOBA_HD) < (MOBA_HD // 2)

    def proj(c0, c1):
        return jnp.dot(xb, w_ref[:, c0:c1], preferred_element_type=F32)

    def rope(a):
        outs = []
        for j in range(a.shape[1] // LANES):
            s = a[:, j * LANES:(j + 1) * LANES]
            rot = jnp.where(first_half, pltpu.roll(s, LANES - MOBA_HD // 2, 1), pltpu.roll(s, MOBA_HD // 2, 1))
            outs.append(s * cos + rot * sin)
        return jnp.concatenate(outs, axis=1)

    def qk_norm(a, gam):
        ss = jnp.dot((a * a).astype(BF16), bd_ref[...], preferred_element_type=F32)
        return a * lax.rsqrt(ss * (1.0 / MOBA_HD) + NORM_EPS) * gam

    rq_ref[...] = rope(proj(0, 512))
    rk_ref[...] = rope(proj(512, 1024)) * (RET_DK ** -0.5)
    rv_ref[...] = proj(1024, 2048)
    rg_ref[...] = proj(2048, 3072)
    mq_ref[...] = rope(qk_norm(proj(3072, 3584), qg_ref[...]))
    mk_ref[...] = rope(qk_norm(proj(3584, 4096), kg_ref[...]))
    mv_ref[...] = proj(4096, 4608)


def _even_inproj(x, g, w, cos, sin, qg, kg, bd, tm):
    n = x.shape[0]
    n_pos_blocks = cos.shape[0] // tm
    row = lambda i: (i, 0)
    const = lambda i: (0, 0)
    widths = (512, 512, 1024, 1024, 512, 512, 512)
    return pl.pallas_call(
        _even_inproj_kernel,
        grid=(n // tm,),
        in_specs=[pl.BlockSpec((tm, D_MODEL), row), pl.BlockSpec((1, D_MODEL), const),
                  pl.BlockSpec(w.shape, const),
                  pl.BlockSpec((tm, LANES), lambda i: (i % n_pos_blocks, 0)),
                  pl.BlockSpec((tm, LANES), lambda i: (i % n_pos_blocks, 0)),
                  pl.BlockSpec((1, 512), const), pl.BlockSpec((1, 512), const), pl.BlockSpec((512, 512), const)],
        out_specs=[pl.BlockSpec((tm, c), row) for c in widths],
        out_shape=[jax.ShapeDtypeStruct((n, c), F32) for c in widths],
        compiler_params=_cparams("parallel"),
        name="even_inproj",
    )(x, g, w, cos, sin, qg, kg, bd)


def _retention_kernel(rq_ref, rk_ref, rv_ref, rg_ref, s0_ref, dmat_ref, qdec_ref, kdec_ref, sdec_ref,
                      o_ref, sout_ref, s_scr, *, n_seq, seq_len):
    c = pl.program_id(1)

    @pl.when(c == 0)
    def _():
        s_scr[...] = s0_ref[...]

    lane = _lane_iota((ROWS, LANES))
    row = _row_iota((ROWS, LANES))
    for p in range(RET_HEADS // 2):
        q2 = rq_ref[:, p * LANES:(p + 1) * LANES]
        k2 = rk_ref[:, p * LANES:(p + 1) * LANES]
        kd_t = (k2 * kdec_ref[p]).T
        v_pair = []
        for e in range(2):
            h = 2 * p + e
            head_lanes = (lane < RET_DK) if e == 0 else (lane >= RET_DK)
            qm = jnp.where(head_lanes, q2, 0.0)
            scores = _dot_nt(qm, k2) * dmat_ref[h]
            vh = rv_ref[:, h * RET_DV:(h + 1) * RET_DV]
            v_pair.append(vh)
            if n_seq == 1:
                inter = _dot(qm, s_scr[0, p])
            else:
                inter = jnp.zeros((ROWS, RET_DV), F32)
                for s in range(n_seq):
                    in_seq = (row >= s * seq_len) & (row < (s + 1) * seq_len)
                    inter = inter + jnp.where(in_seq, _dot(qm, s_scr[s, p]), 0.0)
            o = _dot(scores, vh) + inter * qdec_ref[h]
            gate = rg_ref[:, h * RET_DV:(h + 1) * RET_DV]
            o_ref[:, h * RET_DV:(h + 1) * RET_DV] = jax.nn.silu(gate) * _rmsnorm(o)
        for s in range(n_seq):
            if n_seq == 1:
                kt = kd_t
            else:
                kt = jnp.where((lane >= s * seq_len) & (lane < (s + 1) * seq_len), kd_t, 0.0)
            upd = jnp.where(row < RET_DK, _dot(kt, v_pair[0]), _dot(kt, v_pair[1]))
            s_scr[s, p] = s_scr[s, p] * sdec_ref[p] + upd

    @pl.when(c == pl.num_programs(1) - 1)
    def _():
        sout_ref[...] = s_scr[...]


def _retention_tables(seq_len):
    log_g = jnp.log1p(-jnp.exp2(-5.0 - jnp.arange(RET_HEADS, dtype=F32)))
    r = jnp.arange(ROWS)
    li = (r % seq_len).astype(F32)
    sid = r // seq_len
    diff = li[:, None] - li[None, :]
    same = sid[:, None] == sid[None, :]
    dmat = jnp.where(same[None] & (diff[None] >= 0),
                     jnp.exp(jnp.maximum(diff, 0.0)[None] * log_g[:, None, None]), 0.0)
    qdec = jnp.broadcast_to(jnp.exp((li + 1.0)[None, :, None] * log_g[:, None, None]), (RET_HEADS, ROWS, RET_DV))
    lane_head = jnp.arange(LANES) // RET_DK
    pair_log_g = log_g.reshape(RET_HEADS // 2, 2)[:, lane_head]
    kdec = jnp.exp((seq_len - 1.0 - li)[None, :, None] * pair_log_g[:, None, :])
    sdec = jnp.broadcast_to(jnp.exp(float(seq_len) * pair_log_g)[:, :, None], (RET_HEADS // 2, LANES, RET_DV))
    return dmat, qdec, kdec, sdec


def _retention(rq, rk, rv, rg, s0, n_batch, n_tok):
    seq_len = min(ROWS, n_tok)
    n_seq = ROWS // seq_len
    n_chunks = n_tok // seq_len
    tables = _retention_tables(seq_len)
    s0p = s0.reshape(n_batch, RET_HEADS // 2, 2 * RET_DK, RET_DV)
    blk = lambda i, c: (i * n_chunks + c, 0)
    st = lambda i, c: (i, 0, 0, 0)
    c3 = lambda i, c: (0, 0, 0)
    o, s_out = pl.pallas_call(
        functools.partial(_retention_kernel, n_seq=n_seq, seq_len=seq_len),
        grid=(n_batch // n_seq, n_chunks),
        in_specs=[pl.BlockSpec((ROWS, 512), blk), pl.BlockSpec((ROWS, 512), blk),
                  pl.BlockSpec((ROWS, 1024), blk), pl.BlockSpec((ROWS, 1024), blk),
                  pl.BlockSpec((n_seq,) + s0p.shape[1:], st)]
                 + [pl.BlockSpec(t.shape, c3) for t in tables],
        out_specs=[pl.BlockSpec((ROWS, 1024), blk), pl.BlockSpec((n_seq,) + s0p.shape[1:], st)],
        out_shape=[jax.ShapeDtypeStruct((n_batch * n_tok, 1024), F32), jax.ShapeDtypeStruct(s0p.shape, F32)],
        scratch_shapes=[pltpu.VMEM((n_seq,) + s0p.shape[1:], F32)],
        compiler_params=_cparams("parallel", "arbitrary"),
        name="retention",
    )(rq, rk, rv, rg, s0p, *tables)
    return o, s_out.reshape(s0.shape)


def _moba_prompt_kernel(q_ref, k_ref, v_ref, o_ref, kbar_scr, *, n_blocks):
    qb = pl.program_id(1)
    blk = MOBA_BLOCK

    @pl.when(qb == 0)
    def _():
        kbar_scr[...] = jnp.zeros(kbar_scr.shape, F32)
        for n in range(n_blocks):
            kbar_scr[n:n + 1, :] = jnp.mean(k_ref[n * blk:(n + 1) * blk, :], axis=0, keepdims=True)

    lane = _lane_iota((blk, LANES))
    lane_f = lane.astype(F32)
    qb_f = qb.astype(F32)
    causal = _lane_iota((blk, blk)) <= _row_iota((blk, blk))
    own0 = pl.multiple_of(qb * blk, blk)
    for p in range(MOBA_HEADS // 2):
        cols = slice(p * LANES, (p + 1) * LANES)
        q2 = q_ref[:, cols]
        outs = []
        for e in range(2):
            head_lanes = (lane < MOBA_HD) if e == 0 else (lane >= MOBA_HD)
            qm = jnp.where(head_lanes, q2, 0.0)
            gate = _dot3_nt(qm, kbar_scr[:, cols])
            cnt = jnp.zeros((blk, LANES), F32)
            for m in range(n_blocks - 1):
                gm = gate[:, m:m + 1]
                ahead = (gm > gate) | ((gm == gate) & (lane > m))
                cnt = cnt + jnp.where(ahead, (qb > m).astype(F32), 0.0)
            sel = jnp.where((lane_f < qb_f) & (cnt < float(MOBA_TOPK)), 1.0, 0.0)
            qs = (qm * (MOBA_HD ** -0.5)).astype(BF16)

            s = _dot_nt(qs, k_ref[pl.ds(own0, blk), cols])
            s = jnp.where(causal, s, _NEG_INF)
            m_i = jnp.max(s, axis=1, keepdims=True)
            pr = jnp.exp(s - m_i)
            l_i = jnp.sum(pr, axis=1, keepdims=True)
            acc = _dot(pr, v_ref[pl.ds(own0, blk), cols])

            def body(n, carry):
                m_i, l_i, acc = carry
                r0 = pl.multiple_of(n * blk, blk)
                s = _dot_nt(qs, k_ref[pl.ds(r0, blk), cols])
                picked = jnp.sum(jnp.where(lane == n, sel, 0.0), axis=1, keepdims=True)
                s = jnp.where(picked > 0.0, s, _NEG_INF)
                m_new = jnp.maximum(m_i, jnp.max(s, axis=1, keepdims=True))
                alpha = jnp.exp(m_i - m_new)
                pr = jnp.exp(s - m_new)
                l_new = l_i * alpha + jnp.sum(pr, axis=1, keepdims=True)
                acc_new = acc * alpha + _dot(pr, v_ref[pl.ds(r0, blk), cols])
                return m_new, l_new, acc_new

            m_i, l_i, acc = lax.fori_loop(0, qb, body, (m_i, l_i, acc))
            outs.append(acc / l_i)
        o_ref[:, cols] = jnp.where(lane < MOBA_HD, outs[0], outs[1])


def _moba_prompt(mq, mk, mv, n_batch, n_tok):
    n_blocks = n_tok // MOBA_BLOCK
    qspec = pl.BlockSpec((MOBA_BLOCK, 512), lambda b, i: (b * n_blocks + i, 0))
    kvspec = pl.BlockSpec((n_tok, 512), lambda b, i: (b, 0))
    return pl.pallas_call(
        functools.partial(_moba_prompt_kernel, n_blocks=n_blocks),
        grid=(n_batch, n_blocks),
        in_specs=[qspec, kvspec, kvspec],
        out_specs=qspec,
        out_shape=jax.ShapeDtypeStruct(mq.shape, F32),
        scratch_shapes=[pltpu.VMEM((LANES, 512), F32)],
        compiler_params=_cparams("parallel", "arbitrary"),
        name="moba_prompt",
    )(mq, mk, mv)


PAGES_PER_BLOCK = MOBA_BLOCK // PAGE_SIZE
KBAR_BLOCKS_PER_STEP = 8


def _kbar_kernel(pt_ref, *refs):
    page_refs, o_ref = refs[:-1], refs[-1]
    for j in range(KBAR_BLOCKS_PER_STEP):
        tot = jnp.zeros((MOBA_HEADS, MOBA_HD), F32)
        for half in range(PAGES_PER_BLOCK):
            tot = tot + jnp.sum(page_refs[j * PAGES_PER_BLOCK + half][0], axis=0)
        o_ref[0, j] = tot * (1.0 / MOBA_BLOCK)


def _moba_kbar(pool_k, page_table):
    n_seq, n_pages = page_table.shape
    n_blocks = n_pages // PAGES_PER_BLOCK
    ppstep = KBAR_BLOCKS_PER_STEP * PAGES_PER_BLOCK

    def page_spec(j):
        return pl.BlockSpec((1, PAGE_SIZE, MOBA_HEADS, MOBA_HD),
                            lambda b, i, pt: (pt[b * n_pages + i * ppstep + j], 0, 0, 0))

    return pl.pallas_call(
        _kbar_kernel,
        grid_spec=pltpu.PrefetchScalarGridSpec(
            num_scalar_prefetch=1,
            grid=(n_seq, n_blocks // KBAR_BLOCKS_PER_STEP),
            in_specs=[page_spec(j) for j in range(ppstep)],
            out_specs=pl.BlockSpec((1, KBAR_BLOCKS_PER_STEP, MOBA_HEADS, MOBA_HD), lambda b, i, pt: (b, i, 0, 0)),
        ),
        out_shape=jax.ShapeDtypeStruct((n_seq, n_blocks, MOBA_HEADS, MOBA_HD), F32),
        compiler_params=_cparams("parallel", "parallel"),
        name="moba_kbar",
    )(page_table.reshape(-1), *([pool_k] * ppstep))


def _moba_select_kernel(q_ref, kbar_ref, o_ref, *, n_blocks, n_tok):
    lane = _lane_iota((n_tok, LANES))
    lane_f = lane.astype(F32)
    pad = jnp.zeros((LANES - n_blocks, LANES), F32)
    for h in range(MOBA_HEADS):
        p, e = divmod(h, 2)
        cols = slice(p * LANES, (p + 1) * LANES)
        q2 = q_ref[:, cols]
        head_lanes = (lane < MOBA_HD) if e == 0 else (lane >= MOBA_HD)
        qm = jnp.where(head_lanes, q2, 0.0)
        kb = jnp.concatenate([kbar_ref[0, :, cols], pad], axis=0)
        gate = jnp.where(lane < n_blocks, _dot3_nt(qm, kb), _NEG_INF)
        res = jnp.zeros((n_tok, LANES), F32)
        for j in range(MOBA_TOPK):
            mx = jnp.max(gate, axis=1, keepdims=True)
            idx = jnp.min(jnp.where(gate == mx, lane_f, float(LANES)), axis=1, keepdims=True)
            res = jnp.where(lane == j, idx, res)
            gate = jnp.where(lane_f == idx, _NEG_INF, gate)
        o_ref[0, h * n_tok:(h + 1) * n_tok, :] = res.astype(I32)


def _moba_select(mq, kbar, n_seq, n_tok):
    n_blocks = kbar.shape[1]
    kb = kbar.reshape(n_seq, n_blocks, 512)
    out = pl.pallas_call(
        functools.partial(_moba_select_kernel, n_blocks=n_blocks, n_tok=n_tok),
        grid=(n_seq,),
        in_specs=[pl.BlockSpec((n_tok, 512), lambda b: (b, 0)), pl.BlockSpec((1, n_blocks, 512), lambda b: (b, 0, 0))],
        out_specs=pl.BlockSpec((1, MOBA_HEADS * n_tok, LANES), lambda b: (b, 0, 0)),
        out_shape=jax.ShapeDtypeStruct((n_seq, MOBA_HEADS * n_tok, LANES), I32),
        compiler_params=_cparams("parallel"),
        name="moba_select",
    )(mq, kb)
    return out[:, :, :MOBA_TOPK].reshape(-1)


def _moba_sample_kernel(sel_ref, pt_ref, q_ref, kn_ref, vn_ref, pk_ref, pv_ref, o_ref,
                        kbuf, vbuf, sem, *, n_tok, n_pages):
    s = pl.program_id(0)
    n_steps = pl.num_programs(0)
    n_sel = n_tok * MOBA_TOPK
    past = n_sel * MOBA_BLOCK

    def copies(step, slot):
        b = step // MOBA_HEADS
        h = step % MOBA_HEADS
        out = []
        for i in range(n_sel):
            blk = sel_ref[step * n_sel + i]
            for half in range(PAGES_PER_BLOCK):
                page = pt_ref[b * n_pages + blk * PAGES_PER_BLOCK + half]
                r0 = i * MOBA_BLOCK + half * PAGE_SIZE
                out.append(pltpu.make_async_copy(pk_ref.at[page, :, h, :], kbuf.at[slot, pl.ds(r0, PAGE_SIZE), :], sem.at[slot, 0]))
                out.append(pltpu.make_async_copy(pv_ref.at[page, :, h, :], vbuf.at[slot, pl.ds(r0, PAGE_SIZE), :], sem.at[slot, 1]))
        return out

    slot = s % 2

    @pl.when(s == 0)
    def _():
        for cp in copies(s, slot):
            cp.start()

    @pl.when(s + 1 < n_steps)
    def _():
        for cp in copies(s + 1, 1 - slot):
            cp.start()

    for cp in copies(s, slot):
        cp.wait()

    q = q_ref[0, 0] * (MOBA_HD ** -0.5)
    s_past = _dot_nt(q, kbuf[slot])
    key = _lane_iota((n_tok, past))
    first_key = _row_iota((n_tok, past)) * (MOBA_TOPK * MOBA_BLOCK)
    s_past = jnp.where((key >= first_key) & (key < first_key + MOBA_TOPK * MOBA_BLOCK), s_past, _NEG_INF)
    s_own = _dot_nt(q, kn_ref[0, 0])
    s_own = jnp.where(_lane_iota((n_tok, n_tok)) <= _row_iota((n_tok, n_tok)), s_own, _NEG_INF)
    mx = jnp.maximum(jnp.max(s_past, axis=1, keepdims=True), jnp.max(s_own, axis=1, keepdims=True))
    p_past = jnp.exp(s_past - mx)
    p_own = jnp.exp(s_own - mx)
    den = jnp.sum(p_past, axis=1, keepdims=True) + jnp.sum(p_own, axis=1, keepdims=True)
    o = _dot(p_past, vbuf[slot]) + _dot(p_own, vn_ref[0, 0])
    o_ref[0, 0] = o / den


def _moba_sample(mq, mk, mv, pool_k, pool_v, page_table, n_seq, n_tok):
    n_pages = page_table.shape[1]
    kbar = _moba_kbar(pool_k, page_table)
    sel = _moba_select(mq, kbar, n_seq, n_tok)

    def head_major(a):
        return a.reshape(n_seq, n_tok, MOBA_HEADS, MOBA_HD).transpose(0, 2, 1, 3)

    qh, kh, vh = head_major(mq), head_major(mk), head_major(mv)
    n_sel = n_tok * MOBA_TOPK
    spec = pl.BlockSpec((1, 1, n_tok, MOBA_HD), lambda s, sel, pt: (s // MOBA_HEADS, s % MOBA_HEADS, 0, 0))
    anyspec = pl.BlockSpec(memory_space=pl.ANY)
    o = pl.pallas_call(
        functools.partial(_moba_sample_kernel, n_tok=n_tok, n_pages=n_pages),
        grid_spec=pltpu.PrefetchScalarGridSpec(
            num_scalar_prefetch=2,
            grid=(n_seq * MOBA_HEADS,),
            in_specs=[spec, spec, spec, anyspec, anyspec],
            out_specs=spec,
            scratch_shapes=[pltpu.VMEM((2, n_sel * MOBA_BLOCK, MOBA_HD), F32),
                            pltpu.VMEM((2, n_sel * MOBA_BLOCK, MOBA_HD), F32),
                            pltpu.SemaphoreType.DMA((2, 2))],
        ),
        out_shape=jax.ShapeDtypeStruct(qh.shape, F32),
        compiler_params=_cparams("arbitrary"),
        name="moba_sample",
    )(sel, page_table.reshape(-1), qh, kh, vh, pool_k, pool_v)
    return o.transpose(0, 2, 1, 3).reshape(n_seq * n_tok, MOBA_HEADS * MOBA_HD)


def _outproj_kernel(*refs, n_in):
    x_ref, a_refs, w_refs, o_ref = refs[0], refs[1:1 + n_in], refs[1 + n_in:1 + 2 * n_in], refs[-1]
    acc = x_ref[...]
    for a_ref, w_ref in zip(a_refs, w_refs):
        acc = acc + jnp.dot(a_ref[...].astype(BF16), w_ref[...], preferred_element_type=F32)
    o_ref[...] = acc


def _outproj(x, acts, ws, tm):
    n = x.shape[0]
    row = lambda i: (i, 0)
    const = lambda i: (0, 0)
    return pl.pallas_call(
        functools.partial(_outproj_kernel, n_in=len(acts)),
        grid=(n // tm,),
        in_specs=[pl.BlockSpec((tm, D_MODEL), row)] + [pl.BlockSpec((tm, a.shape[1]), row) for a in acts]
                 + [pl.BlockSpec(w.shape, const) for w in ws],
        out_specs=pl.BlockSpec((tm, D_MODEL), row),
        out_shape=jax.ShapeDtypeStruct((n, D_MODEL), F32),
        compiler_params=_cparams("parallel"),
        name="outproj",
    )(x, *acts, *ws)


def _router_kernel(h_ref, g_ref, wr_ref, tril_ref, route_ref, cnt_ref, cnt_scr):
    i = pl.program_id(0)

    @pl.when(i == 0)
    def _():
        cnt_scr[...] = jnp.zeros(cnt_scr.shape, F32)

    xn = _rmsnorm(h_ref[...], g_ref[...])
    logits = _dot3(xn, wr_ref[...])
    tm = logits.shape[0]
    lane = _lane_iota((tm, LANES)).astype(F32)
    big = float(LANES)

    def first_argmax(v):
        mx = jnp.max(v, axis=1, keepdims=True)
        return mx, jnp.min(jnp.where(v == mx, lane, big), axis=1, keepdims=True)

    is_group = lane < float(MOE_GROUPS)
    lg = jnp.where(is_group, logits, _NEG_INF)
    mg, gi = first_argmax(lg)
    wg = 1.0 / jnp.sum(jnp.where(is_group, jnp.exp(lg - mg), 0.0), axis=1, keepdims=True)
    e0 = float(MOE_GROUPS) + float(MOE_EPG) * gi
    in_group = (lane >= e0) & (lane < e0 + float(MOE_EPG))
    le = jnp.where(in_group, logits, _NEG_INF)
    m1, i1 = first_argmax(le)
    m2, i2 = first_argmax(jnp.where(lane == i1, _NEG_INF, le))
    r = jnp.exp(m2 - m1)
    g1 = wg / (1.0 + r)
    g2 = wg * r / (1.0 + r)
    a = jnp.minimum(i1, i2) - e0
    b = jnp.maximum(i1, i2) - e0
    ga = jnp.where(i1 < i2, g1, g2)
    gb = jnp.where(i1 < i2, g2, g1)
    pair = a * (7.0 - a) * 0.5 + (b - a - 1.0)
    cls = gi * 6.0 + pair
    onehot = jnp.where(lane == cls, 1.0, 0.0)
    before = jnp.dot(tril_ref[...], onehot.astype(BF16), preferred_element_type=F32) + cnt_scr[...]
    rank = jnp.sum(onehot * before, axis=1, keepdims=True)
    cnt_scr[...] = cnt_scr[...] + jnp.sum(onehot, axis=0, keepdims=True)
    route_ref[...] = jnp.where(lane == 0.0, cls, jnp.where(lane == 1.0, ga, jnp.where(lane == 2.0, gb,
                               jnp.where(lane == 3.0, rank, 0.0))))
    cnt_ref[...] = cnt_scr[...]


def _router(h, g, wr, tm):
    n = h.shape[0]
    tril = (jnp.arange(tm)[:, None] > jnp.arange(tm)[None, :]).astype(BF16)
    row = lambda i: (i, 0)
    const = lambda i: (0, 0)
    return pl.pallas_call(
        _router_kernel,
        grid=(n // tm,),
        in_specs=[pl.BlockSpec((tm, D_MODEL), row), pl.BlockSpec((1, D_MODEL), const),
                  pl.BlockSpec((D_MODEL, LANES), const), pl.BlockSpec((tm, tm), const)],
        out_specs=[pl.BlockSpec((tm, LANES), row), pl.BlockSpec((1, LANES), const)],
        out_shape=[jax.ShapeDtypeStruct((n, LANES), F32), jax.ShapeDtypeStruct((1, LANES), F32)],
        scratch_shapes=[pltpu.VMEM((1, LANES), F32)],
        compiler_params=_cparams("arbitrary"),
        name="moe_router",
    )(h, g, wr, tril)


XS_WIDTH = D_MODEL + LANES


def _dispatch_kernel(pos_ref, h_ref, route_ref, init_ref, xs_ref, buf, sem):
    del init_ref
    i = pl.program_id(0)
    tr = h_ref.shape[0]
    buf[:, :D_MODEL] = h_ref[...]
    buf[:, D_MODEL:] = route_ref[...]

    def row_copy(r):
        return pltpu.make_async_copy(buf.at[pl.ds(r, 1), :], xs_ref.at[pl.ds(pos_ref[i * tr + r], 1), :], sem)

    def start(r, c):
        row_copy(r).start()
        return c

    def wait(r, c):
        row_copy(r).wait()
        return c

    lax.fori_loop(0, tr, start, 0)
    lax.fori_loop(0, tr, wait, 0)


def _dispatch(pos, h, route, n_rows, tr):
    n = h.shape[0]
    return pl.pallas_call(
        _dispatch_kernel,
        grid_spec=pltpu.PrefetchScalarGridSpec(
            num_scalar_prefetch=1,
            grid=(n // tr,),
            in_specs=[pl.BlockSpec((tr, D_MODEL), lambda i, pos: (i, 0)), pl.BlockSpec((tr, LANES), lambda i, pos: (i, 0)),
                      pl.BlockSpec(memory_space=pl.ANY)],
            out_specs=pl.BlockSpec(memory_space=pl.ANY),
            scratch_shapes=[pltpu.VMEM((tr, XS_WIDTH), F32), pltpu.SemaphoreType.DMA(())],
        ),
        out_shape=jax.ShapeDtypeStruct((n_rows, XS_WIDTH), F32),
        input_output_aliases={3: 0},
        compiler_params=_cparams("arbitrary"),
        name="moe_dispatch",
    )(pos, h, route, jnp.zeros((n_rows, XS_WIDTH), F32))


def _ffn_kernel(ea_ref, eb_ref, valid_ref, xs_ref, g_ref, wga_ref, wua_ref, wda_ref, wgb_ref, wub_ref, wdb_ref, ys_ref):
    i = pl.program_id(0)

    @pl.when(valid_ref[i] == 0)
    def _():
        ys_ref[...] = jnp.zeros(ys_ref.shape, F32)

    @pl.when(valid_ref[i] != 0)
    def _():
        x = xs_ref[:, :D_MODEL]
        xb = _rmsnorm(x, g_ref[...]).astype(BF16)
        y = x
        for gate_lane, wg_ref, wu_ref, wd_ref in ((1, wga_ref, wua_ref, wda_ref), (2, wgb_ref, wub_ref, wdb_ref)):
            gate = xs_ref[:, D_MODEL + gate_lane:D_MODEL + gate_lane + 1]
            hid = jax.nn.silu(jnp.dot(xb, wg_ref[0], preferred_element_type=F32)) * jnp.dot(xb, wu_ref[0], preferred_element_type=F32)
            y = y + jnp.dot((hid * gate).astype(BF16), wd_ref[0], preferred_element_type=F32)
        ys_ref[...] = y


def _ffn(ea, eb, valid, xs, g, wg, wu, wd):
    n_tiles = xs.shape[0] // MOE_TILE
    tile = lambda i, ea, eb, v: (i, 0)
    const = lambda i, ea, eb, v: (0, 0)
    up = lambda sel: pl.BlockSpec((1, D_MODEL, EXPERT_FF), sel)
    down = lambda sel: pl.BlockSpec((1, EXPERT_FF, D_MODEL), sel)
    sel_a = lambda i, ea, eb, v: (ea[i], 0, 0)
    sel_b = lambda i, ea, eb, v: (eb[i], 0, 0)
    return pl.pallas_call(
        _ffn_kernel,
        grid_spec=pltpu.PrefetchScalarGridSpec(
            num_scalar_prefetch=3,
            grid=(n_tiles,),
            in_specs=[pl.BlockSpec((MOE_TILE, XS_WIDTH), tile), pl.BlockSpec((1, D_MODEL), const),
                      up(sel_a), up(sel_a), down(sel_a), up(sel_b), up(sel_b), down(sel_b)],
            out_specs=pl.BlockSpec((MOE_TILE, D_MODEL), tile),
        ),
        out_shape=jax.ShapeDtypeStruct((xs.shape[0], D_MODEL), F32),
        compiler_params=_cparams("arbitrary"),
        name="moe_ffn",
    )(ea, eb, valid, xs, g, wg, wu, wd, wg, wu, wd)


def _combine_kernel(pos_ref, ys_ref, o_ref, sem):
    i = pl.program_id(0)
    tr = o_ref.shape[0]

    def row_copy(r):
        return pltpu.make_async_copy(ys_ref.at[pl.ds(pos_ref[i * tr + r], 1), :], o_ref.at[pl.ds(r, 1), :], sem)

    def start(r, c):
        row_copy(r).start()
        return c

    def wait(r, c):
        row_copy(r).wait()
        return c

    lax.fori_loop(0, tr, start, 0)
    lax.fori_loop(0, tr, wait, 0)


def _combine(pos, ys, n, tr):
    return pl.pallas_call(
        _combine_kernel,
        grid_spec=pltpu.PrefetchScalarGridSpec(
            num_scalar_prefetch=1,
            grid=(n // tr,),
            in_specs=[pl.BlockSpec(memory_space=pl.ANY)],
            out_specs=pl.BlockSpec((tr, D_MODEL), lambda i, pos: (i, 0)),
            scratch_shapes=[pltpu.SemaphoreType.DMA(())],
        ),
        out_shape=jax.ShapeDtypeStruct((n, D_MODEL), F32),
        compiler_params=_cparams("arbitrary"),
        name="moe_combine",
    )(pos, ys)


_PAIR_A = (0, 0, 0, 1, 1, 2)
_PAIR_B = (1, 2, 3, 2, 3, 3)


def _hier_moe(h, g, wr, wg, wu, wd):
    n = h.shape[0]
    tile = MOE_TILE
    route, cnt = _router(h, g, wr, tile)
    cls = route[:, 0].astype(I32)
    rank = route[:, 3].astype(I32)
    counts = cnt[0, :N_CLASSES].astype(I32)
    tiles_c = (counts + tile - 1) // tile
    tile_end = jnp.cumsum(tiles_c)
    tile_start = tile_end - tiles_c
    pos = tile_start[cls] * tile + rank
    n_tiles = n // tile + N_CLASSES
    ti = jnp.arange(n_tiles, dtype=I32)
    total = tile_end[-1]
    valid = (ti < total).astype(I32)
    tcls = jnp.searchsorted(tile_end, jnp.minimum(ti, total - 1), side="right").astype(I32)
    tcls = jnp.minimum(tcls, N_CLASSES - 1)
    grp, pair = tcls // 6, tcls % 6
    ea = grp * MOE_EPG + jnp.asarray(_PAIR_A, I32)[pair]
    eb = grp * MOE_EPG + jnp.asarray(_PAIR_B, I32)[pair]
    xs = _dispatch(pos, h, route, n_tiles * tile, tile)
    ys = _ffn(ea, eb, valid, xs, g, wg, wu, wd)
    return _combine(pos, ys, n, tile)


def _odd_inproj_kernel(x_ref, g_ref, wz_ref, wx_ref, wdt_ref, z_ref, xbc_ref, dt_ref):
    xb = _rmsnorm(x_ref[...], g_ref[...]).astype(BF16)
    z_ref[...] = jnp.dot(xb, wz_ref[...], preferred_element_type=F32)
    xbc_ref[...] = jnp.dot(xb, wx_ref[...], preferred_element_type=F32)
    dt_ref[...] = jnp.dot(xb, wdt_ref[...], preferred_element_type=F32)


def _odd_inproj(x, g, wz, wx, wdt, tm):
    n = x.shape[0]
    row = lambda i: (i, 0)
    const = lambda i: (0, 0)
    widths = (D_INNER, CONV_DIM, LANES)
    return pl.pallas_call(
        _odd_inproj_kernel,
        grid=(n // tm,),
        in_specs=[pl.BlockSpec((tm, D_MODEL), row), pl.BlockSpec((1, D_MODEL), const),
                  pl.BlockSpec(wz.shape, const), pl.BlockSpec(wx.shape, const), pl.BlockSpec(wdt.shape, const)],
        out_specs=[pl.BlockSpec((tm, c), row) for c in widths],
        out_shape=[jax.ShapeDtypeStruct((n, c), F32) for c in widths],
        compiler_params=_cparams("parallel"),
        name="odd_inproj",
    )(x, g, wz, wx, wdt)


CONV_COLS = 512


def _conv_kernel(xbc_ref, dt_ref, cs_ref, w_ref, b_ref, dtb_ref, xc_ref, dts_ref, cso_ref, carry):
    c = pl.program_id(1)
    tt = xbc_ref.shape[0]
    tail = CONV_W - 1

    @pl.when(c == 0)
    def _():
        carry[...] = jnp.zeros(carry.shape, F32)
        carry[8 - tail:8, :] = cs_ref[0]

    for j in range(CONV_DIM // CONV_COLS):
        cols = slice(j * CONV_COLS, (j + 1) * CONV_COLS)
        x = xbc_ref[:, cols]
        full = jnp.concatenate([carry[:, cols], x], axis=0)
        acc = b_ref[:, cols] + x * w_ref[tail:tail + 1, cols]
        for k in range(tail):
            acc = acc + full[8 - tail + k:8 - tail + k + tt, :] * w_ref[k:k + 1, cols]
        xc_ref[:, cols] = jax.nn.silu(acc)
        new_tail = full[8 + tt - tail:8 + tt, :]
        carry[8 - tail:8, cols] = new_tail
        cso_ref[0, :, cols] = new_tail
    dts_ref[...] = jax.nn.softplus(dt_ref[...] + dtb_ref[...])


def _conv(xbc, dt, conv_state, w, b, dtb, n_batch, n_tok, tt):
    n_chunks = n_tok // tt
    blk = lambda i, c: (i * n_chunks + c, 0)
    const = lambda i, c: (0, 0)
    st = lambda i, c: (i, 0, 0)
    return pl.pallas_call(
        _conv_kernel,
        grid=(n_batch, n_chunks),
        in_specs=[pl.BlockSpec((tt, CONV_DIM), blk), pl.BlockSpec((tt, LANES), blk),
                  pl.BlockSpec((1, CONV_W - 1, CONV_DIM), st),
                  pl.BlockSpec((CONV_W, CONV_DIM), const), pl.BlockSpec((1, CONV_DIM), const), pl.BlockSpec((1, LANES), const)],
        out_specs=[pl.BlockSpec((tt, CONV_DIM), blk), pl.BlockSpec((tt, LANES), blk),
                   pl.BlockSpec((1, CONV_W - 1, CONV_DIM), st)],
        out_shape=[jax.ShapeDtypeStruct(xbc.shape, F32), jax.ShapeDtypeStruct(dt.shape, F32),
                   jax.ShapeDtypeStruct(conv_state.shape, F32)],
        scratch_shapes=[pltpu.VMEM((8, CONV_DIM), F32)],
        compiler_params=_cparams("parallel", "arbitrary"),
        name="ssd_conv",
    )(xbc, dt, conv_state, w, b, dtb)


def _ssd_kernel(xs_ref, bm_ref, cm_ref, z_ref, dt_ref, s0_ref, a_ref, dsk_ref, ng_ref, tril_ref, mask_ref, elast_ref,
                o_ref, sout_ref, s_scr, yz_scr, *, n_seq, seq_len, gps):
    c = pl.program_id(2)

    @pl.when(c == 0)
    def _():
        s_scr[...] = s0_ref[...]

    lane = _lane_iota((ROWS, LANES))
    row = _row_iota((ROWS, LANES))
    lo_half = lane < (LANES // 2)
    dt = dt_ref[...]
    cs = _dot01(tril_ref[...], dt * a_ref[...])
    cs_last = _dot01(elast_ref[...], cs)
    if gps != SSM_GROUPS:
        shift = (LANES - pl.program_id(1) * (gps * SSM_HEADS // SSM_GROUPS)) % LANES
        dt, cs, cs_last = (pltpu.roll(v, shift, 1) for v in (dt, cs, cs_last))
    cs_t = cs.T
    ecs = jnp.exp(cs)
    dt_tail = dt * jnp.exp(cs_last - cs)
    causal = mask_ref[...] > 0.0

    def per_head(v, h0):
        return jnp.where(lo_half, v[:, h0:h0 + 1], v[:, h0 + 1:h0 + 2])

    for g in range(gps):
        gcols = slice(g * D_STATE, (g + 1) * D_STATE)
        cg = cm_ref[:, gcols].astype(BF16)
        bg = bm_ref[:, gcols].astype(BF16)
        cb = _dot_nt(cg, bg)
        for pp in range(2):
            pair = 2 * g + pp
            h0 = 2 * pair
            pcols = slice(pair * LANES, (pair + 1) * LANES)
            x2 = xs_ref[:, pcols]
            xdt = x2 * per_head(dt, h0)
            ys = []
            for e in range(2):
                h = h0 + e
                seg = cs[:, h:h + 1] - cs_t[h:h + 1, :]
                wmat = jnp.exp(jnp.where(causal, seg, _NEG_INF)) * cb
                ys.append(_dot(wmat, xdt))
            y2 = jnp.where(lo_half, ys[0], ys[1])
            if n_seq == 1:
                inter = _dot_nt(cg, s_scr[0, pair])
            else:
                inter = jnp.zeros((ROWS, LANES), F32)
                for s in range(n_seq):
                    in_seq = (row >= s * seq_len) & (row < (s + 1) * seq_len)
                    inter = inter + jnp.where(in_seq, _dot_nt(cg, s_scr[s, pair]), 0.0)
            y2 = y2 + inter * per_head(ecs, h0) + dsk_ref[:, pcols] * x2
            yz_scr[:, pcols] = y2 * jax.nn.silu(z_ref[:, pcols])
            xw_t = (x2 * per_head(dt_tail, h0)).T
            for s in range(n_seq):
                last = (s + 1) * seq_len - 1
                if n_seq == 1:
                    xt = xw_t
                else:
                    xt = jnp.where((lane >= s * seq_len) & (lane <= last), xw_t, 0.0)
                dec = jnp.exp(jnp.where(row < (LANES // 2), cs_t[h0:h0 + 1, last:last + 1], cs_t[h0 + 1:h0 + 2, last:last + 1]))
                s_scr[s, pair] = s_scr[s, pair] * dec + _dot(xt, bg)
    gw = D_INNER // SSM_GROUPS
    for g in range(gps):
        cols = slice(g * gw, (g + 1) * gw)
        o_ref[:, cols] = _rmsnorm(yz_scr[:, cols], ng_ref[:, cols])

    @pl.when(c == pl.num_programs(2) - 1)
    def _():
        sout_ref[...] = s_scr[...]


def _ssd(xc, z, dts, s0, a_row, dsk_row, ng_row, n_batch, n_tok):
    seq_len = min(ROWS, n_tok)
    n_seq = ROWS // seq_len
    n_chunks = n_tok // seq_len
    r = jnp.arange(ROWS)
    li, sid = r % seq_len, r // seq_len
    mask = ((sid[:, None] == sid[None, :]) & (li[:, None] >= li[None, :]))
    elast = (r[None, :] == (sid * seq_len + seq_len - 1)[:, None])
    n_pairs = SSM_HEADS // 2
    s0p = s0.reshape(n_batch, n_pairs, LANES, D_STATE)
    gps = SSM_GROUPS if n_seq == 1 else 1
    xw = gps * (D_INNER // SSM_GROUPS)
    bw = gps * D_STATE
    rows = lambda i, j, c: i * n_chunks + c
    xblk = lambda i, j, c: (rows(i, j, c), j)
    bblk = lambda i, j, c: (rows(i, j, c), D_INNER // bw + j)
    cblk = lambda i, j, c: (rows(i, j, c), (D_INNER + SSM_GROUPS * D_STATE) // bw + j)
    dtblk = lambda i, j, c: (rows(i, j, c), 0)
    st = lambda i, j, c: (i, j, 0, 0)
    const = lambda i, j, c: (0, 0)
    gconst = lambda i, j, c: (0, j)
    state_block = (n_seq, 2 * gps, LANES, D_STATE)
    o, s_out = pl.pallas_call(
        functools.partial(_ssd_kernel, n_seq=n_seq, seq_len=seq_len, gps=gps),
        grid=(n_batch // n_seq, SSM_GROUPS // gps, n_chunks),
        in_specs=[pl.BlockSpec((ROWS, xw), xblk), pl.BlockSpec((ROWS, bw), bblk), pl.BlockSpec((ROWS, bw), cblk),
                  pl.BlockSpec((ROWS, xw), xblk), pl.BlockSpec((ROWS, LANES), dtblk),
                  pl.BlockSpec(state_block, st),
                  pl.BlockSpec((1, LANES), const), pl.BlockSpec((1, xw), gconst), pl.BlockSpec((1, xw), gconst),
                  pl.BlockSpec((ROWS, ROWS), const), pl.BlockSpec((ROWS, ROWS), const), pl.BlockSpec((ROWS, ROWS), const)],
        out_specs=[pl.BlockSpec((ROWS, xw), xblk), pl.BlockSpec(state_block, st)],
        out_shape=[jax.ShapeDtypeStruct((n_batch * n_tok, D_INNER), F32), jax.ShapeDtypeStruct(s0p.shape, F32)],
        scratch_shapes=[pltpu.VMEM(state_block, F32), pltpu.VMEM((ROWS, xw), F32)],
        compiler_params=_cparams("parallel", "parallel", "arbitrary"),
        name="ssd_scan",
    )(xc, xc, xc, z, dts, s0p, a_row, dsk_row, ng_row, mask.astype(BF16), mask.astype(F32), elast.astype(BF16))
    return o, s_out.reshape(s0.shape)


def _rope_tables(pos):
    half = MOBA_HD // 2
    inv = jnp.power(ROPE_THETA, -jnp.arange(half, dtype=F32) * 2.0 / MOBA_HD)
    ang = pos[:, None] * inv[None, :]
    cos, sin = jnp.cos(ang), jnp.sin(ang)
    reps = LANES // MOBA_HD
    return jnp.tile(jnp.concatenate([cos, cos], axis=1), (1, reps)), jnp.tile(jnp.concatenate([-sin, sin], axis=1), (1, reps))


def _pad_cols(w, width):
    return jnp.pad(w, ((0, 0), (0, width - w.shape[1])))


TOKEN_TILE = 256


def _even_layer(h, nb, nt, pos, ret0, moba_fn, norm_g, w_in, w_out, q_g, k_g):
    tm = TOKEN_TILE
    cos, sin = _rope_tables(pos)
    if nt < tm:
        cos, sin = jnp.tile(cos, (tm // nt, 1)), jnp.tile(sin, (tm // nt, 1))
    head_tile = lambda v: jnp.tile(v, MOBA_HEADS)[None, :]
    width = MOBA_HEADS * MOBA_HD
    blockdiag = (jnp.arange(width)[:, None] // MOBA_HD == jnp.arange(width)[None, :] // MOBA_HD).astype(BF16)
    rq, rk, rv, rg, mq, mk, mv = _even_inproj(h, norm_g[None, :], w_in.astype(BF16), cos, sin,
                                              head_tile(q_g), head_tile(k_g), blockdiag, tm)
    mo = moba_fn(mq, mk, mv)
    ro, ret_state = _retention(rq, rk, rv, rg, ret0, nb, nt)
    w_out = w_out.astype(BF16)
    h = _outproj(h, (ro, mo), (w_out[:RET_HEADS * RET_DV], w_out[RET_HEADS * RET_DV:]), tm)
    return h, ret_state, mk, mv


def _odd_layer(h, nb, nt, conv0, ssm0, norm_g, w_in, conv_w, conv_b, dt_bias, a_log, d_skip, ssd_norm, w_out):
    tm = TOKEN_TILE
    w_in = w_in.astype(BF16)
    z, xbc, dt = _odd_inproj(h, norm_g[None, :], w_in[:, :D_INNER], w_in[:, D_INNER:D_INNER + CONV_DIM],
                             _pad_cols(w_in[:, D_INNER + CONV_DIM:], LANES), tm)
    xc, dts, conv_state = _conv(xbc, dt, conv0, conv_w, conv_b[None, :], _pad_cols(dt_bias[None, :], LANES),
                                nb, nt, min(nt, 256))
    a_row = _pad_cols(-jnp.exp(a_log)[None, :], LANES)
    dsk_row = jnp.repeat(d_skip, D_INNER // SSM_HEADS)[None, :]
    yzn, ssm_state = _ssd(xc, z, dts, ssm0, a_row, dsk_row, ssd_norm[None, :], nb, nt)
    h = _outproj(h, (yzn,), (w_out.astype(BF16),), tm)
    return h, conv_state, ssm_state


def _moe_layer(h, norm_g, w_rg, w_re, w_g, w_u, w_d):
    wr = _pad_cols(jnp.concatenate([w_rg, w_re], axis=1), LANES)
    return _hier_moe(h, norm_g[None, :], wr, w_g.astype(BF16), w_u.astype(BF16), w_d.astype(BF16))


def kernel(x_prompt, x_sample, state_ret, cache_k, cache_v, page_table, state_ssm, state_conv, norm_mix, norm_ffn,
           w_in_even, w_out_even, q_norm, k_norm, w_in_odd, conv_w, conv_b, dt_bias, a_log, d_skip, ssd_norm, w_out_odd,
           w_router_group, w_router_expert, w_expert_gate, w_expert_up, w_expert_down):
    bp, seq, _ = x_prompt.shape
    bs, dec_seq, _ = x_sample.shape
    past_len = page_table.shape[1] * PAGE_SIZE
    groups = ((x_prompt.reshape(bp * seq, D_MODEL), bp, seq, jnp.arange(seq, dtype=F32)),
              (x_sample.reshape(bs * dec_seq, D_MODEL), bs, dec_seq, past_len + jnp.arange(dec_seq, dtype=F32)))

    def moe(h, layer):
        return _moe_layer(h, norm_ffn[layer], w_router_group[layer], w_router_expert[layer],
                          w_expert_gate[layer], w_expert_up[layer], w_expert_down[layer])

    outs = []
    for gi, (h, nb, nt, pos) in enumerate(groups):
        if gi == 0:
            ret0 = jnp.zeros((nb, RET_HEADS, RET_DK, RET_DV), F32)
            moba = functools.partial(_moba_prompt, n_batch=nb, n_tok=nt)
            conv0 = jnp.zeros((nb, CONV_W - 1, CONV_DIM), F32)
            ssm0 = jnp.zeros((nb, SSM_HEADS, D_INNER // SSM_HEADS, D_STATE), F32)
        else:
            ret0 = state_ret[0]
            moba = functools.partial(_moba_sample, pool_k=cache_k[0], pool_v=cache_v[0], page_table=page_table,
                                     n_seq=nb, n_tok=nt)
            conv0, ssm0 = state_conv[0], state_ssm[0]
        h, ret_state, mk, mv = _even_layer(h, nb, nt, pos, ret0, moba, norm_mix[0], w_in_even[0], w_out_even[0],
                                           q_norm[0], k_norm[0])
        h = moe(h, 0)
        h, conv_state, ssm_state = _odd_layer(h, nb, nt, conv0, ssm0, norm_mix[1], w_in_odd[0], conv_w[0], conv_b[0],
                                              dt_bias[0], a_log[0], d_skip[0], ssd_norm[0], w_out_odd[0])
        h = moe(h, 1)
        outs.append(dict(h=h.reshape(nb, nt, D_MODEL), ret=ret_state[None],
                         k=mk.reshape(1, nb, nt, MOBA_HEADS, MOBA_HD), v=mv.reshape(1, nb, nt, MOBA_HEADS, MOBA_HD),
                         ssm=ssm_state[None], conv=conv_state[None]))
    p, s = outs
    return (p["h"], s["h"], p["ret"], s["ret"], p["k"], p["v"], s["k"], s["v"], p["ssm"], s["ssm"], p["conv"], s["conv"])
```

```python
import functools
import math

import jax
import jax.numpy as jnp
from jax import lax
from jax.experimental import pallas as pl
from jax.experimental.pallas import tpu as pltpu

F32 = jnp.float32
BF16 = jnp.bfloat16
I32 = jnp.int32

D_MODEL = 1024
PAGE_SIZE = 128
RET_HEADS = 8
RET_DK = 64
RET_DV = 128
MOBA_HEADS = 8
MOBA_HD = 64
MOBA_BLOCK = 256
MOBA_TOPK = 3
ROPE_THETA = 10000.0
D_INNER = 2048
SSM_HEADS = 32
SSM_GROUPS = 8
D_STATE = 128
CONV_W = 4
CONV_DIM = 4096
MOE_GROUPS = 4
MOE_EPG = 4
N_EXPERTS = 16
EXPERT_FF = 512
NORM_EPS = 1e-6
N_CLASSES = MOE_GROUPS * 6
LANES = 128
ROWS = 128
MOE_TILE = 256
VMEM_LIMIT = 56 * 1024 * 1024

_NEG_INF = float("-inf")


def _cparams(*sem):
    return pltpu.CompilerParams(dimension_semantics=sem, vmem_limit_bytes=VMEM_LIMIT)


def _dot(a, b):
    return jnp.dot(a.astype(BF16), b.astype(BF16), preferred_element_type=F32)


def _dot_nt(a, b):
    return lax.dot_general(a.astype(BF16), b.astype(BF16), (((1,), (1,)), ((), ())), preferred_element_type=F32)


def _split2(x):
    hi = x.astype(BF16)
    lo = (x - hi.astype(F32)).astype(BF16)
    return hi, lo


def _split3(x):
    hi = x.astype(BF16)
    r = x - hi.astype(F32)
    mid = r.astype(BF16)
    lo = (r - mid.astype(F32)).astype(BF16)
    return hi, mid, lo


def _dot3(a, b):
    ah, al = _split2(a)
    bh, bl = _split2(b)
    return (jnp.dot(ah, bh, preferred_element_type=F32) + jnp.dot(ah, bl, preferred_element_type=F32)
            + jnp.dot(al, bh, preferred_element_type=F32))


def _dot3_nt(a, b):
    dn = (((1,), (1,)), ((), ()))
    ah, al = _split2(a)
    bh, bl = _split2(b)
    return (lax.dot_general(ah, bh, dn, preferred_element_type=F32) + lax.dot_general(ah, bl, dn, preferred_element_type=F32)
            + lax.dot_general(al, bh, dn, preferred_element_type=F32))


def _dot01(m01, x):
    hi, mid, lo = _split3(x)
    return (jnp.dot(m01, hi, preferred_element_type=F32) + jnp.dot(m01, mid, preferred_element_type=F32)
            + jnp.dot(m01, lo, preferred_element_type=F32))


def _rmsnorm(x, g=None):
    y = x * lax.rsqrt(jnp.mean(x * x, axis=-1, keepdims=True) + NORM_EPS)
    return y if g is None else y * g


def _lane_iota(shape):
    return lax.broadcasted_iota(I32, shape, len(shape) - 1)


def _row_iota(shape):
    return lax.broadcasted_iota(I32, shape, len(shape) - 2)


def _even_inproj_kernel(x_ref, g_ref, w_ref, cos_ref, sin_ref, qg_ref, kg_ref, bd_ref,
                        rq_ref, rk_ref, rv_ref, rg_ref, *moba_refs, key_minor):
    xb = _rmsnorm(x_ref[...], g_ref[...]).astype(BF16)
    cos = cos_ref[...]
    sin = sin_ref[...]
    first_half = (_lane_iota(cos.shape) % MOBA_HD) < (MOBA_HD // 2)

    def proj(c0, c1):
        return jnp.dot(xb, w_ref[:, c0:c1], preferred_element_type=F32)

    def rope(a):
        outs = []
        for j in range(a.shape[1] // LANES):
            s = a[:, j * LANES:(j + 1) * LANES]
            rot = jnp.where(first_half, pltpu.roll(s, LANES - MOBA_HD // 2, 1), pltpu.roll(s, MOBA_HD // 2, 1))
            outs.append(s * cos + rot * sin)
        return jnp.concatenate(outs, axis=1)

    def qk_norm(a, gam):
        ss = jnp.dot((a * a).astype(BF16), bd_ref[...], preferred_element_type=F32)
        return a * lax.rsqrt(ss * (1.0 / MOBA_HD) + NORM_EPS) * gam

    rq_ref[...] = rope(proj(0, 512))
    rk_ref[...] = rope(proj(512, 1024)) * (RET_DK ** -0.5)
    rv_ref[...] = proj(1024, 2048)
    rg_ref[...] = proj(2048, 3072)
    mq = rope(qk_norm(proj(3072, 3584), qg_ref[...]))
    mk = rope(qk_norm(proj(3584, 4096), kg_ref[...]))
    mv = proj(4096, 4608)
    if not key_minor:
        mq_ref, mk_ref, mv_ref = moba_refs
        mq_ref[...] = mq
        mk_ref[...] = mk
        mv_ref[...] = mv
    else:
        qt_ref, kb_ref, kt_ref, vt_ref, vtb_ref, kbar_ref = moba_refs
        qt_ref[0] = mq.T
        kb_ref[...] = mk.astype(BF16)
        kt_ref[0] = mk.T
        vt = mv.T
        vt_ref[0] = vt
        vtb_ref[0] = vt.astype(BF16)
        kbar_ref[0] = jnp.mean(mk, axis=0, keepdims=True)


def _even_inproj(x, g, w, cos, sin, qg, kg, bd, tm, n_tok, key_minor):
    n = x.shape[0]
    n_pos_blocks = cos.shape[0] // tm
    row = lambda i: (i, 0)
    const = lambda i: (0, 0)
    width = MOBA_HEADS * MOBA_HD
    out_specs = [pl.BlockSpec((tm, c), row) for c in (512, 512, 1024, 1024)]
    out_shape = [jax.ShapeDtypeStruct((n, c), F32) for c in (512, 512, 1024, 1024)]
    if not key_minor:
        out_specs += [pl.BlockSpec((tm, width), row)] * 3
        out_shape += [jax.ShapeDtypeStruct((n, width), F32)] * 3
    else:
        assert tm == MOBA_BLOCK and n_tok % tm == 0
        tiles = n_tok // tm
        tspec = pl.BlockSpec((1, width, tm), lambda i: (i // tiles, 0, i % tiles))
        tshape = lambda dt: jax.ShapeDtypeStruct((n // n_tok, width, n_tok), dt)
        out_specs += [tspec, pl.BlockSpec((tm, width), row), tspec, tspec, tspec, pl.BlockSpec((1, 1, width), lambda i: (i, 0, 0))]
        out_shape += [tshape(F32), jax.ShapeDtypeStruct((n, width), BF16), tshape(F32), tshape(F32), tshape(BF16),
                      jax.ShapeDtypeStruct((n // tm, 1, width), F32)]
    return pl.pallas_call(
        functools.partial(_even_inproj_kernel, key_minor=key_minor),
        grid=(n // tm,),
        in_specs=[pl.BlockSpec((tm, D_MODEL), row), pl.BlockSpec((1, D_MODEL), const),
                  pl.BlockSpec(w.shape, const),
                  pl.BlockSpec((tm, LANES), lambda i: (i % n_pos_blocks, 0)),
                  pl.BlockSpec((tm, LANES), lambda i: (i % n_pos_blocks, 0)),
                  pl.BlockSpec((1, 512), const), pl.BlockSpec((1, 512), const), pl.BlockSpec((512, 512), const)],
        out_specs=out_specs,
        out_shape=out_shape,
        compiler_params=_cparams("parallel"),
        name="even_inproj",
    )(x, g, w, cos, sin, qg, kg, bd)


def _retention_kernel(rq_ref, rk_ref, rv_ref, rg_ref, s0_ref, dmat_ref, qdec_ref, kdec_ref, sdec_ref,
                      o_ref, sout_ref, s_scr, *, n_seq, seq_len):
    c = pl.program_id(1)

    @pl.when(c == 0)
    def _():
        s_scr[...] = s0_ref[...]

    lane = _lane_iota((ROWS, LANES))
    row = _row_iota((ROWS, LANES))
    for p in range(RET_HEADS // 2):
        q2 = rq_ref[:, p * LANES:(p + 1) * LANES]
        k2 = rk_ref[:, p * LANES:(p + 1) * LANES]
        kd_t = (k2 * kdec_ref[p]).T
        v_pair = []
        for e in range(2):
            h = 2 * p + e
            head_lanes = (lane < RET_DK) if e == 0 else (lane >= RET_DK)
            qm = jnp.where(head_lanes, q2, 0.0)
            scores = _dot_nt(qm, k2) * dmat_ref[h]
            vh = rv_ref[:, h * RET_DV:(h + 1) * RET_DV]
            v_pair.append(vh)
            if n_seq == 1:
                inter = _dot(qm, s_scr[0, p])
            else:
                inter = jnp.zeros((ROWS, RET_DV), F32)
                for s in range(n_seq):
                    in_seq = (row >= s * seq_len) & (row < (s + 1) * seq_len)
                    inter = inter + jnp.where(in_seq, _dot(qm, s_scr[s, p]), 0.0)
            o = _dot(scores, vh) + inter * qdec_ref[h]
            gate = rg_ref[:, h * RET_DV:(h + 1) * RET_DV]
            o_ref[:, h * RET_DV:(h + 1) * RET_DV] = jax.nn.silu(gate) * _rmsnorm(o)
        for s in range(n_seq):
            if n_seq == 1:
                kt = kd_t
            else:
                kt = jnp.where((lane >= s * seq_len) & (lane < (s + 1) * seq_len), kd_t, 0.0)
            upd = jnp.where(row < RET_DK, _dot(kt, v_pair[0]), _dot(kt, v_pair[1]))
            s_scr[s, p] = s_scr[s, p] * sdec_ref[p] + upd

    @pl.when(c == pl.num_programs(1) - 1)
    def _():
        sout_ref[...] = s_scr[...]


def _retention_tables(seq_len):
    log_g = jnp.log1p(-jnp.exp2(-5.0 - jnp.arange(RET_HEADS, dtype=F32)))
    r = jnp.arange(ROWS)
    li = (r % seq_len).astype(F32)
    sid = r // seq_len
    diff = li[:, None] - li[None, :]
    same = sid[:, None] == sid[None, :]
    dmat = jnp.where(same[None] & (diff[None] >= 0),
                     jnp.exp(jnp.maximum(diff, 0.0)[None] * log_g[:, None, None]), 0.0)
    qdec = jnp.broadcast_to(jnp.exp((li + 1.0)[None, :, None] * log_g[:, None, None]), (RET_HEADS, ROWS, RET_DV))
    lane_head = jnp.arange(LANES) // RET_DK
    pair_log_g = log_g.reshape(RET_HEADS // 2, 2)[:, lane_head]
    kdec = jnp.exp((seq_len - 1.0 - li)[None, :, None] * pair_log_g[:, None, :])
    sdec = jnp.broadcast_to(jnp.exp(float(seq_len) * pair_log_g)[:, :, None], (RET_HEADS // 2, LANES, RET_DV))
    return dmat, qdec, kdec, sdec


def _retention(rq, rk, rv, rg, s0, n_batch, n_tok):
    seq_len = min(ROWS, n_tok)
    n_seq = ROWS // seq_len
    n_chunks = n_tok // seq_len
    tables = _retention_tables(seq_len)
    s0p = s0.reshape(n_batch, RET_HEADS // 2, 2 * RET_DK, RET_DV)
    blk = lambda i, c: (i * n_chunks + c, 0)
    st = lambda i, c: (i, 0, 0, 0)
    c3 = lambda i, c: (0, 0, 0)
    o, s_out = pl.pallas_call(
        functools.partial(_retention_kernel, n_seq=n_seq, seq_len=seq_len),
        grid=(n_batch // n_seq, n_chunks),
        in_specs=[pl.BlockSpec((ROWS, 512), blk), pl.BlockSpec((ROWS, 512), blk),
                  pl.BlockSpec((ROWS, 1024), blk), pl.BlockSpec((ROWS, 1024), blk),
                  pl.BlockSpec((n_seq,) + s0p.shape[1:], st)]
                 + [pl.BlockSpec(t.shape, c3) for t in tables],
        out_specs=[pl.BlockSpec((ROWS, 1024), blk), pl.BlockSpec((n_seq,) + s0p.shape[1:], st)],
        out_shape=[jax.ShapeDtypeStruct((n_batch * n_tok, 1024), F32), jax.ShapeDtypeStruct(s0p.shape, F32)],
        scratch_shapes=[pltpu.VMEM((n_seq,) + s0p.shape[1:], F32)],
        compiler_params=_cparams("parallel", "arbitrary"),
        name="retention",
    )(rq, rk, rv, rg, s0p, *tables)
    return o, s_out.reshape(s0.shape)


def _moba_prompt_kernel(qt_ref, k_ref, vt_ref, kbar_ref, o_ref, sel_scr, qtm_scr, m_scr, l_scr, acc_scr, *, n_blocks):
    qb = pl.program_id(1)
    blk = MOBA_BLOCK
    n_pairs = MOBA_HEADS // 2
    blk_row = _row_iota((n_blocks, blk))
    lo_rows = _row_iota((LANES, blk)) < MOBA_HD
    lo_lanes = _lane_iota((n_blocks, LANES)) < MOBA_HD
    key_le_query = _row_iota((blk, blk)) <= _lane_iota((blk, blk))

    for p in range(n_pairs):
        rows = slice(p * LANES, (p + 1) * LANES)
        qt2 = qt_ref[0, rows, :]
        kb2 = kbar_ref[:, rows]
        for e in range(2):
            h = 2 * p + e
            gate = _dot3(jnp.where(lo_lanes if e == 0 else ~lo_lanes, kb2, 0.0), qt2)
            cnt = jnp.zeros((n_blocks, blk), F32)
            for m in range(n_blocks - 1):
                gm = gate[m:m + 1, :]
                ahead = (gm > gate) | ((gm == gate) & (blk_row > m))
                cnt = cnt + jnp.where(ahead, (qb > m).astype(F32), 0.0)
            sel_scr[h] = jnp.where((blk_row < qb) & (cnt < float(MOBA_TOPK)), 1.0, 0.0)
            head_rows = lo_rows if e == 0 else ~lo_rows
            qtm_scr[h] = (jnp.where(head_rows, qt2, 0.0) * (MOBA_HD ** -0.5)).astype(BF16)

    def attend(n, own):
        c0 = pl.multiple_of(n * blk, blk)
        for p in range(n_pairs):
            rows = slice(p * LANES, (p + 1) * LANES)
            k2 = k_ref[pl.ds(c0, blk), rows]
            vt2 = vt_ref[0, rows, pl.ds(c0, blk)]
            pvs, alphas = [], []
            for e in range(2):
                h = 2 * p + e
                st = jnp.dot(k2, qtm_scr[h], preferred_element_type=F32)
                if own:
                    st = jnp.where(key_le_query, st, _NEG_INF)
                else:
                    st = jnp.where(sel_scr[h, pl.ds(n, 1), :] > 0.0, st, _NEG_INF)
                mx = jnp.max(st, axis=0, keepdims=True)
                if own:
                    m_new = mx
                else:
                    m_old = m_scr[h:h + 1, :]
                    m_new = jnp.maximum(m_old, mx)
                    alphas.append(jnp.exp(m_old - m_new))
                pt = jnp.exp(st - m_new)
                psum = jnp.sum(pt, axis=0, keepdims=True)
                l_scr[h:h + 1, :] = psum if own else l_scr[h:h + 1, :] * alphas[e] + psum
                m_scr[h:h + 1, :] = m_new
                pvs.append(jnp.dot(vt2, pt.astype(BF16), preferred_element_type=F32))
            pv = jnp.where(lo_rows, pvs[0], pvs[1])
            acc_scr[p] = pv if own else acc_scr[p] * jnp.where(lo_rows, alphas[0], alphas[1]) + pv

    attend(qb, True)

    def body(n, c):
        attend(n, False)
        return c

    lax.fori_loop(0, qb, body, 0)
    for p in range(n_pairs):
        den = jnp.where(lo_rows, l_scr[2 * p:2 * p + 1, :], l_scr[2 * p + 1:2 * p + 2, :])
        o_ref[:, p * LANES:(p + 1) * LANES] = (acc_scr[p] / den).T


def _moba_prompt(qt, kb, vtb, kbar, n_batch, n_tok):
    n_blocks = n_tok // MOBA_BLOCK
    width = MOBA_HEADS * MOBA_HD
    return pl.pallas_call(
        functools.partial(_moba_prompt_kernel, n_blocks=n_blocks),
        grid=(n_batch, n_blocks),
        in_specs=[pl.BlockSpec((1, width, MOBA_BLOCK), lambda b, i: (b, 0, i)),
                  pl.BlockSpec((n_tok, width), lambda b, i: (b, 0)),
                  pl.BlockSpec((1, width, n_tok), lambda b, i: (b, 0, 0)),
                  pl.BlockSpec((n_blocks, width), lambda b, i: (b, 0))],
        out_specs=pl.BlockSpec((MOBA_BLOCK, width), lambda b, i: (b * n_blocks + i, 0)),
        out_shape=jax.ShapeDtypeStruct((n_batch * n_tok, width), F32),
        scratch_shapes=[pltpu.VMEM((MOBA_HEADS, n_blocks, MOBA_BLOCK), F32),
                        pltpu.VMEM((MOBA_HEADS, LANES, MOBA_BLOCK), BF16),
                        pltpu.VMEM((MOBA_HEADS, MOBA_BLOCK), F32), pltpu.VMEM((MOBA_HEADS, MOBA_BLOCK), F32),
                        pltpu.VMEM((MOBA_HEADS // 2, LANES, MOBA_BLOCK), F32)],
        compiler_params=_cparams("parallel", "arbitrary"),
        name="moba_prompt",
    )(qt, kb, vtb, kbar)


PAGES_PER_BLOCK = MOBA_BLOCK // PAGE_SIZE
KBAR_BLOCKS_PER_STEP = 8


def _kbar_kernel(pt_ref, *refs):
    page_refs, o_ref = refs[:-1], refs[-1]
    for j in range(KBAR_BLOCKS_PER_STEP):
        tot = page_refs[j * PAGES_PER_BLOCK][0]
        for half in range(1, PAGES_PER_BLOCK):
            tot = tot + page_refs[j * PAGES_PER_BLOCK + half][0]
        o_ref[0, j] = jnp.sum(tot, axis=-1) * (1.0 / MOBA_BLOCK)


def _moba_kbar(pool_k, page_table):
    n_seq, n_pages = page_table.shape
    n_blocks = n_pages // PAGES_PER_BLOCK
    ppstep = KBAR_BLOCKS_PER_STEP * PAGES_PER_BLOCK

    def page_spec(j):
        return pl.BlockSpec((1, MOBA_HEADS, MOBA_HD, PAGE_SIZE),
                            lambda b, i, pt: (pt[b * n_pages + i * ppstep + j], 0, 0, 0))

    return pl.pallas_call(
        _kbar_kernel,
        grid_spec=pltpu.PrefetchScalarGridSpec(
            num_scalar_prefetch=1,
            grid=(n_seq, n_blocks // KBAR_BLOCKS_PER_STEP),
            in_specs=[page_spec(j) for j in range(ppstep)],
            out_specs=pl.BlockSpec((1, KBAR_BLOCKS_PER_STEP, MOBA_HEADS, MOBA_HD), lambda b, i, pt: (b, i, 0, 0)),
        ),
        out_shape=jax.ShapeDtypeStruct((n_seq, n_blocks, MOBA_HEADS, MOBA_HD), F32),
        compiler_params=_cparams("parallel", "parallel"),
        name="moba_kbar",
    )(page_table.reshape(-1), *([pool_k] * ppstep))


def _moba_select_kernel(q_ref, kbar_ref, o_ref, *, n_blocks, n_tok):
    lane = _lane_iota((n_tok, LANES))
    lane_f = lane.astype(F32)
    pad = jnp.zeros((LANES - n_blocks, LANES), F32)
    for h in range(MOBA_HEADS):
        p, e = divmod(h, 2)
        cols = slice(p * LANES, (p + 1) * LANES)
        q2 = q_ref[:, cols]
        head_lanes = (lane < MOBA_HD) if e == 0 else (lane >= MOBA_HD)
        qm = jnp.where(head_lanes, q2, 0.0)
        kb = jnp.concatenate([kbar_ref[0, :, cols], pad], axis=0)
        gate = jnp.where(lane < n_blocks, _dot3_nt(qm, kb), _NEG_INF)
        res = jnp.zeros((n_tok, LANES), F32)
        for j in range(MOBA_TOPK):
            mx = jnp.max(gate, axis=1, keepdims=True)
            idx = jnp.min(jnp.where(gate == mx, lane_f, float(LANES)), axis=1, keepdims=True)
            res = jnp.where(lane == j, idx, res)
            gate = jnp.where(lane_f == idx, _NEG_INF, gate)
        o_ref[0, h * n_tok:(h + 1) * n_tok, :] = res.astype(I32)


def _moba_select(mq, kbar, n_seq, n_tok):
    n_blocks = kbar.shape[1]
    kb = kbar.reshape(n_seq, n_blocks, 512)
    out = pl.pallas_call(
        functools.partial(_moba_select_kernel, n_blocks=n_blocks, n_tok=n_tok),
        grid=(n_seq,),
        in_specs=[pl.BlockSpec((n_tok, 512), lambda b: (b, 0)), pl.BlockSpec((1, n_blocks, 512), lambda b: (b, 0, 0))],
        out_specs=pl.BlockSpec((1, MOBA_HEADS * n_tok, LANES), lambda b: (b, 0, 0)),
        out_shape=jax.ShapeDtypeStruct((n_seq, MOBA_HEADS * n_tok, LANES), I32),
        compiler_params=_cparams("parallel"),
        name="moba_select",
    )(mq, kb)
    return out[:, :, :MOBA_TOPK].reshape(-1)


def _moba_sample_kernel(sel_ref, pt_ref, q_ref, kn_ref, vn_ref, pk_ref, pv_ref, o_ref,
                        kbuf, vbuf, sem, *, n_tok, n_pages):
    s = pl.program_id(0)
    n_steps = pl.num_programs(0)
    n_sel = n_tok * MOBA_TOPK
    past = n_sel * MOBA_BLOCK

    def copies(step, slot):
        b = step // MOBA_HEADS
        h = step % MOBA_HEADS
        out = []
        for i in range(n_sel):
            blk = sel_ref[step * n_sel + i]
            for half in range(PAGES_PER_BLOCK):
                page = pt_ref[b * n_pages + blk * PAGES_PER_BLOCK + half]
                c0 = i * MOBA_BLOCK + half * PAGE_SIZE
                out.append(pltpu.make_async_copy(pk_ref.at[page, h], kbuf.at[slot, :, pl.ds(c0, PAGE_SIZE)], sem.at[slot, 0]))
                out.append(pltpu.make_async_copy(pv_ref.at[page, h], vbuf.at[slot, :, pl.ds(c0, PAGE_SIZE)], sem.at[slot, 1]))
        return out

    slot = s % 2

    @pl.when(s == 0)
    def _():
        for cp in copies(s, slot):
            cp.start()

    @pl.when(s + 1 < n_steps)
    def _():
        for cp in copies(s + 1, 1 - slot):
            cp.start()

    for cp in copies(s, slot):
        cp.wait()

    q = q_ref[0, 0] * (MOBA_HD ** -0.5)
    s_past = _dot(q, kbuf[slot])
    key = _lane_iota((n_tok, past))
    first_key = _row_iota((n_tok, past)) * (MOBA_TOPK * MOBA_BLOCK)
    s_past = jnp.where((key >= first_key) & (key < first_key + MOBA_TOPK * MOBA_BLOCK), s_past, _NEG_INF)
    s_own = _dot_nt(q, kn_ref[0, 0])
    s_own = jnp.where(_lane_iota((n_tok, n_tok)) <= _row_iota((n_tok, n_tok)), s_own, _NEG_INF)
    mx = jnp.maximum(jnp.max(s_past, axis=1, keepdims=True), jnp.max(s_own, axis=1, keepdims=True))
    p_past = jnp.exp(s_past - mx)
    p_own = jnp.exp(s_own - mx)
    den = jnp.sum(p_past, axis=1, keepdims=True) + jnp.sum(p_own, axis=1, keepdims=True)
    o = _dot_nt(p_past, vbuf[slot]) + _dot(p_own, vn_ref[0, 0])
    o_ref[0, 0] = o / den


def _moba_sample(mq, mk, mv, pool_k, pool_v, page_table, n_seq, n_tok):
    n_pages = page_table.shape[1]
    kbar = _moba_kbar(pool_k, page_table)
    sel = _moba_select(mq, kbar, n_seq, n_tok)

    def head_major(a):
        return a.reshape(n_seq, n_tok, MOBA_HEADS, MOBA_HD).transpose(0, 2, 1, 3)

    qh, kh, vh = head_major(mq), head_major(mk), head_major(mv)
    n_sel = n_tok * MOBA_TOPK
    spec = pl.BlockSpec((1, 1, n_tok, MOBA_HD), lambda s, sel, pt: (s // MOBA_HEADS, s % MOBA_HEADS, 0, 0))
    anyspec = pl.BlockSpec(memory_space=pl.ANY)
    o = pl.pallas_call(
        functools.partial(_moba_sample_kernel, n_tok=n_tok, n_pages=n_pages),
        grid_spec=pltpu.PrefetchScalarGridSpec(
            num_scalar_prefetch=2,
            grid=(n_seq * MOBA_HEADS,),
            in_specs=[spec, spec, spec, anyspec, anyspec],
            out_specs=spec,
            scratch_shapes=[pltpu.VMEM((2, MOBA_HD, n_sel * MOBA_BLOCK), F32),
                            pltpu.VMEM((2, MOBA_HD, n_sel * MOBA_BLOCK), F32),
                            pltpu.SemaphoreType.DMA((2, 2))],
        ),
        out_shape=jax.ShapeDtypeStruct(qh.shape, F32),
        compiler_params=_cparams("arbitrary"),
        name="moba_sample",
    )(sel, page_table.reshape(-1), qh, kh, vh, pool_k, pool_v)
    return o.transpose(0, 2, 1, 3).reshape(n_seq * n_tok, MOBA_HEADS * MOBA_HD)


def _outproj_kernel(*refs, n_in):
    x_ref, a_refs, w_refs, o_ref = refs[0], refs[1:1 + n_in], refs[1 + n_in:1 + 2 * n_in], refs[-1]
    acc = x_ref[...]
    for a_ref, w_ref in zip(a_refs, w_refs):
        acc = acc + jnp.dot(a_ref[...].astype(BF16), w_ref[...], preferred_element_type=F32)
    o_ref[...] = acc


def _outproj(x, acts, ws, tm):
    n = x.shape[0]
    row = lambda i: (i, 0)
    const = lambda i: (0, 0)
    return pl.pallas_call(
        functools.partial(_outproj_kernel, n_in=len(acts)),
        grid=(n // tm,),
        in_specs=[pl.BlockSpec((tm, D_MODEL), row)] + [pl.BlockSpec((tm, a.shape[1]), row) for a in acts]
                 + [pl.BlockSpec(w.shape, const) for w in ws],
        out_specs=pl.BlockSpec((tm, D_MODEL), row),
        out_shape=jax.ShapeDtypeStruct((n, D_MODEL), F32),
        compiler_params=_cparams("parallel"),
        name="outproj",
    )(x, *acts, *ws)


def _router_kernel(h_ref, g_ref, wr_ref, tril_ref, route_ref, cnt_ref, cnt_scr):
    i = pl.program_id(0)

    @pl.when(i == 0)
    def _():
        cnt_scr[...] = jnp.zeros(cnt_scr.shape, F32)

    xn = _rmsnorm(h_ref[...], g_ref[...])
    logits = _dot3(xn, wr_ref[...])
    tm = logits.shape[0]
    lane = _lane_iota((tm, LANES)).astype(F32)
    big = float(LANES)

    def first_argmax(v):
        mx = jnp.max(v, axis=1, keepdims=True)
        return mx, jnp.min(jnp.where(v == mx, lane, big), axis=1, keepdims=True)

    is_group = lane < float(MOE_GROUPS)
    lg = jnp.where(is_group, logits, _NEG_INF)
    mg, gi = first_argmax(lg)
    wg = 1.0 / jnp.sum(jnp.where(is_group, jnp.exp(lg - mg), 0.0), axis=1, keepdims=True)
    e0 = float(MOE_GROUPS) + float(MOE_EPG) * gi
    in_group = (lane >= e0) & (lane < e0 + float(MOE_EPG))
    le = jnp.where(in_group, logits, _NEG_INF)
    m1, i1 = first_argmax(le)
    m2, i2 = first_argmax(jnp.where(lane == i1, _NEG_INF, le))
    r = jnp.exp(m2 - m1)
    g1 = wg / (1.0 + r)
    g2 = wg * r / (1.0 + r)
    a = jnp.minimum(i1, i2) - e0
    b = jnp.maximum(i1, i2) - e0
    ga = jnp.where(i1 < i2, g1, g2)
    gb = jnp.where(i1 < i2, g2, g1)
    pair = a * (7.0 - a) * 0.5 + (b - a - 1.0)
    cls = gi * 6.0 + pair
    onehot = jnp.where(lane == cls, 1.0, 0.0)
    before = jnp.dot(tril_ref[...], onehot.astype(BF16), preferred_element_type=F32) + cnt_scr[...]
    rank = jnp.sum(onehot * before, axis=1, keepdims=True)
    cnt_scr[...] = cnt_scr[...] + jnp.sum(onehot, axis=0, keepdims=True)
    route_ref[...] = jnp.where(lane == 0.0, cls, jnp.where(lane == 1.0, ga, jnp.where(lane == 2.0, gb,
                               jnp.where(lane == 3.0, rank, 0.0))))
    cnt_ref[...] = cnt_scr[...]


def _router(h, g, wr, tm):
    n = h.shape[0]
    tril = (jnp.arange(tm)[:, None] > jnp.arange(tm)[None, :]).astype(BF16)
    row = lambda i: (i, 0)
    const = lambda i: (0, 0)
    return pl.pallas_call(
        _router_kernel,
        grid=(n // tm,),
        in_specs=[pl.BlockSpec((tm, D_MODEL), row), pl.BlockSpec((1, D_MODEL), const),
                  pl.BlockSpec((D_MODEL, LANES), const), pl.BlockSpec((tm, tm), const)],
        out_specs=[pl.BlockSpec((tm, LANES), row), pl.BlockSpec((1, LANES), const)],
        out_shape=[jax.ShapeDtypeStruct((n, LANES), F32), jax.ShapeDtypeStruct((1, LANES), F32)],
        scratch_shapes=[pltpu.VMEM((1, LANES), F32)],
        compiler_params=_cparams("arbitrary"),
        name="moe_router",
    )(h, g, wr, tril)


def _ffn_kernel(ea_ref, eb_ref, nv_ref, tok_ref, gates_ref, g_ref, wga_ref, wua_ref, wda_ref, wgb_ref, wub_ref, wdb_ref,
                h_hbm, out_hbm, xbuf, ybuf, gsem, ssem):
    i = pl.program_id(0)
    n_steps = pl.num_programs(0)
    slot = i % 2

    def gather_row(tile, buf_slot, r):
        tok = tok_ref[tile * MOE_TILE + r]
        return pltpu.make_async_copy(h_hbm.at[pl.ds(tok, 1), :], xbuf.at[buf_slot, pl.ds(r, 1), :], gsem.at[buf_slot])

    def scatter_row(tile, buf_slot, r):
        tok = tok_ref[tile * MOE_TILE + r]
        return pltpu.make_async_copy(ybuf.at[buf_slot, pl.ds(r, 1), :], out_hbm.at[pl.ds(tok, 1), :], ssem.at[buf_slot])

    def for_valid_rows(tile, fn):
        def body(r, c):
            fn(r)
            return c
        lax.fori_loop(0, nv_ref[tile], body, 0)

    @pl.when(i == 0)
    def _():
        xbuf[...] = jnp.zeros(xbuf.shape, F32)
        for_valid_rows(0, lambda r: gather_row(0, 0, r).start())

    @pl.when(i + 1 < n_steps)
    def _():
        for_valid_rows(i + 1, lambda r: gather_row(i + 1, 1 - slot, r).start())

    for_valid_rows(i, lambda r: gather_row(i, slot, r).wait())

    @pl.when(i >= 2)
    def _():
        for_valid_rows(i - 2, lambda r: scatter_row(i - 2, slot, r).wait())

    @pl.when(nv_ref[i] > 0)
    def _():
        x = xbuf[slot]
        xb = _rmsnorm(x, g_ref[...]).astype(BF16)
        y = x
        for lane, wg_ref, wu_ref, wd_ref in ((0, wga_ref, wua_ref, wda_ref), (1, wgb_ref, wub_ref, wdb_ref)):
            gate = gates_ref[:, lane:lane + 1]
            hid = jax.nn.silu(jnp.dot(xb, wg_ref[0], preferred_element_type=F32)) * jnp.dot(xb, wu_ref[0], preferred_element_type=F32)
            y = y + jnp.dot((hid * gate).astype(BF16), wd_ref[0], preferred_element_type=F32)
        ybuf[slot] = y

    for_valid_rows(i, lambda r: scatter_row(i, slot, r).start())

    @pl.when(i == n_steps - 1)
    def _():
        for_valid_rows(i - 1, lambda r: scatter_row(i - 1, 1 - slot, r).wait())
        for_valid_rows(i, lambda r: scatter_row(i, slot, r).wait())


def _ffn(ea, eb, nvalid, tok, gates, h, g, wg, wu, wd):
    n_tiles = tok.shape[0] // MOE_TILE
    tile = lambda i, *_: (i, 0)
    const = lambda i, *_: (0, 0)
    up = lambda sel: pl.BlockSpec((1, D_MODEL, EXPERT_FF), sel)
    down = lambda sel: pl.BlockSpec((1, EXPERT_FF, D_MODEL), sel)
    sel_a = lambda i, ea, eb, nv, tok: (ea[i], 0, 0)
    sel_b = lambda i, ea, eb, nv, tok: (eb[i], 0, 0)
    anyspec = pl.BlockSpec(memory_space=pl.ANY)
    return pl.pallas_call(
        _ffn_kernel,
        grid_spec=pltpu.PrefetchScalarGridSpec(
            num_scalar_prefetch=4,
            grid=(n_tiles,),
            in_specs=[pl.BlockSpec((MOE_TILE, 2), tile), pl.BlockSpec((1, D_MODEL), const),
                      up(sel_a), up(sel_a), down(sel_a), up(sel_b), up(sel_b), down(sel_b), anyspec],
            out_specs=anyspec,
            scratch_shapes=[pltpu.VMEM((2, MOE_TILE, D_MODEL), F32), pltpu.VMEM((2, MOE_TILE, D_MODEL), F32),
                            pltpu.SemaphoreType.DMA((2,)), pltpu.SemaphoreType.DMA((2,))],
        ),
        out_shape=jax.ShapeDtypeStruct(h.shape, F32),
        compiler_params=_cparams("arbitrary"),
        name="moe_ffn",
    )(ea, eb, nvalid, tok, gates, g, wg, wu, wd, wg, wu, wd, h)


_PAIR_A = (0, 0, 0, 1, 1, 2)
_PAIR_B = (1, 2, 3, 2, 3, 3)


def _hier_moe(h, g, wr, wg, wu, wd):
    n = h.shape[0]
    tile = MOE_TILE
    route, cnt = _router(h, g, wr, tile)
    cls = route[:, 0].astype(I32)
    rank = route[:, 3].astype(I32)
    counts = cnt[0, :N_CLASSES].astype(I32)
    tiles_c = (counts + tile - 1) // tile
    tile_end = jnp.cumsum(tiles_c)
    tile_start = tile_end - tiles_c
    pos = tile_start[cls] * tile + rank
    n_tiles = n // tile + N_CLASSES
    ti = jnp.arange(n_tiles, dtype=I32)
    total = tile_end[-1]
    tcls = jnp.searchsorted(tile_end, jnp.minimum(ti, total - 1), side="right").astype(I32)
    tcls = jnp.minimum(tcls, N_CLASSES - 1)
    nvalid = jnp.where(ti < total, jnp.clip(counts[tcls] - (ti - tile_start[tcls]) * tile, 0, tile), 0).astype(I32)
    grp, pair = tcls // 6, tcls % 6
    ea = grp * MOE_EPG + jnp.asarray(_PAIR_A, I32)[pair]
    eb = grp * MOE_EPG + jnp.asarray(_PAIR_B, I32)[pair]
    tok = jnp.zeros((n_tiles * tile,), I32).at[pos].set(jnp.arange(n, dtype=I32))
    gates = jnp.zeros((n_tiles * tile, 2), F32).at[pos].set(route[:, 1:3])
    return _ffn(ea, eb, nvalid, tok, gates, h, g, wg, wu, wd)


def _odd_inproj_kernel(x_ref, g_ref, wz_ref, wx_ref, wdt_ref, z_ref, xbc_ref, dt_ref):
    xb = _rmsnorm(x_ref[...], g_ref[...]).astype(BF16)
    z_ref[...] = jnp.dot(xb, wz_ref[...], preferred_element_type=F32)
    xbc_ref[...] = jnp.dot(xb, wx_ref[...], preferred_element_type=F32)
    dt_ref[...] = jnp.dot(xb, wdt_ref[...], preferred_element_type=F32)


def _odd_inproj(x, g, wz, wx, wdt, tm):
    n = x.shape[0]
    row = lambda i: (i, 0)
    const = lambda i: (0, 0)
    widths = (D_INNER, CONV_DIM, LANES)
    return pl.pallas_call(
        _odd_inproj_kernel,
        grid=(n // tm,),
        in_specs=[pl.BlockSpec((tm, D_MODEL), row), pl.BlockSpec((1, D_MODEL), const),
                  pl.BlockSpec(wz.shape, const), pl.BlockSpec(wx.shape, const), pl.BlockSpec(wdt.shape, const)],
        out_specs=[pl.BlockSpec((tm, c), row) for c in widths],
        out_shape=[jax.ShapeDtypeStruct((n, c), F32) for c in widths],
        compiler_params=_cparams("parallel"),
        name="odd_inproj",
    )(x, g, wz, wx, wdt)


CONV_COLS = 512


def _conv_kernel(xbc_ref, dt_ref, cs_ref, w_ref, b_ref, dtb_ref, xc_ref, dts_ref, cso_ref, carry):
    c = pl.program_id(1)
    tt = xbc_ref.shape[0]
    tail = CONV_W - 1

    @pl.when(c == 0)
    def _():
        carry[...] = jnp.zeros(carry.shape, F32)
        carry[8 - tail:8, :] = cs_ref[0]

    for j in range(CONV_DIM // CONV_COLS):
        cols = slice(j * CONV_COLS, (j + 1) * CONV_COLS)
        x = xbc_ref[:, cols]
        full = jnp.concatenate([carry[:, cols], x], axis=0)
        acc = b_ref[:, cols] + x * w_ref[tail:tail + 1, cols]
        for k in range(tail):
            acc = acc + full[8 - tail + k:8 - tail + k + tt, :] * w_ref[k:k + 1, cols]
        xc_ref[:, cols] = jax.nn.silu(acc)
        new_tail = full[8 + tt - tail:8 + tt, :]
        carry[8 - tail:8, cols] = new_tail
        cso_ref[0, :, cols] = new_tail
    dts_ref[...] = jax.nn.softplus(dt_ref[...] + dtb_ref[...])


def _conv(xbc, dt, conv_state, w, b, dtb, n_batch, n_tok, tt):
    n_chunks = n_tok // tt
    blk = lambda i, c: (i * n_chunks + c, 0)
    const = lambda i, c: (0, 0)
    st = lambda i, c: (i, 0, 0)
    return pl.pallas_call(
        _conv_kernel,
        grid=(n_batch, n_chunks),
        in_specs=[pl.BlockSpec((tt, CONV_DIM), blk), pl.BlockSpec((tt, LANES), blk),
                  pl.BlockSpec((1, CONV_W - 1, CONV_DIM), st),
                  pl.BlockSpec((CONV_W, CONV_DIM), const), pl.BlockSpec((1, CONV_DIM), const), pl.BlockSpec((1, LANES), const)],
        out_specs=[pl.BlockSpec((tt, CONV_DIM), blk), pl.BlockSpec((tt, LANES), blk),
                   pl.BlockSpec((1, CONV_W - 1, CONV_DIM), st)],
        out_shape=[jax.ShapeDtypeStruct(xbc.shape, F32), jax.ShapeDtypeStruct(dt.shape, F32),
                   jax.ShapeDtypeStruct(conv_state.shape, F32)],
        scratch_shapes=[pltpu.VMEM((8, CONV_DIM), F32)],
        compiler_params=_cparams("parallel", "arbitrary"),
        name="ssd_conv",
    )(xbc, dt, conv_state, w, b, dtb)


def _ssd_kernel(xs_ref, bm_ref, cm_ref, z_ref, dt_ref, s0_ref, a_ref, dsk_ref, ng_ref, tril_ref, mask_ref, elast_ref,
                o_ref, sout_ref, s_scr, yz_scr, *, n_seq, seq_len, gps):
    c = pl.program_id(2)

    @pl.when(c == 0)
    def _():
        s_scr[...] = s0_ref[...]

    lane = _lane_iota((ROWS, LANES))
    row = _row_iota((ROWS, LANES))
    lo_half = lane < (LANES // 2)
    dt = dt_ref[...]
    cs = _dot01(tril_ref[...], dt * a_ref[...])
    cs_last = _dot01(elast_ref[...], cs)
    if gps != SSM_GROUPS:
        shift = (LANES - pl.program_id(1) * (gps * SSM_HEADS // SSM_GROUPS)) % LANES
        dt, cs, cs_last = (pltpu.roll(v, shift, 1) for v in (dt, cs, cs_last))
    cs_t = cs.T
    ecs = jnp.exp(cs)
    dt_tail = dt * jnp.exp(cs_last - cs)
    causal = mask_ref[...] > 0.0

    def per_head(v, h0):
        return jnp.where(lo_half, v[:, h0:h0 + 1], v[:, h0 + 1:h0 + 2])

    for g in range(gps):
        gcols = slice(g * D_STATE, (g + 1) * D_STATE)
        cg = cm_ref[:, gcols].astype(BF16)
        bg = bm_ref[:, gcols].astype(BF16)
        cb = _dot_nt(cg, bg)
        for pp in range(2):
            pair = 2 * g + pp
            h0 = 2 * pair
            pcols = slice(pair * LANES, (pair + 1) * LANES)
            x2 = xs_ref[:, pcols]
            xdt = x2 * per_head(dt, h0)
            ys = []
            for e in range(2):
                h = h0 + e
                seg = cs[:, h:h + 1] - cs_t[h:h + 1, :]
                wmat = jnp.exp(jnp.where(causal, seg, _NEG_INF)) * cb
                ys.append(_dot(wmat, xdt))
            y2 = jnp.where(lo_half, ys[0], ys[1])
            if n_seq == 1:
                inter = _dot_nt(cg, s_scr[0, pair])
            else:
                inter = jnp.zeros((ROWS, LANES), F32)
                for s in range(n_seq):
                    in_seq = (row >= s * seq_len) & (row < (s + 1) * seq_len)
                    inter = inter + jnp.where(in_seq, _dot_nt(cg, s_scr[s, pair]), 0.0)
            y2 = y2 + inter * per_head(ecs, h0) + dsk_ref[:, pcols] * x2
            yz_scr[:, pcols] = y2 * jax.nn.silu(z_ref[:, pcols])
            xw_t = (x2 * per_head(dt_tail, h0)).T
            for s in range(n_seq):
                last = (s + 1) * seq_len - 1
                if n_seq == 1:
                    xt = xw_t
                else:
                    xt = jnp.where((lane >= s * seq_len) & (lane <= last), xw_t, 0.0)
                dec = jnp.exp(jnp.where(row < (LANES // 2), cs_t[h0:h0 + 1, last:last + 1], cs_t[h0 + 1:h0 + 2, last:last + 1]))
                s_scr[s, pair] = s_scr[s, pair] * dec + _dot(xt, bg)
    gw = D_INNER // SSM_GROUPS
    for g in range(gps):
        cols = slice(g * gw, (g + 1) * gw)
        o_ref[:, cols] = _rmsnorm(yz_scr[:, cols], ng_ref[:, cols])

    @pl.when(c == pl.num_programs(2) - 1)
    def _():
        sout_ref[...] = s_scr[...]


def _ssd(xc, z, dts, s0, a_row, dsk_row, ng_row, n_batch, n_tok):
    seq_len = min(ROWS, n_tok)
    n_seq = ROWS // seq_len
    n_chunks = n_tok // seq_len
    r = jnp.arange(ROWS)
    li, sid = r % seq_len, r // seq_len
    mask = ((sid[:, None] == sid[None, :]) & (li[:, None] >= li[None, :]))
    elast = (r[None, :] == (sid * seq_len + seq_len - 1)[:, None])
    n_pairs = SSM_HEADS // 2
    s0p = s0.reshape(n_batch, n_pairs, LANES, D_STATE)
    gps = SSM_GROUPS if n_seq == 1 else 1
    xw = gps * (D_INNER // SSM_GROUPS)
    bw = gps * D_STATE
    rows = lambda i, j, c: i * n_chunks + c
    xblk = lambda i, j, c: (rows(i, j, c), j)
    bblk = lambda i, j, c: (rows(i, j, c), D_INNER // bw + j)
    cblk = lambda i, j, c: (rows(i, j, c), (D_INNER + SSM_GROUPS * D_STATE) // bw + j)
    dtblk = lambda i, j, c: (rows(i, j, c), 0)
    st = lambda i, j, c: (i, j, 0, 0)
    const = lambda i, j, c: (0, 0)
    gconst = lambda i, j, c: (0, j)
    state_block = (n_seq, 2 * gps, LANES, D_STATE)
    o, s_out = pl.pallas_call(
        functools.partial(_ssd_kernel, n_seq=n_seq, seq_len=seq_len, gps=gps),
        grid=(n_batch // n_seq, SSM_GROUPS // gps, n_chunks),
        in_specs=[pl.BlockSpec((ROWS, xw), xblk), pl.BlockSpec((ROWS, bw), bblk), pl.BlockSpec((ROWS, bw), cblk),
                  pl.BlockSpec((ROWS, xw), xblk), pl.BlockSpec((ROWS, LANES), dtblk),
                  pl.BlockSpec(state_block, st),
                  pl.BlockSpec((1, LANES), const), pl.BlockSpec((1, xw), gconst), pl.BlockSpec((1, xw), gconst),
                  pl.BlockSpec((ROWS, ROWS), const), pl.BlockSpec((ROWS, ROWS), const), pl.BlockSpec((ROWS, ROWS), const)],
        out_specs=[pl.BlockSpec((ROWS, xw), xblk), pl.BlockSpec(state_block, st)],
        out_shape=[jax.ShapeDtypeStruct((n_batch * n_tok, D_INNER), F32), jax.ShapeDtypeStruct(s0p.shape, F32)],
        scratch_shapes=[pltpu.VMEM(state_block, F32), pltpu.VMEM((ROWS, xw), F32)],
        compiler_params=_cparams("parallel", "parallel", "arbitrary"),
        name="ssd_scan",
    )(xc, xc, xc, z, dts, s0p, a_row, dsk_row, ng_row, mask.astype(BF16), mask.astype(F32), elast.astype(BF16))
    return o, s_out.reshape(s0.shape)


def _rope_tables(pos):
    half = MOBA_HD // 2
    inv = jnp.power(ROPE_THETA, -jnp.arange(half, dtype=F32) * 2.0 / MOBA_HD)
    ang = pos[:, None] * inv[None, :]
    cos, sin = jnp.cos(ang), jnp.sin(ang)
    reps = LANES // MOBA_HD
    return jnp.tile(jnp.concatenate([cos, cos], axis=1), (1, reps)), jnp.tile(jnp.concatenate([-sin, sin], axis=1), (1, reps))


def _pad_cols(w, width):
    return jnp.pad(w, ((0, 0), (0, width - w.shape[1])))


TOKEN_TILE = 256


def _even_layer(h, nb, nt, pos, ret0, cache, norm_g, w_in, w_out, q_g, k_g):
    tm = TOKEN_TILE
    cos, sin = _rope_tables(pos)
    if nt < tm:
        cos, sin = jnp.tile(cos, (tm // nt, 1)), jnp.tile(sin, (tm // nt, 1))
    head_tile = lambda v: jnp.tile(v, MOBA_HEADS)[None, :]
    width = MOBA_HEADS * MOBA_HD
    blockdiag = (jnp.arange(width)[:, None] // MOBA_HD == jnp.arange(width)[None, :] // MOBA_HD).astype(BF16)
    outs = _even_inproj(h, norm_g[None, :], w_in.astype(BF16), cos, sin, head_tile(q_g), head_tile(k_g), blockdiag,
                        tm, nt, key_minor=cache is None)
    rq, rk, rv, rg = outs[:4]
    if cache is None:
        qt, kb, kt, vt, vtb, kbar = outs[4:]
        mo = _moba_prompt(qt, kb, vtb, kbar.reshape(-1, width), nb, nt)
        mk, mv = (a.reshape(nb, MOBA_HEADS, MOBA_HD, nt).transpose(0, 3, 1, 2) for a in (kt, vt))
    else:
        mq, mk, mv = outs[4:]
        mo = _moba_sample(mq, mk, mv, *cache, nb, nt)
        mk, mv = (a.reshape(nb, nt, MOBA_HEADS, MOBA_HD) for a in (mk, mv))
    ro, ret_state = _retention(rq, rk, rv, rg, ret0, nb, nt)
    w_out = w_out.astype(BF16)
    h = _outproj(h, (ro, mo), (w_out[:RET_HEADS * RET_DV], w_out[RET_HEADS * RET_DV:]), tm)
    return h, ret_state, mk, mv


def _odd_layer(h, nb, nt, conv0, ssm0, norm_g, w_in, conv_w, conv_b, dt_bias, a_log, d_skip, ssd_norm, w_out):
    tm = TOKEN_TILE
    w_in = w_in.astype(BF16)
    z, xbc, dt = _odd_inproj(h, norm_g[None, :], w_in[:, :D_INNER], w_in[:, D_INNER:D_INNER + CONV_DIM],
                             _pad_cols(w_in[:, D_INNER + CONV_DIM:], LANES), tm)
    xc, dts, conv_state = _conv(xbc, dt, conv0, conv_w, conv_b[None, :], _pad_cols(dt_bias[None, :], LANES),
                                nb, nt, min(nt, 256))
    a_row = _pad_cols(-jnp.exp(a_log)[None, :], LANES)
    dsk_row = jnp.repeat(d_skip, D_INNER // SSM_HEADS)[None, :]
    yzn, ssm_state = _ssd(xc, z, dts, ssm0, a_row, dsk_row, ssd_norm[None, :], nb, nt)
    h = _outproj(h, (yzn,), (w_out.astype(BF16),), tm)
    return h, conv_state, ssm_state


def _moe_layer(h, norm_g, w_rg, w_re, w_g, w_u, w_d):
    wr = _pad_cols(jnp.concatenate([w_rg, w_re], axis=1), LANES)
    return _hier_moe(h, norm_g[None, :], wr, w_g.astype(BF16), w_u.astype(BF16), w_d.astype(BF16))


def kernel(x_prompt, x_sample, state_ret, cache_k, cache_v, page_table, state_ssm, state_conv, norm_mix, norm_ffn,
           w_in_even, w_out_even, q_norm, k_norm, w_in_odd, conv_w, conv_b, dt_bias, a_log, d_skip, ssd_norm, w_out_odd,
           w_router_group, w_router_expert, w_expert_gate, w_expert_up, w_expert_down):
    bp, seq, _ = x_prompt.shape
    bs, dec_seq, _ = x_sample.shape
    past_len = page_table.shape[1] * PAGE_SIZE
    groups = ((x_prompt.reshape(bp * seq, D_MODEL), bp, seq, jnp.arange(seq, dtype=F32)),
              (x_sample.reshape(bs * dec_seq, D_MODEL), bs, dec_seq, past_len + jnp.arange(dec_seq, dtype=F32)))

    def moe(h, layer):
        return _moe_layer(h, norm_ffn[layer], w_router_group[layer], w_router_expert[layer],
                          w_expert_gate[layer], w_expert_up[layer], w_expert_down[layer])

    outs = []
    for gi, (h, nb, nt, pos) in enumerate(groups):
        if gi == 0:
            ret0 = jnp.zeros((nb, RET_HEADS, RET_DK, RET_DV), F32)
            cache = None
            conv0 = jnp.zeros((nb, CONV_W - 1, CONV_DIM), F32)
            ssm0 = jnp.zeros((nb, SSM_HEADS, D_INNER // SSM_HEADS, D_STATE), F32)
        else:
            ret0 = state_ret[0]
            cache = (cache_k[0].transpose(0, 2, 3, 1), cache_v[0].transpose(0, 2, 3, 1), page_table)
            conv0, ssm0 = state_conv[0], state_ssm[0]
        h, ret_state, mk, mv = _even_layer(h, nb, nt, pos, ret0, cache, norm_mix[0], w_in_even[0], w_out_even[0],
                                           q_norm[0], k_norm[0])
        h = moe(h, 0)
        h, conv_state, ssm_state = _odd_layer(h, nb, nt, conv0, ssm0, norm_mix[1], w_in_odd[0], conv_w[0], conv_b[0],
                                              dt_bias[0], a_log[0], d_skip[0], ssd_norm[0], w_out_odd[0])
        h = moe(h, 1)
        outs.append(dict(h=h.reshape(nb, nt, D_MODEL), ret=ret_state[None], k=mk[None], v=mv[None],
                         ssm=ssm_state[None], conv=conv_state[None]))
    p, s = outs
    return (p["h"], s["h"], p["ret"], s["ret"], p["k"], p["v"], s["k"], s["v"], p["ssm"], s["ssm"], p["conv"], s["conv"])
```

```python
import functools
import math

import jax
import jax.numpy as jnp
from jax import lax
from jax.experimental import pallas as pl
from jax.experimental.pallas import tpu as pltpu

F32 = jnp.float32
BF16 = jnp.bfloat16
I32 = jnp.int32

D_MODEL = 1024
PAGE_SIZE = 128
RET_HEADS = 8
RET_DK = 64
RET_DV = 128
MOBA_HEADS = 8
MOBA_HD = 64
MOBA_BLOCK = 256
MOBA_TOPK = 3
ROPE_THETA = 10000.0
D_INNER = 2048
SSM_HEADS = 32
SSM_GROUPS = 8
D_STATE = 128
CONV_W = 4
CONV_DIM = 4096
MOE_GROUPS = 4
MOE_EPG = 4
N_EXPERTS = 16
EXPERT_FF = 512
NORM_EPS = 1e-6
N_CLASSES = MOE_GROUPS * 6
LANES = 128
ROWS = 128
MOE_TILE = 256
VMEM_LIMIT = 56 * 1024 * 1024

_NEG_INF = float("-inf")


def _cparams(*sem):
    return pltpu.CompilerParams(dimension_semantics=sem, vmem_limit_bytes=VMEM_LIMIT)


def _dot(a, b):
    return jnp.dot(a.astype(BF16), b.astype(BF16), preferred_element_type=F32)


def _dot_nt(a, b):
    return lax.dot_general(a.astype(BF16), b.astype(BF16), (((1,), (1,)), ((), ())), preferred_element_type=F32)


def _split3(x):
    hi = x.astype(BF16)
    r = x - hi.astype(F32)
    mid = r.astype(BF16)
    lo = (r - mid.astype(F32)).astype(BF16)
    return hi, mid, lo


def _dot01(m01, x):
    hi, mid, lo = _split3(x)
    return (jnp.dot(m01, hi, preferred_element_type=F32) + jnp.dot(m01, mid, preferred_element_type=F32)
            + jnp.dot(m01, lo, preferred_element_type=F32))


def _rmsnorm(x, g=None):
    y = x * lax.rsqrt(jnp.mean(x * x, axis=-1, keepdims=True) + NORM_EPS)
    return y if g is None else y * g


def _lane_iota(shape):
    return lax.broadcasted_iota(I32, shape, len(shape) - 1)


def _row_iota(shape):
    return lax.broadcasted_iota(I32, shape, len(shape) - 2)


def _even_inproj_kernel(x_ref, g_ref, w_ref, cos_ref, sin_ref, qg_ref, kg_ref, bd_ref,
                        rq_ref, rk_ref, rv_ref, rg_ref, *moba_refs, key_minor):
    xb = _rmsnorm(x_ref[...], g_ref[...]).astype(BF16)
    cos = cos_ref[...]
    sin = sin_ref[...]
    first_half = (_lane_iota(cos.shape) % MOBA_HD) < (MOBA_HD // 2)

    def proj(c0, c1):
        return jnp.dot(xb, w_ref[:, c0:c1], preferred_element_type=F32)

    def rope(a):
        outs = []
        for j in range(a.shape[1] // LANES):
            s = a[:, j * LANES:(j + 1) * LANES]
            rot = jnp.where(first_half, pltpu.roll(s, LANES - MOBA_HD // 2, 1), pltpu.roll(s, MOBA_HD // 2, 1))
            outs.append(s * cos + rot * sin)
        return jnp.concatenate(outs, axis=1)

    def qk_norm(a, gam):
        ss = jnp.dot((a * a).astype(BF16), bd_ref[...], preferred_element_type=F32)
        return a * lax.rsqrt(ss * (1.0 / MOBA_HD) + NORM_EPS) * gam

    rq_ref[...] = rope(proj(0, 512))
    rk_ref[...] = rope(proj(512, 1024)) * (RET_DK ** -0.5)
    rv_ref[...] = proj(1024, 2048)
    rg_ref[...] = proj(2048, 3072)
    mq = rope(qk_norm(proj(3072, 3584), qg_ref[...]))
    mk = rope(qk_norm(proj(3584, 4096), kg_ref[...]))
    mv = proj(4096, 4608)
    if not key_minor:
        mq_ref, mk_ref, mv_ref = moba_refs
        mq_ref[...] = mq
        mk_ref[...] = mk
        mv_ref[...] = mv
    else:
        qt_ref, kb_ref, kt_ref, vt_ref, vtb_ref, kbar_ref = moba_refs
        qt_ref[0] = mq.T
        kb_ref[...] = mk.astype(BF16)
        kt_ref[0] = mk.T
        vt = mv.T
        vt_ref[0] = vt
        vtb_ref[0] = vt.astype(BF16)
        kbar_ref[0] = jnp.mean(mk, axis=0, keepdims=True)


def _even_inproj(x, g, w, cos, sin, qg, kg, bd, tm, n_tok, key_minor):
    n = x.shape[0]
    n_pos_blocks = cos.shape[0] // tm
    row = lambda i: (i, 0)
    const = lambda i: (0, 0)
    width = MOBA_HEADS * MOBA_HD
    out_specs = [pl.BlockSpec((tm, c), row) for c in (512, 512, 1024, 1024)]
    out_shape = [jax.ShapeDtypeStruct((n, c), F32) for c in (512, 512, 1024, 1024)]
    if not key_minor:
        out_specs += [pl.BlockSpec((tm, width), row)] * 3
        out_shape += [jax.ShapeDtypeStruct((n, width), F32)] * 3
    else:
        assert tm == MOBA_BLOCK and n_tok % tm == 0
        tiles = n_tok // tm
        tspec = pl.BlockSpec((1, width, tm), lambda i: (i // tiles, 0, i % tiles))
        tshape = lambda dt: jax.ShapeDtypeStruct((n // n_tok, width, n_tok), dt)
        out_specs += [tspec, pl.BlockSpec((tm, width), row), tspec, tspec, tspec, pl.BlockSpec((1, 1, width), lambda i: (i, 0, 0))]
        out_shape += [tshape(F32), jax.ShapeDtypeStruct((n, width), BF16), tshape(F32), tshape(F32), tshape(BF16),
                      jax.ShapeDtypeStruct((n // tm, 1, width), F32)]
    return pl.pallas_call(
        functools.partial(_even_inproj_kernel, key_minor=key_minor),
        grid=(n // tm,),
        in_specs=[pl.BlockSpec((tm, D_MODEL), row), pl.BlockSpec((1, D_MODEL), const),
                  pl.BlockSpec(w.shape, const),
                  pl.BlockSpec((tm, LANES), lambda i: (i % n_pos_blocks, 0)),
                  pl.BlockSpec((tm, LANES), lambda i: (i % n_pos_blocks, 0)),
                  pl.BlockSpec((1, 512), const), pl.BlockSpec((1, 512), const), pl.BlockSpec((512, 512), const)],
        out_specs=out_specs,
        out_shape=out_shape,
        compiler_params=_cparams("parallel"),
        name="even_inproj",
    )(x, g, w, cos, sin, qg, kg, bd)


def _retention_kernel(rq_ref, rk_ref, rv_ref, rg_ref, s0_ref, dmat_ref, qdec_ref, kdec_ref, sdec_ref,
                      o_ref, sout_ref, s_scr, *, n_seq, seq_len):
    c = pl.program_id(1)

    @pl.when(c == 0)
    def _():
        s_scr[...] = s0_ref[...]

    lane = _lane_iota((ROWS, LANES))
    row = _row_iota((ROWS, LANES))
    for p in range(RET_HEADS // 2):
        q2 = rq_ref[:, p * LANES:(p + 1) * LANES]
        k2 = rk_ref[:, p * LANES:(p + 1) * LANES]
        kd_t = (k2 * kdec_ref[p]).T
        v_pair = []
        for e in range(2):
            h = 2 * p + e
            head_lanes = (lane < RET_DK) if e == 0 else (lane >= RET_DK)
            qm = jnp.where(head_lanes, q2, 0.0)
            scores = _dot_nt(qm, k2) * dmat_ref[h]
            vh = rv_ref[:, h * RET_DV:(h + 1) * RET_DV]
            v_pair.append(vh)
            if n_seq == 1:
                inter = _dot(qm, s_scr[0, p])
            else:
                inter = jnp.zeros((ROWS, RET_DV), F32)
                for s in range(n_seq):
                    in_seq = (row >= s * seq_len) & (row < (s + 1) * seq_len)
                    inter = inter + jnp.where(in_seq, _dot(qm, s_scr[s, p]), 0.0)
            o = _dot(scores, vh) + inter * qdec_ref[h]
            gate = rg_ref[:, h * RET_DV:(h + 1) * RET_DV]
            o_ref[:, h * RET_DV:(h + 1) * RET_DV] = jax.nn.silu(gate) * _rmsnorm(o)
        for s in range(n_seq):
            if n_seq == 1:
                kt = kd_t
            else:
                kt = jnp.where((lane >= s * seq_len) & (lane < (s + 1) * seq_len), kd_t, 0.0)
            upd = jnp.where(row < RET_DK, _dot(kt, v_pair[0]), _dot(kt, v_pair[1]))
            s_scr[s, p] = s_scr[s, p] * sdec_ref[p] + upd

    @pl.when(c == pl.num_programs(1) - 1)
    def _():
        sout_ref[...] = s_scr[...]


def _retention_tables(seq_len):
    log_g = jnp.log1p(-jnp.exp2(-5.0 - jnp.arange(RET_HEADS, dtype=F32)))
    r = jnp.arange(ROWS)
    li = (r % seq_len).astype(F32)
    sid = r // seq_len
    diff = li[:, None] - li[None, :]
    same = sid[:, None] == sid[None, :]
    dmat = jnp.where(same[None] & (diff[None] >= 0),
                     jnp.exp(jnp.maximum(diff, 0.0)[None] * log_g[:, None, None]), 0.0)
    qdec = jnp.broadcast_to(jnp.exp((li + 1.0)[None, :, None] * log_g[:, None, None]), (RET_HEADS, ROWS, RET_DV))
    lane_head = jnp.arange(LANES) // RET_DK
    pair_log_g = log_g.reshape(RET_HEADS // 2, 2)[:, lane_head]
    kdec = jnp.exp((seq_len - 1.0 - li)[None, :, None] * pair_log_g[:, None, :])
    sdec = jnp.broadcast_to(jnp.exp(float(seq_len) * pair_log_g)[:, :, None], (RET_HEADS // 2, LANES, RET_DV))
    return dmat, qdec, kdec, sdec


def _retention(rq, rk, rv, rg, s0, n_batch, n_tok):
    seq_len = min(ROWS, n_tok)
    n_seq = ROWS // seq_len
    n_chunks = n_tok // seq_len
    tables = _retention_tables(seq_len)
    s0p = s0.reshape(n_batch, RET_HEADS // 2, 2 * RET_DK, RET_DV)
    blk = lambda i, c: (i * n_chunks + c, 0)
    st = lambda i, c: (i, 0, 0, 0)
    c3 = lambda i, c: (0, 0, 0)
    o, s_out = pl.pallas_call(
        functools.partial(_retention_kernel, n_seq=n_seq, seq_len=seq_len),
        grid=(n_batch // n_seq, n_chunks),
        in_specs=[pl.BlockSpec((ROWS, 512), blk), pl.BlockSpec((ROWS, 512), blk),
                  pl.BlockSpec((ROWS, 1024), blk), pl.BlockSpec((ROWS, 1024), blk),
                  pl.BlockSpec((n_seq,) + s0p.shape[1:], st)]
                 + [pl.BlockSpec(t.shape, c3) for t in tables],
        out_specs=[pl.BlockSpec((ROWS, 1024), blk), pl.BlockSpec((n_seq,) + s0p.shape[1:], st)],
        out_shape=[jax.ShapeDtypeStruct((n_batch * n_tok, 1024), F32), jax.ShapeDtypeStruct(s0p.shape, F32)],
        scratch_shapes=[pltpu.VMEM((n_seq,) + s0p.shape[1:], F32)],
        compiler_params=_cparams("parallel", "arbitrary"),
        name="retention",
    )(rq, rk, rv, rg, s0p, *tables)
    return o, s_out.reshape(s0.shape)


def _moba_prompt_kernel(qt_ref, k_ref, vt_ref, kbar_ref, o_ref, sel_scr, qtm_scr, m_scr, l_scr, acc_scr, *, n_blocks):
    qb = pl.program_id(1)
    blk = MOBA_BLOCK
    n_pairs = MOBA_HEADS // 2
    blk_row = _row_iota((n_blocks, blk))
    lo_rows = _row_iota((LANES, blk)) < MOBA_HD
    lo_lanes = _lane_iota((n_blocks, LANES)) < MOBA_HD
    key_le_query = _row_iota((blk, blk)) <= _lane_iota((blk, blk))

    for p in range(n_pairs):
        rows = slice(p * LANES, (p + 1) * LANES)
        qt2 = qt_ref[0, rows, :]
        kb2 = kbar_ref[:, rows]
        for e in range(2):
            h = 2 * p + e
            gate = _dot(jnp.where(lo_lanes if e == 0 else ~lo_lanes, kb2, 0.0), qt2)
            cnt = jnp.zeros((n_blocks, blk), F32)
            for m in range(n_blocks - 1):
                gm = gate[m:m + 1, :]
                ahead = (gm > gate) | ((gm == gate) & (blk_row > m))
                cnt = cnt + jnp.where(ahead, (qb > m).astype(F32), 0.0)
            sel_scr[h] = jnp.where((blk_row < qb) & (cnt < float(MOBA_TOPK)), 1.0, 0.0)
            head_rows = lo_rows if e == 0 else ~lo_rows
            qtm_scr[h] = (jnp.where(head_rows, qt2, 0.0) * (MOBA_HD ** -0.5)).astype(BF16)

    def attend(n, own):
        c0 = pl.multiple_of(n * blk, blk)
        for p in range(n_pairs):
            rows = slice(p * LANES, (p + 1) * LANES)
            k2 = k_ref[pl.ds(c0, blk), rows]
            vt2 = vt_ref[0, rows, pl.ds(c0, blk)]
            pvs, alphas = [], []
            for e in range(2):
                h = 2 * p + e
                st = jnp.dot(k2, qtm_scr[h], preferred_element_type=F32)
                if own:
                    st = jnp.where(key_le_query, st, _NEG_INF)
                else:
                    st = jnp.where(sel_scr[h, pl.ds(n, 1), :] > 0.0, st, _NEG_INF)
                mx = jnp.max(st, axis=0, keepdims=True)
                if own:
                    m_new = mx
                else:
                    m_old = m_scr[h:h + 1, :]
                    m_new = jnp.maximum(m_old, mx)
                    alphas.append(jnp.exp(m_old - m_new))
                pt = jnp.exp(st - m_new)
                psum = jnp.sum(pt, axis=0, keepdims=True)
                l_scr[h:h + 1, :] = psum if own else l_scr[h:h + 1, :] * alphas[e] + psum
                m_scr[h:h + 1, :] = m_new
                pvs.append(jnp.dot(vt2, pt.astype(BF16), preferred_element_type=F32))
            pv = jnp.where(lo_rows, pvs[0], pvs[1])
            acc_scr[p] = pv if own else acc_scr[p] * jnp.where(lo_rows, alphas[0], alphas[1]) + pv

    attend(qb, True)

    def body(n, c):
        attend(n, False)
        return c

    lax.fori_loop(0, qb, body, 0)
    for p in range(n_pairs):
        den = jnp.where(lo_rows, l_scr[2 * p:2 * p + 1, :], l_scr[2 * p + 1:2 * p + 2, :])
        o_ref[:, p * LANES:(p + 1) * LANES] = (acc_scr[p] / den).T


def _moba_prompt(qt, kb, vtb, kbar, n_batch, n_tok):
    n_blocks = n_tok // MOBA_BLOCK
    width = MOBA_HEADS * MOBA_HD
    return pl.pallas_call(
        functools.partial(_moba_prompt_kernel, n_blocks=n_blocks),
        grid=(n_batch, n_blocks),
        in_specs=[pl.BlockSpec((1, width, MOBA_BLOCK), lambda b, i: (b, 0, i)),
                  pl.BlockSpec((n_tok, width), lambda b, i: (b, 0)),
                  pl.BlockSpec((1, width, n_tok), lambda b, i: (b, 0, 0)),
                  pl.BlockSpec((n_blocks, width), lambda b, i: (b, 0))],
        out_specs=pl.BlockSpec((MOBA_BLOCK, width), lambda b, i: (b * n_blocks + i, 0)),
        out_shape=jax.ShapeDtypeStruct((n_batch * n_tok, width), F32),
        scratch_shapes=[pltpu.VMEM((MOBA_HEADS, n_blocks, MOBA_BLOCK), F32),
                        pltpu.VMEM((MOBA_HEADS, LANES, MOBA_BLOCK), BF16),
                        pltpu.VMEM((MOBA_HEADS, MOBA_BLOCK), F32), pltpu.VMEM((MOBA_HEADS, MOBA_BLOCK), F32),
                        pltpu.VMEM((MOBA_HEADS // 2, LANES, MOBA_BLOCK), F32)],
        compiler_params=_cparams("parallel", "arbitrary"),
        name="moba_prompt",
    )(qt, kb, vtb, kbar)


PAGES_PER_BLOCK = MOBA_BLOCK // PAGE_SIZE
KBAR_BLOCKS_PER_STEP = 8


def _kbar_kernel(pt_ref, *refs):
    page_refs, o_ref = refs[:-1], refs[-1]
    for j in range(KBAR_BLOCKS_PER_STEP):
        tot = page_refs[j * PAGES_PER_BLOCK][0]
        for half in range(1, PAGES_PER_BLOCK):
            tot = tot + page_refs[j * PAGES_PER_BLOCK + half][0]
        o_ref[0, j] = jnp.sum(tot, axis=-1) * (1.0 / MOBA_BLOCK)


def _moba_kbar(pool_k, page_table):
    n_seq, n_pages = page_table.shape
    n_blocks = n_pages // PAGES_PER_BLOCK
    ppstep = KBAR_BLOCKS_PER_STEP * PAGES_PER_BLOCK

    def page_spec(j):
        return pl.BlockSpec((1, MOBA_HEADS, MOBA_HD, PAGE_SIZE),
                            lambda b, i, pt: (pt[b * n_pages + i * ppstep + j], 0, 0, 0))

    return pl.pallas_call(
        _kbar_kernel,
        grid_spec=pltpu.PrefetchScalarGridSpec(
            num_scalar_prefetch=1,
            grid=(n_seq, n_blocks // KBAR_BLOCKS_PER_STEP),
            in_specs=[page_spec(j) for j in range(ppstep)],
            out_specs=pl.BlockSpec((1, KBAR_BLOCKS_PER_STEP, MOBA_HEADS, MOBA_HD), lambda b, i, pt: (b, i, 0, 0)),
        ),
        out_shape=jax.ShapeDtypeStruct((n_seq, n_blocks, MOBA_HEADS, MOBA_HD), F32),
        compiler_params=_cparams("parallel", "parallel"),
        name="moba_kbar",
    )(page_table.reshape(-1), *([pool_k] * ppstep))


def _moba_select_kernel(q_ref, kbar_ref, o_ref, *, n_blocks, n_tok):
    lane = _lane_iota((n_tok, LANES))
    lane_f = lane.astype(F32)
    pad = jnp.zeros((LANES - n_blocks, LANES), F32)
    for h in range(MOBA_HEADS):
        p, e = divmod(h, 2)
        cols = slice(p * LANES, (p + 1) * LANES)
        q2 = q_ref[:, cols]
        head_lanes = (lane < MOBA_HD) if e == 0 else (lane >= MOBA_HD)
        qm = jnp.where(head_lanes, q2, 0.0)
        kb = jnp.concatenate([kbar_ref[0, :, cols], pad], axis=0)
        gate = jnp.where(lane < n_blocks, _dot_nt(qm, kb), _NEG_INF)
        res = jnp.zeros((n_tok, LANES), F32)
        for j in range(MOBA_TOPK):
            mx = jnp.max(gate, axis=1, keepdims=True)
            idx = jnp.min(jnp.where(gate == mx, lane_f, float(LANES)), axis=1, keepdims=True)
            res = jnp.where(lane == j, idx, res)
            gate = jnp.where(lane_f == idx, _NEG_INF, gate)
        o_ref[0, h * n_tok:(h + 1) * n_tok, :] = res.astype(I32)


def _moba_select(mq, kbar, n_seq, n_tok):
    n_blocks = kbar.shape[1]
    kb = kbar.reshape(n_seq, n_blocks, 512)
    out = pl.pallas_call(
        functools.partial(_moba_select_kernel, n_blocks=n_blocks, n_tok=n_tok),
        grid=(n_seq,),
        in_specs=[pl.BlockSpec((n_tok, 512), lambda b: (b, 0)), pl.BlockSpec((1, n_blocks, 512), lambda b: (b, 0, 0))],
        out_specs=pl.BlockSpec((1, MOBA_HEADS * n_tok, LANES), lambda b: (b, 0, 0)),
        out_shape=jax.ShapeDtypeStruct((n_seq, MOBA_HEADS * n_tok, LANES), I32),
        compiler_params=_cparams("parallel"),
        name="moba_select",
    )(mq, kb)
    return out[:, :, :MOBA_TOPK].reshape(-1)


def _moba_sample_kernel(sel_ref, pt_ref, q_ref, kn_ref, vn_ref, pk_ref, pv_ref, o_ref,
                        kbuf, vbuf, sem, *, n_tok, n_pages):
    s = pl.program_id(0)
    n_steps = pl.num_programs(0)
    n_sel = n_tok * MOBA_TOPK
    past = n_sel * MOBA_BLOCK

    def copies(step, slot):
        b = step // MOBA_HEADS
        h = step % MOBA_HEADS
        out = []
        for i in range(n_sel):
            blk = sel_ref[step * n_sel + i]
            for half in range(PAGES_PER_BLOCK):
                page = pt_ref[b * n_pages + blk * PAGES_PER_BLOCK + half]
                c0 = i * MOBA_BLOCK + half * PAGE_SIZE
                out.append(pltpu.make_async_copy(pk_ref.at[page, h], kbuf.at[slot, :, pl.ds(c0, PAGE_SIZE)], sem.at[slot, 0]))
                out.append(pltpu.make_async_copy(pv_ref.at[page, h], vbuf.at[slot, :, pl.ds(c0, PAGE_SIZE)], sem.at[slot, 1]))
        return out

    slot = s % 2

    @pl.when(s == 0)
    def _():
        for cp in copies(s, slot):
            cp.start()

    @pl.when(s + 1 < n_steps)
    def _():
        for cp in copies(s + 1, 1 - slot):
            cp.start()

    for cp in copies(s, slot):
        cp.wait()

    q = q_ref[0, 0] * (MOBA_HD ** -0.5)
    s_past = _dot(q, kbuf[slot])
    key = _lane_iota((n_tok, past))
    first_key = _row_iota((n_tok, past)) * (MOBA_TOPK * MOBA_BLOCK)
    s_past = jnp.where((key >= first_key) & (key < first_key + MOBA_TOPK * MOBA_BLOCK), s_past, _NEG_INF)
    s_own = _dot_nt(q, kn_ref[0, 0])
    s_own = jnp.where(_lane_iota((n_tok, n_tok)) <= _row_iota((n_tok, n_tok)), s_own, _NEG_INF)
    mx = jnp.maximum(jnp.max(s_past, axis=1, keepdims=True), jnp.max(s_own, axis=1, keepdims=True))
    p_past = jnp.exp(s_past - mx)
    p_own = jnp.exp(s_own - mx)
    den = jnp.sum(p_past, axis=1, keepdims=True) + jnp.sum(p_own, axis=1, keepdims=True)
    o = _dot_nt(p_past, vbuf[slot]) + _dot(p_own, vn_ref[0, 0])
    o_ref[0, 0] = o / den


def _moba_sample(mq, mk, mv, pool_k, pool_v, page_table, n_seq, n_tok):
    n_pages = page_table.shape[1]
    kbar = _moba_kbar(pool_k, page_table)
    sel = _moba_select(mq, kbar, n_seq, n_tok)

    def head_major(a):
        return a.reshape(n_seq, n_tok, MOBA_HEADS, MOBA_HD).transpose(0, 2, 1, 3)

    qh, kh, vh = head_major(mq), head_major(mk), head_major(mv)
    n_sel = n_tok * MOBA_TOPK
    spec = pl.BlockSpec((1, 1, n_tok, MOBA_HD), lambda s, sel, pt: (s // MOBA_HEADS, s % MOBA_HEADS, 0, 0))
    anyspec = pl.BlockSpec(memory_space=pl.ANY)
    o = pl.pallas_call(
        functools.partial(_moba_sample_kernel, n_tok=n_tok, n_pages=n_pages),
        grid_spec=pltpu.PrefetchScalarGridSpec(
            num_scalar_prefetch=2,
            grid=(n_seq * MOBA_HEADS,),
            in_specs=[spec, spec, spec, anyspec, anyspec],
            out_specs=spec,
            scratch_shapes=[pltpu.VMEM((2, MOBA_HD, n_sel * MOBA_BLOCK), F32),
                            pltpu.VMEM((2, MOBA_HD, n_sel * MOBA_BLOCK), F32),
                            pltpu.SemaphoreType.DMA((2, 2))],
        ),
        out_shape=jax.ShapeDtypeStruct(qh.shape, F32),
        compiler_params=_cparams("arbitrary"),
        name="moba_sample",
    )(sel, page_table.reshape(-1), qh, kh, vh, pool_k, pool_v)
    return o.transpose(0, 2, 1, 3).reshape(n_seq * n_tok, MOBA_HEADS * MOBA_HD)


def _outproj_kernel(*refs, n_in):
    x_ref, a_refs, w_refs, o_ref = refs[0], refs[1:1 + n_in], refs[1 + n_in:1 + 2 * n_in], refs[-1]
    acc = x_ref[...]
    for a_ref, w_ref in zip(a_refs, w_refs):
        acc = acc + jnp.dot(a_ref[...].astype(BF16), w_ref[...], preferred_element_type=F32)
    o_ref[...] = acc


def _outproj(x, acts, ws, tm):
    n = x.shape[0]
    row = lambda i: (i, 0)
    const = lambda i: (0, 0)
    return pl.pallas_call(
        functools.partial(_outproj_kernel, n_in=len(acts)),
        grid=(n // tm,),
        in_specs=[pl.BlockSpec((tm, D_MODEL), row)] + [pl.BlockSpec((tm, a.shape[1]), row) for a in acts]
                 + [pl.BlockSpec(w.shape, const) for w in ws],
        out_specs=pl.BlockSpec((tm, D_MODEL), row),
        out_shape=jax.ShapeDtypeStruct((n, D_MODEL), F32),
        compiler_params=_cparams("parallel"),
        name="outproj",
    )(x, *acts, *ws)


ROUTE_ROWS = 32


def _router_kernel(h_ref, g_ref, wr_ref, before_ref, route_ref, cnt_ref, cnt_scr):
    i = pl.program_id(0)

    @pl.when(i == 0)
    def _():
        cnt_scr[...] = jnp.zeros(cnt_scr.shape, F32)

    xn = _rmsnorm(h_ref[...], g_ref[...])
    logits = _dot(xn, wr_ref[...]).T[:ROUTE_ROWS, :]
    tm = logits.shape[1]
    lane = _row_iota((ROUTE_ROWS, tm)).astype(F32)
    big = float(LANES)

    def first_argmax(v):
        mx = jnp.max(v, axis=0, keepdims=True)
        return mx, jnp.min(jnp.where(v == mx, lane, big), axis=0, keepdims=True)

    is_group = lane < float(MOE_GROUPS)
    lg = jnp.where(is_group, logits, _NEG_INF)
    mg, gi = first_argmax(lg)
    wg = 1.0 / jnp.sum(jnp.where(is_group, jnp.exp(lg - mg), 0.0), axis=0, keepdims=True)
    e0 = float(MOE_GROUPS) + float(MOE_EPG) * gi
    in_group = (lane >= e0) & (lane < e0 + float(MOE_EPG))
    le = jnp.where(in_group, logits, _NEG_INF)
    m1, i1 = first_argmax(le)
    m2, i2 = first_argmax(jnp.where(lane == i1, _NEG_INF, le))
    r = jnp.exp(m2 - m1)
    g1 = wg / (1.0 + r)
    g2 = wg * r / (1.0 + r)
    a = jnp.minimum(i1, i2) - e0
    b = jnp.maximum(i1, i2) - e0
    ga = jnp.where(i1 < i2, g1, g2)
    gb = jnp.where(i1 < i2, g2, g1)
    pair = a * (7.0 - a) * 0.5 + (b - a - 1.0)
    cls = gi * 6.0 + pair
    onehot = jnp.where(lane == cls, 1.0, 0.0)
    before = jnp.dot(onehot.astype(BF16), before_ref[...], preferred_element_type=F32) + cnt_scr[:, 0:1]
    rank = jnp.sum(onehot * before, axis=0, keepdims=True)
    cnt_scr[...] = cnt_scr[...] + jnp.sum(onehot, axis=1, keepdims=True)
    out_row = _row_iota(route_ref.shape)
    route_ref[...] = jnp.where(out_row == 0, cls, jnp.where(out_row == 1, ga, jnp.where(out_row == 2, gb,
                               jnp.where(out_row == 3, rank, 0.0))))
    cnt_ref[...] = cnt_scr[...]


def _router(h, g, wr, tm):
    n = h.shape[0]
    earlier = (jnp.arange(tm)[:, None] < jnp.arange(tm)[None, :]).astype(BF16)
    row = lambda i: (i, 0)
    const = lambda i: (0, 0)
    return pl.pallas_call(
        _router_kernel,
        grid=(n // tm,),
        in_specs=[pl.BlockSpec((tm, D_MODEL), row), pl.BlockSpec((1, D_MODEL), const),
                  pl.BlockSpec((D_MODEL, LANES), const), pl.BlockSpec((tm, tm), const)],
        out_specs=[pl.BlockSpec((8, tm), lambda i: (0, i)), pl.BlockSpec((ROUTE_ROWS, LANES), const)],
        out_shape=[jax.ShapeDtypeStruct((8, n), F32), jax.ShapeDtypeStruct((ROUTE_ROWS, LANES), F32)],
        scratch_shapes=[pltpu.VMEM((ROUTE_ROWS, LANES), F32)],
        compiler_params=_cparams("arbitrary"),
        name="moe_router",
    )(h, g, wr, earlier)


def _ffn_kernel(ea_ref, eb_ref, nv_ref, tok_ref, gates_ref, g_ref, wga_ref, wua_ref, wda_ref, wgb_ref, wub_ref, wdb_ref,
                h_hbm, out_hbm, xbuf, ybuf, row_sized, gsem, ssem):
    i = pl.program_id(0)
    n_steps = pl.num_programs(0)
    slot = i % 2

    def gather_tile(tile, buf_slot):
        for r in range(MOE_TILE):
            tok = tok_ref[tile * MOE_TILE + r]
            pltpu.make_async_copy(h_hbm.at[pl.ds(tok, 1), :], xbuf.at[buf_slot, pl.ds(r, 1), :], gsem.at[buf_slot]).start()

    def wait_gather(buf_slot):
        pltpu.make_async_copy(h_hbm.at[pl.ds(0, MOE_TILE), :], xbuf.at[buf_slot], gsem.at[buf_slot]).wait()

    def scatter_tile(tile, buf_slot):
        def body(r, c):
            tok = tok_ref[tile * MOE_TILE + r]
            pltpu.make_async_copy(ybuf.at[buf_slot, pl.ds(r, 1), :], out_hbm.at[pl.ds(tok, 1), :], ssem.at[buf_slot]).start()
            return c
        lax.fori_loop(0, nv_ref[tile], body, 0)

    def wait_scatter(tile, buf_slot):
        n = nv_ref[tile]

        @pl.when(n > 0)
        def _():
            pltpu.make_async_copy(row_sized.at[pl.ds(0, n)], row_sized.at[pl.ds(0, n)], ssem.at[buf_slot]).wait()

    @pl.when(i == 0)
    def _():
        gather_tile(0, 0)

    @pl.when((i == 0) | (nv_ref[jnp.maximum(i - 1, 0)] > 0))
    def _():
        wait_gather(slot)

    @pl.when(i >= 2)
    def _():
        wait_scatter(i - 2, slot)

    @pl.when(nv_ref[i] > 0)
    def _():
        gather_tile(i + 1, 1 - slot)
        x = xbuf[slot]
        xb = _rmsnorm(x, g_ref[...]).astype(BF16)
        y = x
        for lane, wg_ref, wu_ref, wd_ref in ((0, wga_ref, wua_ref, wda_ref), (1, wgb_ref, wub_ref, wdb_ref)):
            gate = gates_ref[:, lane:lane + 1]
            hid = jax.nn.silu(jnp.dot(xb, wg_ref[0], preferred_element_type=F32)) * jnp.dot(xb, wu_ref[0], preferred_element_type=F32)
            y = y + jnp.dot((hid * gate).astype(BF16), wd_ref[0], preferred_element_type=F32)
        ybuf[slot] = y
        scatter_tile(i, slot)

    @pl.when(i == n_steps - 1)
    def _():
        wait_scatter(i - 1, 1 - slot)


def _ffn(ea, eb, nvalid, tok, gates, h, g, wg, wu, wd):
    n_tiles = tok.shape[0] // MOE_TILE
    tile = lambda i, *_: (i, 0)
    const = lambda i, *_: (0, 0)
    up = lambda sel: pl.BlockSpec((1, D_MODEL, EXPERT_FF), sel)
    down = lambda sel: pl.BlockSpec((1, EXPERT_FF, D_MODEL), sel)
    sel_a = lambda i, ea, eb, nv, tok: (ea[i], 0, 0)
    sel_b = lambda i, ea, eb, nv, tok: (eb[i], 0, 0)
    anyspec = pl.BlockSpec(memory_space=pl.ANY)
    return pl.pallas_call(
        _ffn_kernel,
        grid_spec=pltpu.PrefetchScalarGridSpec(
            num_scalar_prefetch=4,
            grid=(n_tiles,),
            in_specs=[pl.BlockSpec((MOE_TILE, 2), tile), pl.BlockSpec((1, D_MODEL), const),
                      up(sel_a), up(sel_a), down(sel_a), up(sel_b), up(sel_b), down(sel_b), anyspec],
            out_specs=anyspec,
            scratch_shapes=[pltpu.VMEM((2, MOE_TILE, D_MODEL), F32), pltpu.VMEM((2, MOE_TILE, D_MODEL), F32),
                            pltpu.VMEM((MOE_TILE, D_MODEL // LANES, LANES), F32),
                            pltpu.SemaphoreType.DMA((2,)), pltpu.SemaphoreType.DMA((2,))],
        ),
        out_shape=jax.ShapeDtypeStruct(h.shape, F32),
        compiler_params=_cparams("arbitrary"),
        name="moe_ffn",
    )(ea, eb, nvalid, tok, gates, g, wg, wu, wd, wg, wu, wd, h)


_PAIR_A = (0, 0, 0, 1, 1, 2)
_PAIR_B = (1, 2, 3, 2, 3, 3)


def _hier_moe(h, g, wr, wg, wu, wd):
    n = h.shape[0]
    tile = MOE_TILE
    route, cnt = _router(h, g, wr, tile)
    cls = route[0].astype(I32)
    rank = route[3].astype(I32)
    counts = cnt[:N_CLASSES, 0].astype(I32)
    tiles_c = (counts + tile - 1) // tile
    tile_end = jnp.cumsum(tiles_c)
    tile_start = tile_end - tiles_c
    pos = tile_start[cls] * tile + rank
    n_tiles = n // tile + N_CLASSES
    ti = jnp.arange(n_tiles, dtype=I32)
    total = tile_end[-1]
    tcls = jnp.searchsorted(tile_end, jnp.minimum(ti, total - 1), side="right").astype(I32)
    tcls = jnp.minimum(tcls, N_CLASSES - 1)
    nvalid = jnp.where(ti < total, jnp.clip(counts[tcls] - (ti - tile_start[tcls]) * tile, 0, tile), 0).astype(I32)
    grp, pair = tcls // 6, tcls % 6
    ea = grp * MOE_EPG + jnp.asarray(_PAIR_A, I32)[pair]
    eb = grp * MOE_EPG + jnp.asarray(_PAIR_B, I32)[pair]
    per_token = jnp.stack([jnp.arange(n, dtype=I32), lax.bitcast_convert_type(route[1], I32),
                           lax.bitcast_convert_type(route[2], I32)], axis=1)
    per_slot = jnp.zeros((n_tiles * tile, 3), I32).at[pos].set(per_token)
    tok = per_slot[:, 0]
    gates = lax.bitcast_convert_type(per_slot[:, 1:3], F32)
    return _ffn(ea, eb, nvalid, tok, gates, h, g, wg, wu, wd)


def _odd_inproj_kernel(x_ref, g_ref, wz_ref, wx_ref, wdt_ref, z_ref, xbc_ref, dt_ref):
    xb = _rmsnorm(x_ref[...], g_ref[...]).astype(BF16)
    z_ref[...] = jnp.dot(xb, wz_ref[...], preferred_element_type=F32)
    xbc_ref[...] = jnp.dot(xb, wx_ref[...], preferred_element_type=F32)
    dt_ref[...] = jnp.dot(xb, wdt_ref[...], preferred_element_type=F32)


def _odd_inproj(x, g, wz, wx, wdt, tm):
    n = x.shape[0]
    row = lambda i: (i, 0)
    const = lambda i: (0, 0)
    widths = (D_INNER, CONV_DIM, LANES)
    return pl.pallas_call(
        _odd_inproj_kernel,
        grid=(n // tm,),
        in_specs=[pl.BlockSpec((tm, D_MODEL), row), pl.BlockSpec((1, D_MODEL), const),
                  pl.BlockSpec(wz.shape, const), pl.BlockSpec(wx.shape, const), pl.BlockSpec(wdt.shape, const)],
        out_specs=[pl.BlockSpec((tm, c), row) for c in widths],
        out_shape=[jax.ShapeDtypeStruct((n, c), F32) for c in widths],
        compiler_params=_cparams("parallel"),
        name="odd_inproj",
    )(x, g, wz, wx, wdt)


CONV_COLS = 512


def _conv_kernel(xbc_ref, dt_ref, cs_ref, w_ref, b_ref, dtb_ref, xc_ref, dts_ref, cso_ref, carry):
    c = pl.program_id(1)
    tt = xbc_ref.shape[0]
    tail = CONV_W - 1

    @pl.when(c == 0)
    def _():
        carry[...] = jnp.zeros(carry.shape, F32)
        carry[8 - tail:8, :] = cs_ref[0]

    for j in range(CONV_DIM // CONV_COLS):
        cols = slice(j * CONV_COLS, (j + 1) * CONV_COLS)
        x = xbc_ref[:, cols]
        full = jnp.concatenate([carry[:, cols], x], axis=0)
        acc = b_ref[:, cols] + x * w_ref[tail:tail + 1, cols]
        for k in range(tail):
            acc = acc + full[8 - tail + k:8 - tail + k + tt, :] * w_ref[k:k + 1, cols]
        xc_ref[:, cols] = jax.nn.silu(acc)
        new_tail = full[8 + tt - tail:8 + tt, :]
        carry[8 - tail:8, cols] = new_tail
        cso_ref[0, :, cols] = new_tail
    dts_ref[...] = jax.nn.softplus(dt_ref[...] + dtb_ref[...])


def _conv(xbc, dt, conv_state, w, b, dtb, n_batch, n_tok, tt):
    n_chunks = n_tok // tt
    blk = lambda i, c: (i * n_chunks + c, 0)
    const = lambda i, c: (0, 0)
    st = lambda i, c: (i, 0, 0)
    return pl.pallas_call(
        _conv_kernel,
        grid=(n_batch, n_chunks),
        in_specs=[pl.BlockSpec((tt, CONV_DIM), blk), pl.BlockSpec((tt, LANES), blk),
                  pl.BlockSpec((1, CONV_W - 1, CONV_DIM), st),
                  pl.BlockSpec((CONV_W, CONV_DIM), const), pl.BlockSpec((1, CONV_DIM), const), pl.BlockSpec((1, LANES), const)],
        out_specs=[pl.BlockSpec((tt, CONV_DIM), blk), pl.BlockSpec((tt, LANES), blk),
                   pl.BlockSpec((1, CONV_W - 1, CONV_DIM), st)],
        out_shape=[jax.ShapeDtypeStruct(xbc.shape, F32), jax.ShapeDtypeStruct(dt.shape, F32),
                   jax.ShapeDtypeStruct(conv_state.shape, F32)],
        scratch_shapes=[pltpu.VMEM((8, CONV_DIM), F32)],
        compiler_params=_cparams("parallel", "arbitrary"),
        name="ssd_conv",
    )(xbc, dt, conv_state, w, b, dtb)


def _ssd_kernel(xs_ref, bm_ref, cm_ref, z_ref, dt_ref, s0_ref, a_ref, dsk_ref, ng_ref, tril_ref, mask_ref, elast_ref,
                o_ref, sout_ref, s_scr, yz_scr, *, n_seq, seq_len, gps):
    c = pl.program_id(2)

    @pl.when(c == 0)
    def _():
        s_scr[...] = s0_ref[...]

    lane = _lane_iota((ROWS, LANES))
    row = _row_iota((ROWS, LANES))
    lo_half = lane < (LANES // 2)
    dt = dt_ref[...]
    cs = _dot01(tril_ref[...], dt * a_ref[...])
    cs_last = _dot01(elast_ref[...], cs)
    if gps != SSM_GROUPS:
        shift = (LANES - pl.program_id(1) * (gps * SSM_HEADS // SSM_GROUPS)) % LANES
        dt, cs, cs_last = (pltpu.roll(v, shift, 1) for v in (dt, cs, cs_last))
    cs_t = cs.T
    ecs = jnp.exp(cs)
    dt_tail = dt * jnp.exp(cs_last - cs)
    causal = mask_ref[...] > 0.0

    def per_head(v, h0):
        return jnp.where(lo_half, v[:, h0:h0 + 1], v[:, h0 + 1:h0 + 2])

    for g in range(gps):
        gcols = slice(g * D_STATE, (g + 1) * D_STATE)
        cg = cm_ref[:, gcols].astype(BF16)
        bg = bm_ref[:, gcols].astype(BF16)
        cb = _dot_nt(cg, bg)
        for pp in range(2):
            pair = 2 * g + pp
            h0 = 2 * pair
            pcols = slice(pair * LANES, (pair + 1) * LANES)
            x2 = xs_ref[:, pcols]
            xdt = x2 * per_head(dt, h0)
            ys = []
            for e in range(2):
                h = h0 + e
                seg = cs[:, h:h + 1] - cs_t[h:h + 1, :]
                wmat = jnp.exp(jnp.where(causal, seg, _NEG_INF)) * cb
                ys.append(_dot(wmat, xdt))
            y2 = jnp.where(lo_half, ys[0], ys[1])
            if n_seq == 1:
                inter = _dot_nt(cg, s_scr[0, pair])
            else:
                inter = jnp.zeros((ROWS, LANES), F32)
                for s in range(n_seq):
                    in_seq = (row >= s * seq_len) & (row < (s + 1) * seq_len)
                    inter = inter + jnp.where(in_seq, _dot_nt(cg, s_scr[s, pair]), 0.0)
            y2 = y2 + inter * per_head(ecs, h0) + dsk_ref[:, pcols] * x2
            yz_scr[:, pcols] = y2 * jax.nn.silu(z_ref[:, pcols])
            xw_t = (x2 * per_head(dt_tail, h0)).T
            for s in range(n_seq):
                last = (s + 1) * seq_len - 1
                if n_seq == 1:
                    xt = xw_t
                else:
                    xt = jnp.where((lane >= s * seq_len) & (lane <= last), xw_t, 0.0)
                dec = jnp.exp(jnp.where(row < (LANES // 2), cs_t[h0:h0 + 1, last:last + 1], cs_t[h0 + 1:h0 + 2, last:last + 1]))
                s_scr[s, pair] = s_scr[s, pair] * dec + _dot(xt, bg)
    gw = D_INNER // SSM_GROUPS
    for g in range(gps):
        cols = slice(g * gw, (g + 1) * gw)
        o_ref[:, cols] = _rmsnorm(yz_scr[:, cols], ng_ref[:, cols])

    @pl.when(c == pl.num_programs(2) - 1)
    def _():
        sout_ref[...] = s_scr[...]


def _ssd(xc, z, dts, s0, a_row, dsk_row, ng_row, n_batch, n_tok):
    seq_len = min(ROWS, n_tok)
    n_seq = ROWS // seq_len
    n_chunks = n_tok // seq_len
    r = jnp.arange(ROWS)
    li, sid = r % seq_len, r // seq_len
    mask = ((sid[:, None] == sid[None, :]) & (li[:, None] >= li[None, :]))
    elast = (r[None, :] == (sid * seq_len + seq_len - 1)[:, None])
    n_pairs = SSM_HEADS // 2
    s0p = s0.reshape(n_batch, n_pairs, LANES, D_STATE)
    gps = SSM_GROUPS if n_seq == 1 else 1
    xw = gps * (D_INNER // SSM_GROUPS)
    bw = gps * D_STATE
    rows = lambda i, j, c: i * n_chunks + c
    xblk = lambda i, j, c: (rows(i, j, c), j)
    bblk = lambda i, j, c: (rows(i, j, c), D_INNER // bw + j)
    cblk = lambda i, j, c: (rows(i, j, c), (D_INNER + SSM_GROUPS * D_STATE) // bw + j)
    dtblk = lambda i, j, c: (rows(i, j, c), 0)
    st = lambda i, j, c: (i, j, 0, 0)
    const = lambda i, j, c: (0, 0)
    gconst = lambda i, j, c: (0, j)
    state_block = (n_seq, 2 * gps, LANES, D_STATE)
    o, s_out = pl.pallas_call(
        functools.partial(_ssd_kernel, n_seq=n_seq, seq_len=seq_len, gps=gps),
        grid=(n_batch // n_seq, SSM_GROUPS // gps, n_chunks),
        in_specs=[pl.BlockSpec((ROWS, xw), xblk), pl.BlockSpec((ROWS, bw), bblk), pl.BlockSpec((ROWS, bw), cblk),
                  pl.BlockSpec((ROWS, xw), xblk), pl.BlockSpec((ROWS, LANES), dtblk),
                  pl.BlockSpec(state_block, st),
                  pl.BlockSpec((1, LANES), const), pl.BlockSpec((1, xw), gconst), pl.BlockSpec((1, xw), gconst),
                  pl.BlockSpec((ROWS, ROWS), const), pl.BlockSpec((ROWS, ROWS), const), pl.BlockSpec((ROWS, ROWS), const)],
        out_specs=[pl.BlockSpec((ROWS, xw), xblk), pl.BlockSpec(state_block, st)],
        out_shape=[jax.ShapeDtypeStruct((n_batch * n_tok, D_INNER), F32), jax.ShapeDtypeStruct(s0p.shape, F32)],
        scratch_shapes=[pltpu.VMEM(state_block, F32), pltpu.VMEM((ROWS, xw), F32)],
        compiler_params=_cparams("parallel", "parallel", "arbitrary"),
        name="ssd_scan",
    )(xc, xc, xc, z, dts, s0p, a_row, dsk_row, ng_row, mask.astype(BF16), mask.astype(F32), elast.astype(BF16))
    return o, s_out.reshape(s0.shape)


def _rope_tables(pos):
    half = MOBA_HD // 2
    inv = jnp.power(ROPE_THETA, -jnp.arange(half, dtype=F32) * 2.0 / MOBA_HD)
    ang = pos[:, None] * inv[None, :]
    cos, sin = jnp.cos(ang), jnp.sin(ang)
    reps = LANES // MOBA_HD
    return jnp.tile(jnp.concatenate([cos, cos], axis=1), (1, reps)), jnp.tile(jnp.concatenate([-sin, sin], axis=1), (1, reps))


def _pad_cols(w, width):
    return jnp.pad(w, ((0, 0), (0, width - w.shape[1])))


TOKEN_TILE = 256


def _even_layer(h, nb, nt, pos, ret0, cache, norm_g, w_in, w_out, q_g, k_g):
    tm = TOKEN_TILE
    cos, sin = _rope_tables(pos)
    if nt < tm:
        cos, sin = jnp.tile(cos, (tm // nt, 1)), jnp.tile(sin, (tm // nt, 1))
    head_tile = lambda v: jnp.tile(v, MOBA_HEADS)[None, :]
    width = MOBA_HEADS * MOBA_HD
    blockdiag = (jnp.arange(width)[:, None] // MOBA_HD == jnp.arange(width)[None, :] // MOBA_HD).astype(BF16)
    outs = _even_inproj(h, norm_g[None, :], w_in.astype(BF16), cos, sin, head_tile(q_g), head_tile(k_g), blockdiag,
                        tm, nt, key_minor=cache is None)
    rq, rk, rv, rg = outs[:4]
    if cache is None:
        qt, kb, kt, vt, vtb, kbar = outs[4:]
        mo = _moba_prompt(qt, kb, vtb, kbar.reshape(-1, width), nb, nt)
        mk, mv = (a.reshape(nb, MOBA_HEADS, MOBA_HD, nt).transpose(0, 3, 1, 2) for a in (kt, vt))
    else:
        mq, mk, mv = outs[4:]
        mo = _moba_sample(mq, mk, mv, *cache, nb, nt)
        mk, mv = (a.reshape(nb, nt, MOBA_HEADS, MOBA_HD) for a in (mk, mv))
    ro, ret_state = _retention(rq, rk, rv, rg, ret0, nb, nt)
    w_out = w_out.astype(BF16)
    h = _outproj(h, (ro, mo), (w_out[:RET_HEADS * RET_DV], w_out[RET_HEADS * RET_DV:]), tm)
    return h, ret_state, mk, mv


def _odd_layer(h, nb, nt, conv0, ssm0, norm_g, w_in, conv_w, conv_b, dt_bias, a_log, d_skip, ssd_norm, w_out):
    tm = TOKEN_TILE
    w_in = w_in.astype(BF16)
    z, xbc, dt = _odd_inproj(h, norm_g[None, :], w_in[:, :D_INNER], w_in[:, D_INNER:D_INNER + CONV_DIM],
                             _pad_cols(w_in[:, D_INNER + CONV_DIM:], LANES), tm)
    xc, dts, conv_state = _conv(xbc, dt, conv0, conv_w, conv_b[None, :], _pad_cols(dt_bias[None, :], LANES),
                                nb, nt, min(nt, 256))
    a_row = _pad_cols(-jnp.exp(a_log)[None, :], LANES)
    dsk_row = jnp.repeat(d_skip, D_INNER // SSM_HEADS)[None, :]
    yzn, ssm_state = _ssd(xc, z, dts, ssm0, a_row, dsk_row, ssd_norm[None, :], nb, nt)
    h = _outproj(h, (yzn,), (w_out.astype(BF16),), tm)
    return h, conv_state, ssm_state


def _moe_layer(h, norm_g, w_rg, w_re, w_g, w_u, w_d):
    wr = _pad_cols(jnp.concatenate([w_rg, w_re], axis=1), LANES)
    return _hier_moe(h, norm_g[None, :], wr, w_g.astype(BF16), w_u.astype(BF16), w_d.astype(BF16))


def kernel(x_prompt, x_sample, state_ret, cache_k, cache_v, page_table, state_ssm, state_conv, norm_mix, norm_ffn,
           w_in_even, w_out_even, q_norm, k_norm, w_in_odd, conv_w, conv_b, dt_bias, a_log, d_skip, ssd_norm, w_out_odd,
           w_router_group, w_router_expert, w_expert_gate, w_expert_up, w_expert_down):
    bp, seq, _ = x_prompt.shape
    bs, dec_seq, _ = x_sample.shape
    past_len = page_table.shape[1] * PAGE_SIZE
    groups = ((x_prompt.reshape(bp * seq, D_MODEL), bp, seq, jnp.arange(seq, dtype=F32)),
              (x_sample.reshape(bs * dec_seq, D_MODEL), bs, dec_seq, past_len + jnp.arange(dec_seq, dtype=F32)))

    def moe(h, layer):
        return _moe_layer(h, norm_ffn[layer], w_router_group[layer], w_router_expert[layer],
                          w_expert_gate[layer], w_expert_up[layer], w_expert_down[layer])

    outs = []
    for gi, (h, nb, nt, pos) in enumerate(groups):
        if gi == 0:
            ret0 = jnp.zeros((nb, RET_HEADS, RET_DK, RET_DV), F32)
            cache = None
            conv0 = jnp.zeros((nb, CONV_W - 1, CONV_DIM), F32)
            ssm0 = jnp.zeros((nb, SSM_HEADS, D_INNER // SSM_HEADS, D_STATE), F32)
        else:
            ret0 = state_ret[0]
            cache = (cache_k[0].transpose(0, 2, 3, 1), cache_v[0].transpose(0, 2, 3, 1), page_table)
            conv0, ssm0 = state_conv[0], state_ssm[0]
        h, ret_state, mk, mv = _even_layer(h, nb, nt, pos, ret0, cache, norm_mix[0], w_in_even[0], w_out_even[0],
                                           q_norm[0], k_norm[0])
        h = moe(h, 0)
        h, conv_state, ssm_state = _odd_layer(h, nb, nt, conv0, ssm0, norm_mix[1], w_in_odd[0], conv_w[0], conv_b[0],
                                              dt_bias[0], a_log[0], d_skip[0], ssd_norm[0], w_out_odd[0])
        h = moe(h, 1)
        outs.append(dict(h=h.reshape(nb, nt, D_MODEL), ret=ret_state[None], k=mk[None], v=mv[None],
                         ssm=ssm_state[None], conv=conv_state[None]))
    p, s = outs
    return (p["h"], s["h"], p["ret"], s["ret"], p["k"], p["v"], s["k"], s["v"], p["ssm"], s["ssm"], p["conv"], s["conv"])
```

```python
import functools
import math

import jax
import jax.numpy as jnp
from jax import lax
from jax.experimental import pallas as pl
from jax.experimental.pallas import tpu as pltpu

F32 = jnp.float32
BF16 = jnp.bfloat16
I32 = jnp.int32

D_MODEL = 1024
PAGE_SIZE = 128
RET_HEADS = 8
RET_DK = 64
RET_DV = 128
MOBA_HEADS = 8
MOBA_HD = 64
MOBA_BLOCK = 256
MOBA_TOPK = 3
ROPE_THETA = 10000.0
D_INNER = 2048
SSM_HEADS = 32
SSM_GROUPS = 8
D_STATE = 128
CONV_W = 4
CONV_DIM = 4096
MOE_GROUPS = 4
MOE_EPG = 4
N_EXPERTS = 16
EXPERT_FF = 512
NORM_EPS = 1e-6
N_CLASSES = MOE_GROUPS * 6
LANES = 128
ROWS = 128
MOE_TILE = 256
MOE_TILE_SMALL = 32
VMEM_LIMIT = 56 * 1024 * 1024

_NEG_INF = float("-inf")


def _cparams(*sem):
    return pltpu.CompilerParams(dimension_semantics=sem, vmem_limit_bytes=VMEM_LIMIT)


def _dot(a, b):
    return jnp.dot(a.astype(BF16), b.astype(BF16), preferred_element_type=F32)


def _dot_nt(a, b):
    return lax.dot_general(a.astype(BF16), b.astype(BF16), (((1,), (1,)), ((), ())), preferred_element_type=F32)


def _split3(x):
    hi = x.astype(BF16)
    r = x - hi.astype(F32)
    mid = r.astype(BF16)
    lo = (r - mid.astype(F32)).astype(BF16)
    return hi, mid, lo


def _dot01(m01, x):
    hi, mid, lo = _split3(x)
    return (jnp.dot(m01, hi, preferred_element_type=F32) + jnp.dot(m01, mid, preferred_element_type=F32)
            + jnp.dot(m01, lo, preferred_element_type=F32))


def _rmsnorm(x, g=None):
    y = x * lax.rsqrt(jnp.mean(x * x, axis=-1, keepdims=True) + NORM_EPS)
    return y if g is None else y * g


def _lane_iota(shape):
    return lax.broadcasted_iota(I32, shape, len(shape) - 1)


def _row_iota(shape):
    return lax.broadcasted_iota(I32, shape, len(shape) - 2)


def _even_inproj_kernel(x_ref, g_ref, w_ref, cos_ref, sin_ref, qg_ref, kg_ref, bd_ref,
                        rq_ref, rk_ref, rv_ref, rg_ref, *moba_refs, key_minor):
    xb = _rmsnorm(x_ref[...], g_ref[...]).astype(BF16)
    cos = cos_ref[...]
    sin = sin_ref[...]
    first_half = (_lane_iota(cos.shape) % MOBA_HD) < (MOBA_HD // 2)

    def proj(c0, c1):
        return jnp.dot(xb, w_ref[:, c0:c1], preferred_element_type=F32)

    def rope(a):
        outs = []
        for j in range(a.shape[1] // LANES):
            s = a[:, j * LANES:(j + 1) * LANES]
            rot = jnp.where(first_half, pltpu.roll(s, LANES - MOBA_HD // 2, 1), pltpu.roll(s, MOBA_HD // 2, 1))
            outs.append(s * cos + rot * sin)
        return jnp.concatenate(outs, axis=1)

    def qk_norm(a, gam):
        ss = jnp.dot((a * a).astype(BF16), bd_ref[...], preferred_element_type=F32)
        return a * lax.rsqrt(ss * (1.0 / MOBA_HD) + NORM_EPS) * gam

    rq_ref[...] = rope(proj(0, 512))
    rk_ref[...] = rope(proj(512, 1024)) * (RET_DK ** -0.5)
    rv_ref[...] = proj(1024, 2048)
    rg_ref[...] = proj(2048, 3072)
    mq = rope(qk_norm(proj(3072, 3584), qg_ref[...]))
    mk = rope(qk_norm(proj(3584, 4096), kg_ref[...]))
    mv = proj(4096, 4608)
    if not key_minor:
        mq_ref, mk_ref, mv_ref = moba_refs
        mq_ref[...] = mq
        mk_ref[...] = mk
        mv_ref[...] = mv
    else:
        qt_ref, kb_ref, kt_ref, vt_ref, vtb_ref, kbar_ref = moba_refs
        qt_ref[0] = mq.T
        kb_ref[...] = mk.astype(BF16)
        kt_ref[0] = mk.T
        vt = mv.T
        vt_ref[0] = vt
        vtb_ref[0] = vt.astype(BF16)
        kbar_ref[0] = jnp.mean(mk, axis=0, keepdims=True)


def _even_inproj(x, g, w, cos, sin, qg, kg, bd, tm, n_tok, key_minor):
    n = x.shape[0]
    n_pos_blocks = cos.shape[0] // tm
    row = lambda i: (i, 0)
    const = lambda i: (0, 0)
    width = MOBA_HEADS * MOBA_HD
    out_specs = [pl.BlockSpec((tm, c), row) for c in (512, 512, 1024, 1024)]
    out_shape = [jax.ShapeDtypeStruct((n, c), F32) for c in (512, 512, 1024, 1024)]
    if not key_minor:
        out_specs += [pl.BlockSpec((tm, width), row)] * 3
        out_shape += [jax.ShapeDtypeStruct((n, width), F32)] * 3
    else:
        assert tm == MOBA_BLOCK and n_tok % tm == 0
        tiles = n_tok // tm
        tspec = pl.BlockSpec((1, width, tm), lambda i: (i // tiles, 0, i % tiles))
        tshape = lambda dt: jax.ShapeDtypeStruct((n // n_tok, width, n_tok), dt)
        out_specs += [tspec, pl.BlockSpec((tm, width), row), tspec, tspec, tspec, pl.BlockSpec((1, 1, width), lambda i: (i, 0, 0))]
        out_shape += [tshape(F32), jax.ShapeDtypeStruct((n, width), BF16), tshape(F32), tshape(F32), tshape(BF16),
                      jax.ShapeDtypeStruct((n // tm, 1, width), F32)]
    return pl.pallas_call(
        functools.partial(_even_inproj_kernel, key_minor=key_minor),
        grid=(n // tm,),
        in_specs=[pl.BlockSpec((tm, D_MODEL), row), pl.BlockSpec((1, D_MODEL), const),
                  pl.BlockSpec(w.shape, const),
                  pl.BlockSpec((tm, LANES), lambda i: (i % n_pos_blocks, 0)),
                  pl.BlockSpec((tm, LANES), lambda i: (i % n_pos_blocks, 0)),
                  pl.BlockSpec((1, 512), const), pl.BlockSpec((1, 512), const), pl.BlockSpec((512, 512), const)],
        out_specs=out_specs,
        out_shape=out_shape,
        compiler_params=_cparams("parallel"),
        name="even_inproj",
    )(x, g, w, cos, sin, qg, kg, bd)


def _retention_kernel(rq_ref, rk_ref, rv_ref, rg_ref, s0_ref, dmat_ref, qdec_ref, kdec_ref, sdec_ref,
                      o_ref, sout_ref, s_scr, *, n_seq, seq_len):
    c = pl.program_id(1)

    @pl.when(c == 0)
    def _():
        s_scr[...] = s0_ref[...]

    lane = _lane_iota((ROWS, LANES))
    row = _row_iota((ROWS, LANES))
    for p in range(RET_HEADS // 2):
        q2 = rq_ref[:, p * LANES:(p + 1) * LANES]
        k2 = rk_ref[:, p * LANES:(p + 1) * LANES]
        kd_t = (k2 * kdec_ref[p]).T
        v_pair = []
        for e in range(2):
            h = 2 * p + e
            head_lanes = (lane < RET_DK) if e == 0 else (lane >= RET_DK)
            qm = jnp.where(head_lanes, q2, 0.0)
            scores = _dot_nt(qm, k2) * dmat_ref[h]
            vh = rv_ref[:, h * RET_DV:(h + 1) * RET_DV]
            v_pair.append(vh)
            if n_seq == 1:
                inter = _dot(qm, s_scr[0, p])
            else:
                inter = jnp.zeros((ROWS, RET_DV), F32)
                for s in range(n_seq):
                    in_seq = (row >= s * seq_len) & (row < (s + 1) * seq_len)
                    inter = inter + jnp.where(in_seq, _dot(qm, s_scr[s, p]), 0.0)
            o = _dot(scores, vh) + inter * qdec_ref[h]
            gate = rg_ref[:, h * RET_DV:(h + 1) * RET_DV]
            o_ref[:, h * RET_DV:(h + 1) * RET_DV] = jax.nn.silu(gate) * _rmsnorm(o)
        for s in range(n_seq):
            if n_seq == 1:
                kt = kd_t
            else:
                kt = jnp.where((lane >= s * seq_len) & (lane < (s + 1) * seq_len), kd_t, 0.0)
            upd = jnp.where(row < RET_DK, _dot(kt, v_pair[0]), _dot(kt, v_pair[1]))
            s_scr[s, p] = s_scr[s, p] * sdec_ref[p] + upd

    @pl.when(c == pl.num_programs(1) - 1)
    def _():
        sout_ref[...] = s_scr[...]


def _retention_tables(seq_len):
    log_g = jnp.log1p(-jnp.exp2(-5.0 - jnp.arange(RET_HEADS, dtype=F32)))
    r = jnp.arange(ROWS)
    li = (r % seq_len).astype(F32)
    sid = r // seq_len
    diff = li[:, None] - li[None, :]
    same = sid[:, None] == sid[None, :]
    dmat = jnp.where(same[None] & (diff[None] >= 0),
                     jnp.exp(jnp.maximum(diff, 0.0)[None] * log_g[:, None, None]), 0.0)
    qdec = jnp.broadcast_to(jnp.exp((li + 1.0)[None, :, None] * log_g[:, None, None]), (RET_HEADS, ROWS, RET_DV))
    lane_head = jnp.arange(LANES) // RET_DK
    pair_log_g = log_g.reshape(RET_HEADS // 2, 2)[:, lane_head]
    kdec = jnp.exp((seq_len - 1.0 - li)[None, :, None] * pair_log_g[:, None, :])
    sdec = jnp.broadcast_to(jnp.exp(float(seq_len) * pair_log_g)[:, :, None], (RET_HEADS // 2, LANES, RET_DV))
    return dmat, qdec, kdec, sdec


def _retention(rq, rk, rv, rg, s0, n_batch, n_tok):
    seq_len = min(ROWS, n_tok)
    n_seq = ROWS // seq_len
    n_chunks = n_tok // seq_len
    tables = _retention_tables(seq_len)
    s0p = s0.reshape(n_batch, RET_HEADS // 2, 2 * RET_DK, RET_DV)
    blk = lambda i, c: (i * n_chunks + c, 0)
    st = lambda i, c: (i, 0, 0, 0)
    c3 = lambda i, c: (0, 0, 0)
    o, s_out = pl.pallas_call(
        functools.partial(_retention_kernel, n_seq=n_seq, seq_len=seq_len),
        grid=(n_batch // n_seq, n_chunks),
        in_specs=[pl.BlockSpec((ROWS, 512), blk), pl.BlockSpec((ROWS, 512), blk),
                  pl.BlockSpec((ROWS, 1024), blk), pl.BlockSpec((ROWS, 1024), blk),
                  pl.BlockSpec((n_seq,) + s0p.shape[1:], st)]
                 + [pl.BlockSpec(t.shape, c3) for t in tables],
        out_specs=[pl.BlockSpec((ROWS, 1024), blk), pl.BlockSpec((n_seq,) + s0p.shape[1:], st)],
        out_shape=[jax.ShapeDtypeStruct((n_batch * n_tok, 1024), F32), jax.ShapeDtypeStruct(s0p.shape, F32)],
        scratch_shapes=[pltpu.VMEM((n_seq,) + s0p.shape[1:], F32)],
        compiler_params=_cparams("parallel", "arbitrary"),
        name="retention",
    )(rq, rk, rv, rg, s0p, *tables)
    return o, s_out.reshape(s0.shape)


def _moba_prompt_kernel(qt_ref, k_ref, vt_ref, kbar_ref, o_ref, sel_scr, qtm_scr, m_scr, l_scr, acc_scr, *, n_blocks):
    qb = pl.program_id(1)
    blk = MOBA_BLOCK
    n_pairs = MOBA_HEADS // 2
    blk_row = _row_iota((n_blocks, blk))
    lo_rows = _row_iota((LANES, blk)) < MOBA_HD
    lo_lanes = _lane_iota((n_blocks, LANES)) < MOBA_HD
    key_le_query = _row_iota((blk, blk)) <= _lane_iota((blk, blk))

    for p in range(n_pairs):
        rows = slice(p * LANES, (p + 1) * LANES)
        qt2 = qt_ref[0, rows, :]
        kb2 = kbar_ref[:, rows]
        for e in range(2):
            h = 2 * p + e
            gate = _dot(jnp.where(lo_lanes if e == 0 else ~lo_lanes, kb2, 0.0), qt2)
            cnt = jnp.zeros((n_blocks, blk), F32)
            for m in range(n_blocks - 1):
                gm = gate[m:m + 1, :]
                ahead = (gm > gate) | ((gm == gate) & (blk_row > m))
                cnt = cnt + jnp.where(ahead, (qb > m).astype(F32), 0.0)
            sel_scr[h] = jnp.where((blk_row < qb) & (cnt < float(MOBA_TOPK)), 1.0, 0.0)
            head_rows = lo_rows if e == 0 else ~lo_rows
            qtm_scr[h] = (jnp.where(head_rows, qt2, 0.0) * (MOBA_HD ** -0.5)).astype(BF16)

    def attend(n, own):
        c0 = pl.multiple_of(n * blk, blk)
        for p in range(n_pairs):
            rows = slice(p * LANES, (p + 1) * LANES)
            k2 = k_ref[pl.ds(c0, blk), rows]
            for e in range(2):
                h = 2 * p + e
                st = jnp.dot(k2, qtm_scr[h], preferred_element_type=F32)
                if own:
                    st = jnp.where(key_le_query, st, _NEG_INF)
                else:
                    st = jnp.where(sel_scr[h, pl.ds(n, 1), :] > 0.0, st, _NEG_INF)
                mx = jnp.max(st, axis=0, keepdims=True)
                if own:
                    m_new = mx
                else:
                    m_old = m_scr[h:h + 1, :]
                    m_new = jnp.maximum(m_old, mx)
                    alpha = jnp.exp(m_old - m_new)
                pt = jnp.exp(st - m_new)
                psum = jnp.sum(pt, axis=0, keepdims=True)
                vt = vt_ref[0, h * MOBA_HD:(h + 1) * MOBA_HD, pl.ds(c0, blk)]
                pv = jnp.dot(vt, pt.astype(BF16), preferred_element_type=F32)
                l_scr[h:h + 1, :] = psum if own else l_scr[h:h + 1, :] * alpha + psum
                acc_scr[h] = pv if own else acc_scr[h] * alpha + pv
                m_scr[h:h + 1, :] = m_new

    attend(qb, True)

    def body(n, c):
        attend(n, False)
        return c

    lax.fori_loop(0, qb, body, 0)
    for p in range(n_pairs):
        pair = [acc_scr[h] / l_scr[h:h + 1, :] for h in (2 * p, 2 * p + 1)]
        o_ref[:, p * LANES:(p + 1) * LANES] = jnp.concatenate(pair, axis=0).T


def _moba_prompt(qt, kb, vtb, kbar, n_batch, n_tok):
    n_blocks = n_tok // MOBA_BLOCK
    width = MOBA_HEADS * MOBA_HD
    return pl.pallas_call(
        functools.partial(_moba_prompt_kernel, n_blocks=n_blocks),
        grid=(n_batch, n_blocks),
        in_specs=[pl.BlockSpec((1, width, MOBA_BLOCK), lambda b, i: (b, 0, i)),
                  pl.BlockSpec((n_tok, width), lambda b, i: (b, 0)),
                  pl.BlockSpec((1, width, n_tok), lambda b, i: (b, 0, 0)),
                  pl.BlockSpec((n_blocks, width), lambda b, i: (b, 0))],
        out_specs=pl.BlockSpec((MOBA_BLOCK, width), lambda b, i: (b * n_blocks + i, 0)),
        out_shape=jax.ShapeDtypeStruct((n_batch * n_tok, width), F32),
        scratch_shapes=[pltpu.VMEM((MOBA_HEADS, n_blocks, MOBA_BLOCK), F32),
                        pltpu.VMEM((MOBA_HEADS, LANES, MOBA_BLOCK), BF16),
                        pltpu.VMEM((MOBA_HEADS, MOBA_BLOCK), F32), pltpu.VMEM((MOBA_HEADS, MOBA_BLOCK), F32),
                        pltpu.VMEM((MOBA_HEADS, MOBA_HD, MOBA_BLOCK), F32)],
        compiler_params=_cparams("parallel", "arbitrary"),
        name="moba_prompt",
    )(qt, kb, vtb, kbar)


PAGES_PER_BLOCK = MOBA_BLOCK // PAGE_SIZE
KBAR_BLOCKS_PER_STEP = 16


def _kbar_kernel(pt_ref, *refs):
    page_refs, o_ref = refs[:-1], refs[-1]
    for j in range(KBAR_BLOCKS_PER_STEP):
        tot = page_refs[j * PAGES_PER_BLOCK][0]
        for half in range(1, PAGES_PER_BLOCK):
            tot = tot + page_refs[j * PAGES_PER_BLOCK + half][0]
        o_ref[0, j] = jnp.sum(tot, axis=-1) * (1.0 / MOBA_BLOCK)


def _moba_kbar(pool_k, page_table):
    n_seq, n_pages = page_table.shape
    n_blocks = n_pages // PAGES_PER_BLOCK
    ppstep = KBAR_BLOCKS_PER_STEP * PAGES_PER_BLOCK

    def page_spec(j):
        return pl.BlockSpec((1, MOBA_HEADS, MOBA_HD, PAGE_SIZE),
                            lambda b, i, pt: (pt[b * n_pages + i * ppstep + j], 0, 0, 0))

    return pl.pallas_call(
        _kbar_kernel,
        grid_spec=pltpu.PrefetchScalarGridSpec(
            num_scalar_prefetch=1,
            grid=(n_seq, n_blocks // KBAR_BLOCKS_PER_STEP),
            in_specs=[page_spec(j) for j in range(ppstep)],
            out_specs=pl.BlockSpec((1, KBAR_BLOCKS_PER_STEP, MOBA_HEADS, MOBA_HD), lambda b, i, pt: (b, i, 0, 0)),
        ),
        out_shape=jax.ShapeDtypeStruct((n_seq, n_blocks, MOBA_HEADS, MOBA_HD), F32),
        compiler_params=_cparams("parallel", "parallel"),
        name="moba_kbar",
    )(page_table.reshape(-1), *([pool_k] * ppstep))


def _moba_select_kernel(q_ref, kbar_ref, o_ref, *, n_blocks, n_tok):
    lane = _lane_iota((n_tok, LANES))
    lane_f = lane.astype(F32)
    pad = jnp.zeros((LANES - n_blocks, LANES), F32)
    for h in range(MOBA_HEADS):
        p, e = divmod(h, 2)
        cols = slice(p * LANES, (p + 1) * LANES)
        q2 = q_ref[:, cols]
        head_lanes = (lane < MOBA_HD) if e == 0 else (lane >= MOBA_HD)
        qm = jnp.where(head_lanes, q2, 0.0)
        kb = jnp.concatenate([kbar_ref[0, :, cols], pad], axis=0)
        gate = jnp.where(lane < n_blocks, _dot_nt(qm, kb), _NEG_INF)
        res = jnp.zeros((n_tok, LANES), F32)
        for j in range(MOBA_TOPK):
            mx = jnp.max(gate, axis=1, keepdims=True)
            idx = jnp.min(jnp.where(gate == mx, lane_f, float(LANES)), axis=1, keepdims=True)
            res = jnp.where(lane == j, idx, res)
            gate = jnp.where(lane_f == idx, _NEG_INF, gate)
        o_ref[0, h * n_tok:(h + 1) * n_tok, :] = res.astype(I32)


def _moba_select(mq, kbar, n_seq, n_tok):
    n_blocks = kbar.shape[1]
    kb = kbar.reshape(n_seq, n_blocks, 512)
    out = pl.pallas_call(
        functools.partial(_moba_select_kernel, n_blocks=n_blocks, n_tok=n_tok),
        grid=(n_seq,),
        in_specs=[pl.BlockSpec((n_tok, 512), lambda b: (b, 0)), pl.BlockSpec((1, n_blocks, 512), lambda b: (b, 0, 0))],
        out_specs=pl.BlockSpec((1, MOBA_HEADS * n_tok, LANES), lambda b: (b, 0, 0)),
        out_shape=jax.ShapeDtypeStruct((n_seq, MOBA_HEADS * n_tok, LANES), I32),
        compiler_params=_cparams("parallel"),
        name="moba_select",
    )(mq, kb)
    return out[:, :, :MOBA_TOPK].reshape(-1)


def _moba_sample_kernel(sel_ref, pt_ref, q_ref, kn_ref, vn_ref, pk_ref, pv_ref, o_ref,
                        kbuf, vbuf, sem, *, n_tok, n_pages):
    s = pl.program_id(0)
    n_steps = pl.num_programs(0)
    n_sel = n_tok * MOBA_TOPK
    past = n_sel * MOBA_BLOCK

    def copies(step, slot):
        b = step // MOBA_HEADS
        h = step % MOBA_HEADS
        out = []
        for i in range(n_sel):
            blk = sel_ref[step * n_sel + i]
            for half in range(PAGES_PER_BLOCK):
                page = pt_ref[b * n_pages + blk * PAGES_PER_BLOCK + half]
                c0 = i * MOBA_BLOCK + half * PAGE_SIZE
                out.append(pltpu.make_async_copy(pk_ref.at[page, h], kbuf.at[slot, :, pl.ds(c0, PAGE_SIZE)], sem.at[slot, 0]))
                out.append(pltpu.make_async_copy(pv_ref.at[page, h], vbuf.at[slot, :, pl.ds(c0, PAGE_SIZE)], sem.at[slot, 1]))
        return out

    slot = s % 2

    @pl.when(s == 0)
    def _():
        for cp in copies(s, slot):
            cp.start()

    @pl.when(s + 1 < n_steps)
    def _():
        for cp in copies(s + 1, 1 - slot):
            cp.start()

    for cp in copies(s, slot):
        cp.wait()

    q = q_ref[0, 0] * (MOBA_HD ** -0.5)
    s_past = _dot(q, kbuf[slot])
    key = _lane_iota((n_tok, past))
    first_key = _row_iota((n_tok, past)) * (MOBA_TOPK * MOBA_BLOCK)
    s_past = jnp.where((key >= first_key) & (key < first_key + MOBA_TOPK * MOBA_BLOCK), s_past, _NEG_INF)
    s_own = _dot_nt(q, kn_ref[0, 0])
    s_own = jnp.where(_lane_iota((n_tok, n_tok)) <= _row_iota((n_tok, n_tok)), s_own, _NEG_INF)
    mx = jnp.maximum(jnp.max(s_past, axis=1, keepdims=True), jnp.max(s_own, axis=1, keepdims=True))
    p_past = jnp.exp(s_past - mx)
    p_own = jnp.exp(s_own - mx)
    den = jnp.sum(p_past, axis=1, keepdims=True) + jnp.sum(p_own, axis=1, keepdims=True)
    o = _dot_nt(p_past, vbuf[slot]) + _dot(p_own, vn_ref[0, 0])
    o_ref[0, 0] = o / den


def _moba_sample(mq, mk, mv, pool_k, pool_v, page_table, n_seq, n_tok):
    n_pages = page_table.shape[1]
    kbar = _moba_kbar(pool_k, page_table)
    sel = _moba_select(mq, kbar, n_seq, n_tok)

    def head_major(a):
        return a.reshape(n_seq, n_tok, MOBA_HEADS, MOBA_HD).transpose(0, 2, 1, 3)

    qh, kh, vh = head_major(mq), head_major(mk), head_major(mv)
    n_sel = n_tok * MOBA_TOPK
    spec = pl.BlockSpec((1, 1, n_tok, MOBA_HD), lambda s, sel, pt: (s // MOBA_HEADS, s % MOBA_HEADS, 0, 0))
    anyspec = pl.BlockSpec(memory_space=pl.ANY)
    o = pl.pallas_call(
        functools.partial(_moba_sample_kernel, n_tok=n_tok, n_pages=n_pages),
        grid_spec=pltpu.PrefetchScalarGridSpec(
            num_scalar_prefetch=2,
            grid=(n_seq * MOBA_HEADS,),
            in_specs=[spec, spec, spec, anyspec, anyspec],
            out_specs=spec,
            scratch_shapes=[pltpu.VMEM((2, MOBA_HD, n_sel * MOBA_BLOCK), F32),
                            pltpu.VMEM((2, MOBA_HD, n_sel * MOBA_BLOCK), F32),
                            pltpu.SemaphoreType.DMA((2, 2))],
        ),
        out_shape=jax.ShapeDtypeStruct(qh.shape, F32),
        compiler_params=_cparams("arbitrary"),
        name="moba_sample",
    )(sel, page_table.reshape(-1), qh, kh, vh, pool_k, pool_v)
    return o.transpose(0, 2, 1, 3).reshape(n_seq * n_tok, MOBA_HEADS * MOBA_HD)


def _outproj_kernel(*refs, n_in):
    x_ref, a_refs, w_refs, o_ref = refs[0], refs[1:1 + n_in], refs[1 + n_in:1 + 2 * n_in], refs[-1]
    acc = x_ref[...]
    for a_ref, w_ref in zip(a_refs, w_refs):
        acc = acc + jnp.dot(a_ref[...].astype(BF16), w_ref[...], preferred_element_type=F32)
    o_ref[...] = acc


def _outproj(x, acts, ws, tm):
    n = x.shape[0]
    row = lambda i: (i, 0)
    const = lambda i: (0, 0)
    return pl.pallas_call(
        functools.partial(_outproj_kernel, n_in=len(acts)),
        grid=(n // tm,),
        in_specs=[pl.BlockSpec((tm, D_MODEL), row)] + [pl.BlockSpec((tm, a.shape[1]), row) for a in acts]
                 + [pl.BlockSpec(w.shape, const) for w in ws],
        out_specs=pl.BlockSpec((tm, D_MODEL), row),
        out_shape=jax.ShapeDtypeStruct((n, D_MODEL), F32),
        compiler_params=_cparams("parallel"),
        name="outproj",
    )(x, *acts, *ws)


ROUTE_ROWS = 32


def _router_kernel(h_ref, g_ref, wr_ref, before_ref, route_ref, cnt_ref, cnt_scr):
    i = pl.program_id(0)

    @pl.when(i == 0)
    def _():
        cnt_scr[...] = jnp.zeros(cnt_scr.shape, F32)

    xn = _rmsnorm(h_ref[...], g_ref[...])
    logits = _dot(xn, wr_ref[...]).T[:ROUTE_ROWS, :]
    tm = logits.shape[1]
    lane = _row_iota((ROUTE_ROWS, tm)).astype(F32)
    big = float(LANES)

    def first_argmax(v):
        mx = jnp.max(v, axis=0, keepdims=True)
        return mx, jnp.min(jnp.where(v == mx, lane, big), axis=0, keepdims=True)

    is_group = lane < float(MOE_GROUPS)
    lg = jnp.where(is_group, logits, _NEG_INF)
    mg, gi = first_argmax(lg)
    wg = 1.0 / jnp.sum(jnp.where(is_group, jnp.exp(lg - mg), 0.0), axis=0, keepdims=True)
    e0 = float(MOE_GROUPS) + float(MOE_EPG) * gi
    in_group = (lane >= e0) & (lane < e0 + float(MOE_EPG))
    le = jnp.where(in_group, logits, _NEG_INF)
    m1, i1 = first_argmax(le)
    m2, i2 = first_argmax(jnp.where(lane == i1, _NEG_INF, le))
    r = jnp.exp(m2 - m1)
    g1 = wg / (1.0 + r)
    g2 = wg * r / (1.0 + r)
    a = jnp.minimum(i1, i2) - e0
    b = jnp.maximum(i1, i2) - e0
    ga = jnp.where(i1 < i2, g1, g2)
    gb = jnp.where(i1 < i2, g2, g1)
    pair = a * (7.0 - a) * 0.5 + (b - a - 1.0)
    cls = gi * 6.0 + pair
    onehot = jnp.where(lane == cls, 1.0, 0.0)
    before = jnp.dot(onehot.astype(BF16), before_ref[...], preferred_element_type=F32) + cnt_scr[:, 0:1]
    rank = jnp.sum(onehot * before, axis=0, keepdims=True)
    cnt_scr[...] = cnt_scr[...] + jnp.sum(onehot, axis=1, keepdims=True)
    out_row = _row_iota(route_ref.shape)
    route_ref[...] = jnp.where(out_row == 0, cls, jnp.where(out_row == 1, ga, jnp.where(out_row == 2, gb,
                               jnp.where(out_row == 3, rank, 0.0))))
    cnt_ref[...] = cnt_scr[...]


def _router(h, g, wr, tm):
    n = h.shape[0]
    earlier = (jnp.arange(tm)[:, None] < jnp.arange(tm)[None, :]).astype(BF16)
    row = lambda i: (i, 0)
    const = lambda i: (0, 0)
    return pl.pallas_call(
        _router_kernel,
        grid=(n // tm,),
        in_specs=[pl.BlockSpec((tm, D_MODEL), row), pl.BlockSpec((1, D_MODEL), const),
                  pl.BlockSpec((D_MODEL, LANES), const), pl.BlockSpec((tm, tm), const)],
        out_specs=[pl.BlockSpec((8, tm), lambda i: (0, i)), pl.BlockSpec((ROUTE_ROWS, LANES), const)],
        out_shape=[jax.ShapeDtypeStruct((8, n), F32), jax.ShapeDtypeStruct((ROUTE_ROWS, LANES), F32)],
        scratch_shapes=[pltpu.VMEM((ROUTE_ROWS, LANES), F32)],
        compiler_params=_cparams("arbitrary"),
        name="moe_router",
    )(h, g, wr, earlier)


def _ffn_kernel(ea_ref, eb_ref, nv_ref, tok_ref, gates_ref, g_ref, wga_ref, wua_ref, wda_ref, wgb_ref, wub_ref, wdb_ref,
                h_hbm, out_hbm, xbuf, ybuf, row_sized, gsem, ssem):
    i = pl.program_id(0)
    n_steps = pl.num_programs(0)
    slot = i % 2
    rows = xbuf.shape[1]

    def gather_tile(tile, buf_slot):
        for r in range(rows):
            tok = tok_ref[tile * rows + r]
            pltpu.make_async_copy(h_hbm.at[pl.ds(tok, 1), :], xbuf.at[buf_slot, pl.ds(r, 1), :], gsem.at[buf_slot]).start()

    def wait_gather(buf_slot):
        pltpu.make_async_copy(h_hbm.at[pl.ds(0, rows), :], xbuf.at[buf_slot], gsem.at[buf_slot]).wait()

    def scatter_tile(tile, buf_slot):
        def start_row(r):
            tok = tok_ref[tile * rows + r]
            pltpu.make_async_copy(ybuf.at[buf_slot, pl.ds(r, 1), :], out_hbm.at[pl.ds(tok, 1), :], ssem.at[buf_slot]).start()

        @pl.when(nv_ref[tile] == rows)
        def _():
            for r in range(rows):
                start_row(r)

        @pl.when(nv_ref[tile] < rows)
        def _():
            def body(r, c):
                start_row(r)
                return c
            lax.fori_loop(0, nv_ref[tile], body, 0)

    def wait_scatter(tile, buf_slot):
        n = nv_ref[tile]

        @pl.when(n > 0)
        def _():
            pltpu.make_async_copy(row_sized.at[pl.ds(0, n)], row_sized.at[pl.ds(0, n)], ssem.at[buf_slot]).wait()

    @pl.when(i == 0)
    def _():
        gather_tile(0, 0)

    @pl.when((i == 0) | (nv_ref[jnp.maximum(i - 1, 0)] > 0))
    def _():
        wait_gather(slot)

    @pl.when(i >= 2)
    def _():
        wait_scatter(i - 2, slot)

    @pl.when(nv_ref[i] > 0)
    def _():
        x = xbuf[slot]
        xb = _rmsnorm(x, g_ref[...]).astype(BF16)
        y = x
        for lane, wg_ref, wu_ref, wd_ref in ((0, wga_ref, wua_ref, wda_ref), (1, wgb_ref, wub_ref, wdb_ref)):
            gate = gates_ref[:, lane:lane + 1]
            hid = jax.nn.silu(_dot(xb, wg_ref[0, 0])) * _dot(xb, wu_ref[0, 0])
            y = y + _dot(hid * gate, wd_ref[0, 0])
        ybuf[slot] = y
        gather_tile(i + 1, 1 - slot)
        scatter_tile(i, slot)

    @pl.when(i == n_steps - 1)
    def _():
        wait_scatter(i - 1, 1 - slot)


def _ffn(ea, eb, nvalid, tok, gates, h, g, wg, wu, wd, layer, rows):
    n_tiles = tok.shape[0] // rows
    tile = lambda i, *_: (i, 0)
    const = lambda i, *_: (0, 0)
    up = lambda sel: pl.BlockSpec((1, 1, D_MODEL, EXPERT_FF), sel)
    down = lambda sel: pl.BlockSpec((1, 1, EXPERT_FF, D_MODEL), sel)
    sel_a = lambda i, ea, eb, nv, tok: (layer, ea[i], 0, 0)
    sel_b = lambda i, ea, eb, nv, tok: (layer, eb[i], 0, 0)
    anyspec = pl.BlockSpec(memory_space=pl.ANY)
    return pl.pallas_call(
        _ffn_kernel,
        grid_spec=pltpu.PrefetchScalarGridSpec(
            num_scalar_prefetch=4,
            grid=(n_tiles,),
            in_specs=[pl.BlockSpec((rows, 2), tile), pl.BlockSpec((1, D_MODEL), const),
                      up(sel_a), up(sel_a), down(sel_a), up(sel_b), up(sel_b), down(sel_b), anyspec],
            out_specs=anyspec,
            scratch_shapes=[pltpu.VMEM((2, rows, D_MODEL), F32), pltpu.VMEM((2, rows, D_MODEL), F32),
                            pltpu.VMEM((rows, D_MODEL // LANES, LANES), F32),
                            pltpu.SemaphoreType.DMA((2,)), pltpu.SemaphoreType.DMA((2,))],
        ),
        out_shape=jax.ShapeDtypeStruct(h.shape, F32),
        compiler_params=_cparams("arbitrary"),
        name="moe_ffn",
    )(ea, eb, nvalid, tok, gates, g, wg, wu, wd, wg, wu, wd, h)


_PAIR_A = (0, 0, 0, 1, 1, 2)
_PAIR_B = (1, 2, 3, 2, 3, 3)


def _hier_moe(h, g, wr, wg, wu, wd, layer):
    n = h.shape[0]
    tile = MOE_TILE if n >= N_CLASSES * MOE_TILE else MOE_TILE_SMALL
    route, cnt = _router(h, g, wr, TOKEN_TILE)
    cls = route[0].astype(I32)
    rank = route[3].astype(I32)
    counts = cnt[:N_CLASSES, 0].astype(I32)
    tiles_c = (counts + tile - 1) // tile
    tile_end = jnp.cumsum(tiles_c)
    tile_start = tile_end - tiles_c
    pos = tile_start[cls] * tile + rank
    n_tiles = n // tile + N_CLASSES
    ti = jnp.arange(n_tiles, dtype=I32)
    total = tile_end[-1]
    tcls = jnp.searchsorted(tile_end, jnp.minimum(ti, total - 1), side="right").astype(I32)
    tcls = jnp.minimum(tcls, N_CLASSES - 1)
    nvalid = jnp.where(ti < total, jnp.clip(counts[tcls] - (ti - tile_start[tcls]) * tile, 0, tile), 0).astype(I32)
    grp, pair = tcls // 6, tcls % 6
    ea = grp * MOE_EPG + jnp.asarray(_PAIR_A, I32)[pair]
    eb = grp * MOE_EPG + jnp.asarray(_PAIR_B, I32)[pair]
    per_token = jnp.stack([jnp.arange(n, dtype=I32), lax.bitcast_convert_type(route[1], I32),
                           lax.bitcast_convert_type(route[2], I32)], axis=1)
    per_slot = jnp.zeros((n_tiles * tile, 3), I32).at[pos].set(per_token)
    tok = per_slot[:, 0]
    gates = lax.bitcast_convert_type(per_slot[:, 1:3], F32)
    return _ffn(ea, eb, nvalid, tok, gates, h, g, wg, wu, wd, layer, tile)


def _odd_inproj_kernel(x_ref, g_ref, wz_ref, wx_ref, wdt_ref, z_ref, xbc_ref, dt_ref):
    xb = _rmsnorm(x_ref[...], g_ref[...]).astype(BF16)
    z_ref[...] = jnp.dot(xb, wz_ref[...], preferred_element_type=F32)
    xbc_ref[...] = jnp.dot(xb, wx_ref[...], preferred_element_type=F32)
    dt_ref[...] = jnp.dot(xb, wdt_ref[...], preferred_element_type=F32)


def _odd_inproj(x, g, wz, wx, wdt, tm):
    n = x.shape[0]
    row = lambda i: (i, 0)
    const = lambda i: (0, 0)
    widths = (D_INNER, CONV_DIM, LANES)
    return pl.pallas_call(
        _odd_inproj_kernel,
        grid=(n // tm,),
        in_specs=[pl.BlockSpec((tm, D_MODEL), row), pl.BlockSpec((1, D_MODEL), const),
                  pl.BlockSpec(wz.shape, const), pl.BlockSpec(wx.shape, const), pl.BlockSpec(wdt.shape, const)],
        out_specs=[pl.BlockSpec((tm, c), row) for c in widths],
        out_shape=[jax.ShapeDtypeStruct((n, c), F32) for c in widths],
        compiler_params=_cparams("parallel"),
        name="odd_inproj",
    )(x, g, wz, wx, wdt)


CONV_COLS = 512


def _odd_inproj_conv_kernel(x_ref, g_ref, wz_ref, wx_ref, wdt_ref, cs_ref, cw_ref, cb_ref, dtb_ref,
                            z_ref, xc_ref, dts_ref, cso_ref, raw, *, tiles_per_seq):
    i = pl.program_id(0)
    tm = x_ref.shape[0]
    tail = CONV_W - 1
    xb = _rmsnorm(x_ref[...], g_ref[...]).astype(BF16)
    z_ref[...] = jnp.dot(xb, wz_ref[...], preferred_element_type=F32)
    dts_ref[...] = jax.nn.softplus(jnp.dot(xb, wdt_ref[...], preferred_element_type=F32) + dtb_ref[...])

    @pl.when(i % tiles_per_seq == 0)
    def _():
        raw[0:8, :] = jnp.zeros((8, CONV_DIM), F32)
        raw[8 - tail:8, :] = cs_ref[0]

    raw[8:, :] = jnp.dot(xb, wx_ref[...], preferred_element_type=F32)
    for j in range(CONV_DIM // CONV_COLS):
        cols = slice(j * CONV_COLS, (j + 1) * CONV_COLS)
        acc = cb_ref[:, cols] + raw[8:8 + tm, cols] * cw_ref[tail:tail + 1, cols]
        for k in range(tail):
            acc = acc + raw[8 - tail + k:8 - tail + k + tm, cols] * cw_ref[k:k + 1, cols]
        xc_ref[:, cols] = jax.nn.silu(acc)
    new_tail = raw[8 + tm - tail:8 + tm, :]
    cso_ref[0] = new_tail
    raw[8 - tail:8, :] = new_tail


def _odd_inproj_conv(x, g, wz, wx, wdt, conv_state, cw, cb, dtb, tm, n_tok):
    n = x.shape[0]
    tiles_per_seq = n_tok // tm
    row = lambda i: (i, 0)
    const = lambda i: (0, 0)
    st = lambda i: (i // tiles_per_seq, 0, 0)
    widths = (D_INNER, CONV_DIM, LANES)
    return pl.pallas_call(
        functools.partial(_odd_inproj_conv_kernel, tiles_per_seq=tiles_per_seq),
        grid=(n // tm,),
        in_specs=[pl.BlockSpec((tm, D_MODEL), row), pl.BlockSpec((1, D_MODEL), const),
                  pl.BlockSpec(wz.shape, const), pl.BlockSpec(wx.shape, const), pl.BlockSpec(wdt.shape, const),
                  pl.BlockSpec((1, CONV_W - 1, CONV_DIM), st),
                  pl.BlockSpec((CONV_W, CONV_DIM), const), pl.BlockSpec((1, CONV_DIM), const), pl.BlockSpec((1, LANES), const)],
        out_specs=[pl.BlockSpec((tm, c), row) for c in widths] + [pl.BlockSpec((1, CONV_W - 1, CONV_DIM), st)],
        out_shape=[jax.ShapeDtypeStruct((n, c), F32) for c in widths] + [jax.ShapeDtypeStruct(conv_state.shape, F32)],
        scratch_shapes=[pltpu.VMEM((8 + tm, CONV_DIM), F32)],
        compiler_params=_cparams("arbitrary"),
        name="odd_inproj_conv",
    )(x, g, wz, wx, wdt, conv_state, cw, cb, dtb)


def _conv_kernel(xbc_ref, dt_ref, cs_ref, w_ref, b_ref, dtb_ref, xc_ref, dts_ref, cso_ref, carry):
    c = pl.program_id(1)
    tt = xbc_ref.shape[0]
    tail = CONV_W - 1

    @pl.when(c == 0)
    def _():
        carry[...] = jnp.zeros(carry.shape, F32)
        carry[8 - tail:8, :] = cs_ref[0]

    for j in range(CONV_DIM // CONV_COLS):
        cols = slice(j * CONV_COLS, (j + 1) * CONV_COLS)
        x = xbc_ref[:, cols]
        full = jnp.concatenate([carry[:, cols], x], axis=0)
        acc = b_ref[:, cols] + x * w_ref[tail:tail + 1, cols]
        for k in range(tail):
            acc = acc + full[8 - tail + k:8 - tail + k + tt, :] * w_ref[k:k + 1, cols]
        xc_ref[:, cols] = jax.nn.silu(acc)
        new_tail = full[8 + tt - tail:8 + tt, :]
        carry[8 - tail:8, cols] = new_tail
        cso_ref[0, :, cols] = new_tail
    dts_ref[...] = jax.nn.softplus(dt_ref[...] + dtb_ref[...])


def _conv(xbc, dt, conv_state, w, b, dtb, n_batch, n_tok, tt):
    n_chunks = n_tok // tt
    blk = lambda i, c: (i * n_chunks + c, 0)
    const = lambda i, c: (0, 0)
    st = lambda i, c: (i, 0, 0)
    return pl.pallas_call(
        _conv_kernel,
        grid=(n_batch, n_chunks),
        in_specs=[pl.BlockSpec((tt, CONV_DIM), blk), pl.BlockSpec((tt, LANES), blk),
                  pl.BlockSpec((1, CONV_W - 1, CONV_DIM), st),
                  pl.BlockSpec((CONV_W, CONV_DIM), const), pl.BlockSpec((1, CONV_DIM), const), pl.BlockSpec((1, LANES), const)],
        out_specs=[pl.BlockSpec((tt, CONV_DIM), blk), pl.BlockSpec((tt, LANES), blk),
                   pl.BlockSpec((1, CONV_W - 1, CONV_DIM), st)],
        out_shape=[jax.ShapeDtypeStruct(xbc.shape, F32), jax.ShapeDtypeStruct(dt.shape, F32),
                   jax.ShapeDtypeStruct(conv_state.shape, F32)],
        scratch_shapes=[pltpu.VMEM((8, CONV_DIM), F32)],
        compiler_params=_cparams("parallel", "arbitrary"),
        name="ssd_conv",
    )(xbc, dt, conv_state, w, b, dtb)


def _ssd_kernel(xs_ref, bm_ref, cm_ref, z_ref, dt_ref, s0_ref, a_ref, dsk_ref, ng_ref, tril_ref, mask_ref, elast_ref,
                o_ref, sout_ref, s_scr, yz_scr, *, n_seq, seq_len, gps):
    c = pl.program_id(2)

    @pl.when(c == 0)
    def _():
        s_scr[...] = s0_ref[...]

    lane = _lane_iota((ROWS, LANES))
    row = _row_iota((ROWS, LANES))
    lo_half = lane < (LANES // 2)
    dt = dt_ref[...]
    cs = _dot01(tril_ref[...], dt * a_ref[...])
    cs_last = _dot01(elast_ref[...], cs)
    if gps != SSM_GROUPS:
        shift = (LANES - pl.program_id(1) * (gps * SSM_HEADS // SSM_GROUPS)) % LANES
        dt, cs, cs_last = (pltpu.roll(v, shift, 1) for v in (dt, cs, cs_last))
    cs_t = cs.T
    ecs = jnp.exp(cs)
    dt_tail = dt * jnp.exp(cs_last - cs)
    causal = mask_ref[...] > 0.0

    def per_head(v, h0):
        return jnp.where(lo_half, v[:, h0:h0 + 1], v[:, h0 + 1:h0 + 2])

    for g in range(gps):
        gcols = slice(g * D_STATE, (g + 1) * D_STATE)
        cg = cm_ref[:, gcols].astype(BF16)
        bg = bm_ref[:, gcols].astype(BF16)
        cb = _dot_nt(cg, bg)
        for pp in range(2):
            pair = 2 * g + pp
            h0 = 2 * pair
            pcols = slice(pair * LANES, (pair + 1) * LANES)
            x2 = xs_ref[:, pcols]
            xdt = x2 * per_head(dt, h0)
            ys = []
            for e in range(2):
                h = h0 + e
                seg = cs[:, h:h + 1] - cs_t[h:h + 1, :]
                wmat = jnp.exp(jnp.where(causal, seg, _NEG_INF)) * cb
                ys.append(_dot(wmat, xdt))
            y2 = jnp.where(lo_half, ys[0], ys[1])
            if n_seq == 1:
                inter = _dot_nt(cg, s_scr[0, pair])
            else:
                inter = jnp.zeros((ROWS, LANES), F32)
                for s in range(n_seq):
                    in_seq = (row >= s * seq_len) & (row < (s + 1) * seq_len)
                    inter = inter + jnp.where(in_seq, _dot_nt(cg, s_scr[s, pair]), 0.0)
            y2 = y2 + inter * per_head(ecs, h0) + dsk_ref[:, pcols] * x2
            yz_scr[:, pcols] = y2 * jax.nn.silu(z_ref[:, pcols])
            xw_t = (x2 * per_head(dt_tail, h0)).T
            for s in range(n_seq):
                last = (s + 1) * seq_len - 1
                if n_seq == 1:
                    xt = xw_t
                else:
                    xt = jnp.where((lane >= s * seq_len) & (lane <= last), xw_t, 0.0)
                dec = jnp.exp(jnp.where(row < (LANES // 2), cs_t[h0:h0 + 1, last:last + 1], cs_t[h0 + 1:h0 + 2, last:last + 1]))
                s_scr[s, pair] = s_scr[s, pair] * dec + _dot(xt, bg)
    gw = D_INNER // SSM_GROUPS
    for g in range(gps):
        cols = slice(g * gw, (g + 1) * gw)
        o_ref[:, cols] = _rmsnorm(yz_scr[:, cols], ng_ref[:, cols])

    @pl.when(c == pl.num_programs(2) - 1)
    def _():
        sout_ref[...] = s_scr[...]


def _ssd(xc, z, dts, s0, a_row, dsk_row, ng_row, n_batch, n_tok):
    seq_len = min(ROWS, n_tok)
    n_seq = ROWS // seq_len
    n_chunks = n_tok // seq_len
    r = jnp.arange(ROWS)
    li, sid = r % seq_len, r // seq_len
    mask = ((sid[:, None] == sid[None, :]) & (li[:, None] >= li[None, :]))
    elast = (r[None, :] == (sid * seq_len + seq_len - 1)[:, None])
    n_pairs = SSM_HEADS // 2
    s0p = s0.reshape(n_batch, n_pairs, LANES, D_STATE)
    gps = SSM_GROUPS if n_seq == 1 else 1
    xw = gps * (D_INNER // SSM_GROUPS)
    bw = gps * D_STATE
    rows = lambda i, j, c: i * n_chunks + c
    xblk = lambda i, j, c: (rows(i, j, c), j)
    bblk = lambda i, j, c: (rows(i, j, c), D_INNER // bw + j)
    cblk = lambda i, j, c: (rows(i, j, c), (D_INNER + SSM_GROUPS * D_STATE) // bw + j)
    dtblk = lambda i, j, c: (rows(i, j, c), 0)
    st = lambda i, j, c: (i, j, 0, 0)
    const = lambda i, j, c: (0, 0)
    gconst = lambda i, j, c: (0, j)
    state_block = (n_seq, 2 * gps, LANES, D_STATE)
    o, s_out = pl.pallas_call(
        functools.partial(_ssd_kernel, n_seq=n_seq, seq_len=seq_len, gps=gps),
        grid=(n_batch // n_seq, SSM_GROUPS // gps, n_chunks),
        in_specs=[pl.BlockSpec((ROWS, xw), xblk), pl.BlockSpec((ROWS, bw), bblk), pl.BlockSpec((ROWS, bw), cblk),
                  pl.BlockSpec((ROWS, xw), xblk), pl.BlockSpec((ROWS, LANES), dtblk),
                  pl.BlockSpec(state_block, st),
                  pl.BlockSpec((1, LANES), const), pl.BlockSpec((1, xw), gconst), pl.BlockSpec((1, xw), gconst),
                  pl.BlockSpec((ROWS, ROWS), const), pl.BlockSpec((ROWS, ROWS), const), pl.BlockSpec((ROWS, ROWS), const)],
        out_specs=[pl.BlockSpec((ROWS, xw), xblk), pl.BlockSpec(state_block, st)],
        out_shape=[jax.ShapeDtypeStruct((n_batch * n_tok, D_INNER), F32), jax.ShapeDtypeStruct(s0p.shape, F32)],
        scratch_shapes=[pltpu.VMEM(state_block, F32), pltpu.VMEM((ROWS, xw), F32)],
        compiler_params=_cparams("parallel", "parallel", "arbitrary"),
        name="ssd_scan",
    )(xc, xc, xc, z, dts, s0p, a_row, dsk_row, ng_row, mask.astype(BF16), mask.astype(F32), elast.astype(BF16))
    return o, s_out.reshape(s0.shape)


def _rope_tables(pos):
    half = MOBA_HD // 2
    inv = jnp.power(ROPE_THETA, -jnp.arange(half, dtype=F32) * 2.0 / MOBA_HD)
    ang = pos[:, None] * inv[None, :]
    cos, sin = jnp.cos(ang), jnp.sin(ang)
    reps = LANES // MOBA_HD
    return jnp.tile(jnp.concatenate([cos, cos], axis=1), (1, reps)), jnp.tile(jnp.concatenate([-sin, sin], axis=1), (1, reps))


def _pad_cols(w, width):
    return jnp.pad(w, ((0, 0), (0, width - w.shape[1])))


TOKEN_TILE = 256


def _even_layer(h, nb, nt, pos, ret0, cache, norm_g, w_in, w_out, q_g, k_g):
    tm = TOKEN_TILE
    cos, sin = _rope_tables(pos)
    if nt < tm:
        cos, sin = jnp.tile(cos, (tm // nt, 1)), jnp.tile(sin, (tm // nt, 1))
    head_tile = lambda v: jnp.tile(v, MOBA_HEADS)[None, :]
    width = MOBA_HEADS * MOBA_HD
    blockdiag = (jnp.arange(width)[:, None] // MOBA_HD == jnp.arange(width)[None, :] // MOBA_HD).astype(BF16)
    outs = _even_inproj(h, norm_g[None, :], w_in.astype(BF16), cos, sin, head_tile(q_g), head_tile(k_g), blockdiag,
                        tm, nt, key_minor=cache is None)
    rq, rk, rv, rg = outs[:4]
    if cache is None:
        qt, kb, kt, vt, vtb, kbar = outs[4:]
        mo = _moba_prompt(qt, kb, vtb, kbar.reshape(-1, width), nb, nt)
        mk, mv = (a.reshape(nb, MOBA_HEADS, MOBA_HD, nt).transpose(0, 3, 1, 2) for a in (kt, vt))
    else:
        mq, mk, mv = outs[4:]
        mo = _moba_sample(mq, mk, mv, *cache, nb, nt)
        mk, mv = (a.reshape(nb, nt, MOBA_HEADS, MOBA_HD) for a in (mk, mv))
    ro, ret_state = _retention(rq, rk, rv, rg, ret0, nb, nt)
    w_out = w_out.astype(BF16)
    h = _outproj(h, (ro, mo), (w_out[:RET_HEADS * RET_DV], w_out[RET_HEADS * RET_DV:]), tm)
    return h, ret_state, mk, mv


def _odd_layer(h, nb, nt, conv0, ssm0, norm_g, w_in, conv_w, conv_b, dt_bias, a_log, d_skip, ssd_norm, w_out):
    tm = TOKEN_TILE
    w_in = w_in.astype(BF16)
    weights = (w_in[:, :D_INNER], w_in[:, D_INNER:D_INNER + CONV_DIM], _pad_cols(w_in[:, D_INNER + CONV_DIM:], LANES))
    conv_params = (conv_w, conv_b[None, :], _pad_cols(dt_bias[None, :], LANES))
    if nt % tm == 0:
        z, xc, dts, conv_state = _odd_inproj_conv(h, norm_g[None, :], *weights, conv0, *conv_params, tm, nt)
    else:
        z, xbc, dt = _odd_inproj(h, norm_g[None, :], *weights, tm)
        xc, dts, conv_state = _conv(xbc, dt, conv0, *conv_params, nb, nt, nt)
    a_row = _pad_cols(-jnp.exp(a_log)[None, :], LANES)
    dsk_row = jnp.repeat(d_skip, D_INNER // SSM_HEADS)[None, :]
    yzn, ssm_state = _ssd(xc, z, dts, ssm0, a_row, dsk_row, ssd_norm[None, :], nb, nt)
    h = _outproj(h, (yzn,), (w_out.astype(BF16),), tm)
    return h, conv_state, ssm_state


def _moe_layer(h, norm_g, w_rg, w_re, w_g, w_u, w_d, layer):
    wr = _pad_cols(jnp.concatenate([w_rg, w_re], axis=1), LANES)
    return _hier_moe(h, norm_g[None, :], wr, w_g, w_u, w_d, layer)


def kernel(x_prompt, x_sample, state_ret, cache_k, cache_v, page_table, state_ssm, state_conv, norm_mix, norm_ffn,
           w_in_even, w_out_even, q_norm, k_norm, w_in_odd, conv_w, conv_b, dt_bias, a_log, d_skip, ssd_norm, w_out_odd,
           w_router_group, w_router_expert, w_expert_gate, w_expert_up, w_expert_down):
    bp, seq, _ = x_prompt.shape
    bs, dec_seq, _ = x_sample.shape
    past_len = page_table.shape[1] * PAGE_SIZE
    groups = ((x_prompt.reshape(bp * seq, D_MODEL), bp, seq, jnp.arange(seq, dtype=F32)),
              (x_sample.reshape(bs * dec_seq, D_MODEL), bs, dec_seq, past_len + jnp.arange(dec_seq, dtype=F32)))

    def moe(h, layer):
        return _moe_layer(h, norm_ffn[layer], w_router_group[layer], w_router_expert[layer],
                          w_expert_gate, w_expert_up, w_expert_down, layer)

    outs = []
    for gi, (h, nb, nt, pos) in enumerate(groups):
        if gi == 0:
            ret0 = jnp.zeros((nb, RET_HEADS, RET_DK, RET_DV), F32)
            cache = None
            conv0 = jnp.zeros((nb, CONV_W - 1, CONV_DIM), F32)
            ssm0 = jnp.zeros((nb, SSM_HEADS, D_INNER // SSM_HEADS, D_STATE), F32)
        else:
            ret0 = state_ret[0]
            cache = (cache_k[0].transpose(0, 2, 3, 1), cache_v[0].transpose(0, 2, 3, 1), page_table)
            conv0, ssm0 = state_conv[0], state_ssm[0]
        h, ret_state, mk, mv = _even_layer(h, nb, nt, pos, ret0, cache, norm_mix[0], w_in_even[0], w_out_even[0],
                                           q_norm[0], k_norm[0])
        h = moe(h, 0)
        h, conv_state, ssm_state = _odd_layer(h, nb, nt, conv0, ssm0, norm_mix[1], w_in_odd[0], conv_w[0], conv_b[0],
                                              dt_bias[0], a_log[0], d_skip[0], ssd_norm[0], w_out_odd[0])
        h = moe(h, 1)
        outs.append(dict(h=h.reshape(nb, nt, D_MODEL), ret=ret_state[None], k=mk[None], v=mv[None],
                         ssm=ssm_state[None], conv=conv_state[None]))
    p, s = outs
    return (p["h"], s["h"], p["ret"], s["ret"], p["k"], p["v"], s["k"], s["v"], p["ssm"], s["ssm"], p["conv"], s["conv"])
```

```python
import functools
import math

import jax
import jax.numpy as jnp
from jax import lax
from jax.experimental import pallas as pl
from jax.experimental.pallas import tpu as pltpu

F32 = jnp.float32
BF16 = jnp.bfloat16
I32 = jnp.int32

D_MODEL = 1024
PAGE_SIZE = 128
RET_HEADS = 8
RET_DK = 64
RET_DV = 128
MOBA_HEADS = 8
MOBA_HD = 64
MOBA_BLOCK = 256
MOBA_TOPK = 3
ROPE_THETA = 10000.0
D_INNER = 2048
SSM_HEADS = 32
SSM_GROUPS = 8
D_STATE = 128
CONV_W = 4
CONV_DIM = 4096
MOE_GROUPS = 4
MOE_EPG = 4
N_EXPERTS = 16
EXPERT_FF = 512
NORM_EPS = 1e-6
N_CLASSES = MOE_GROUPS * 6
LANES = 128
ROWS = 128
MOE_TILE = 256
MOE_TILE_SMALL = 32
VMEM_LIMIT = 56 * 1024 * 1024

_NEG_INF = float("-inf")


def _cparams(*sem):
    return pltpu.CompilerParams(dimension_semantics=sem, vmem_limit_bytes=VMEM_LIMIT)


def _dot(a, b):
    return jnp.dot(a.astype(BF16), b.astype(BF16), preferred_element_type=F32)


def _dot_nt(a, b):
    return lax.dot_general(a.astype(BF16), b.astype(BF16), (((1,), (1,)), ((), ())), preferred_element_type=F32)


def _split3(x):
    hi = x.astype(BF16)
    r = x - hi.astype(F32)
    mid = r.astype(BF16)
    lo = (r - mid.astype(F32)).astype(BF16)
    return hi, mid, lo


def _dot01(m01, x):
    hi, mid, lo = _split3(x)
    return (jnp.dot(m01, hi, preferred_element_type=F32) + jnp.dot(m01, mid, preferred_element_type=F32)
            + jnp.dot(m01, lo, preferred_element_type=F32))


def _rmsnorm(x, g=None):
    y = x * lax.rsqrt(jnp.mean(x * x, axis=-1, keepdims=True) + NORM_EPS)
    return y if g is None else y * g


def _lane_iota(shape):
    return lax.broadcasted_iota(I32, shape, len(shape) - 1)


def _row_iota(shape):
    return lax.broadcasted_iota(I32, shape, len(shape) - 2)


def _even_inproj_kernel(x_ref, g_ref, w_ref, cos_ref, sin_ref, qg_ref, kg_ref, bd_ref,
                        rq_ref, rk_ref, rv_ref, rg_ref, *moba_refs, key_minor):
    xb = _rmsnorm(x_ref[...], g_ref[...]).astype(BF16)
    cos = cos_ref[...]
    sin = sin_ref[...]
    first_half = (_lane_iota(cos.shape) % MOBA_HD) < (MOBA_HD // 2)

    def proj(c0, c1):
        return jnp.dot(xb, w_ref[:, c0:c1], preferred_element_type=F32)

    def rope(a):
        outs = []
        for j in range(a.shape[1] // LANES):
            s = a[:, j * LANES:(j + 1) * LANES]
            rot = jnp.where(first_half, pltpu.roll(s, LANES - MOBA_HD // 2, 1), pltpu.roll(s, MOBA_HD // 2, 1))
            outs.append(s * cos + rot * sin)
        return jnp.concatenate(outs, axis=1)

    def qk_norm(a, gam):
        ss = jnp.dot((a * a).astype(BF16), bd_ref[...], preferred_element_type=F32)
        return a * lax.rsqrt(ss * (1.0 / MOBA_HD) + NORM_EPS) * gam

    rq_ref[...] = rope(proj(0, 512))
    rk_ref[...] = rope(proj(512, 1024)) * (RET_DK ** -0.5)
    rv_ref[...] = proj(1024, 2048)
    rg_ref[...] = proj(2048, 3072)
    mq = rope(qk_norm(proj(3072, 3584), qg_ref[...]))
    mk = rope(qk_norm(proj(3584, 4096), kg_ref[...]))
    mv = proj(4096, 4608)
    if not key_minor:
        mq_ref, mk_ref, mv_ref = moba_refs
        mq_ref[...] = mq
        mk_ref[...] = mk
        mv_ref[...] = mv
    else:
        qt_ref, kb_ref, kt_ref, vt_ref, vtb_ref, kbar_ref = moba_refs
        qt_ref[0] = mq.T
        kb_ref[...] = mk.astype(BF16)
        kt_ref[0] = mk.T
        vt = mv.T
        vt_ref[0] = vt
        vtb_ref[0] = vt.astype(BF16)
        kbar_ref[0] = jnp.mean(mk, axis=0, keepdims=True)


def _even_inproj(x, g, w, cos, sin, qg, kg, bd, tm, n_tok, key_minor):
    n = x.shape[0]
    n_pos_blocks = cos.shape[0] // tm
    row = lambda i: (i, 0)
    const = lambda i: (0, 0)
    width = MOBA_HEADS * MOBA_HD
    out_specs = [pl.BlockSpec((tm, c), row) for c in (512, 512, 1024, 1024)]
    out_shape = [jax.ShapeDtypeStruct((n, c), F32) for c in (512, 512, 1024, 1024)]
    if not key_minor:
        out_specs += [pl.BlockSpec((tm, width), row)] * 3
        out_shape += [jax.ShapeDtypeStruct((n, width), F32)] * 3
    else:
        assert tm == MOBA_BLOCK and n_tok % tm == 0
        tiles = n_tok // tm
        tspec = pl.BlockSpec((1, width, tm), lambda i: (i // tiles, 0, i % tiles))
        tshape = lambda dt: jax.ShapeDtypeStruct((n // n_tok, width, n_tok), dt)
        out_specs += [tspec, pl.BlockSpec((tm, width), row), tspec, tspec, tspec, pl.BlockSpec((1, 1, width), lambda i: (i, 0, 0))]
        out_shape += [tshape(F32), jax.ShapeDtypeStruct((n, width), BF16), tshape(F32), tshape(F32), tshape(BF16),
                      jax.ShapeDtypeStruct((n // tm, 1, width), F32)]
    return pl.pallas_call(
        functools.partial(_even_inproj_kernel, key_minor=key_minor),
        grid=(n // tm,),
        in_specs=[pl.BlockSpec((tm, D_MODEL), row), pl.BlockSpec((1, D_MODEL), const),
                  pl.BlockSpec(w.shape, const),
                  pl.BlockSpec((tm, LANES), lambda i: (i % n_pos_blocks, 0)),
                  pl.BlockSpec((tm, LANES), lambda i: (i % n_pos_blocks, 0)),
                  pl.BlockSpec((1, 512), const), pl.BlockSpec((1, 512), const), pl.BlockSpec((512, 512), const)],
        out_specs=out_specs,
        out_shape=out_shape,
        compiler_params=_cparams("parallel"),
        name="even_inproj",
    )(x, g, w, cos, sin, qg, kg, bd)


def _retention_kernel(rq_ref, rk_ref, rv_ref, rg_ref, s0_ref, dmat_ref, qdec_ref, kdec_ref, sdec_ref,
                      o_ref, sout_ref, s_scr, *, n_seq, seq_len):
    c = pl.program_id(1)

    @pl.when(c == 0)
    def _():
        s_scr[...] = s0_ref[...]

    lane = _lane_iota((ROWS, LANES))
    row = _row_iota((ROWS, LANES))
    for p in range(RET_HEADS // 2):
        q2 = rq_ref[:, p * LANES:(p + 1) * LANES]
        k2 = rk_ref[:, p * LANES:(p + 1) * LANES]
        kd_t = (k2 * kdec_ref[p]).T
        v_pair = []
        for e in range(2):
            h = 2 * p + e
            head_lanes = (lane < RET_DK) if e == 0 else (lane >= RET_DK)
            qm = jnp.where(head_lanes, q2, 0.0)
            scores = _dot_nt(qm, k2) * dmat_ref[h]
            vh = rv_ref[:, h * RET_DV:(h + 1) * RET_DV]
            v_pair.append(vh)
            if n_seq == 1:
                inter = _dot(qm, s_scr[0, p])
            else:
                inter = jnp.zeros((ROWS, RET_DV), F32)
                for s in range(n_seq):
                    in_seq = (row >= s * seq_len) & (row < (s + 1) * seq_len)
                    inter = inter + jnp.where(in_seq, _dot(qm, s_scr[s, p]), 0.0)
            o = _dot(scores, vh) + inter * qdec_ref[h]
            gate = rg_ref[:, h * RET_DV:(h + 1) * RET_DV]
            o_ref[:, h * RET_DV:(h + 1) * RET_DV] = jax.nn.silu(gate) * _rmsnorm(o)
        for s in range(n_seq):
            if n_seq == 1:
                kt = kd_t
            else:
                kt = jnp.where((lane >= s * seq_len) & (lane < (s + 1) * seq_len), kd_t, 0.0)
            upd = jnp.where(row < RET_DK, _dot(kt, v_pair[0]), _dot(kt, v_pair[1]))
            s_scr[s, p] = s_scr[s, p] * sdec_ref[p] + upd

    @pl.when(c == pl.num_programs(1) - 1)
    def _():
        sout_ref[...] = s_scr[...]


def _retention_tables(seq_len):
    log_g = jnp.log1p(-jnp.exp2(-5.0 - jnp.arange(RET_HEADS, dtype=F32)))
    r = jnp.arange(ROWS)
    li = (r % seq_len).astype(F32)
    sid = r // seq_len
    diff = li[:, None] - li[None, :]
    same = sid[:, None] == sid[None, :]
    dmat = jnp.where(same[None] & (diff[None] >= 0),
                     jnp.exp(jnp.maximum(diff, 0.0)[None] * log_g[:, None, None]), 0.0)
    qdec = jnp.broadcast_to(jnp.exp((li + 1.0)[None, :, None] * log_g[:, None, None]), (RET_HEADS, ROWS, RET_DV))
    lane_head = jnp.arange(LANES) // RET_DK
    pair_log_g = log_g.reshape(RET_HEADS // 2, 2)[:, lane_head]
    kdec = jnp.exp((seq_len - 1.0 - li)[None, :, None] * pair_log_g[:, None, :])
    sdec = jnp.broadcast_to(jnp.exp(float(seq_len) * pair_log_g)[:, :, None], (RET_HEADS // 2, LANES, RET_DV))
    return dmat, qdec, kdec, sdec


def _retention(rq, rk, rv, rg, s0, n_batch, n_tok):
    seq_len = min(ROWS, n_tok)
    n_seq = ROWS // seq_len
    n_chunks = n_tok // seq_len
    tables = _retention_tables(seq_len)
    s0p = s0.reshape(n_batch, RET_HEADS // 2, 2 * RET_DK, RET_DV)
    blk = lambda i, c: (i * n_chunks + c, 0)
    st = lambda i, c: (i, 0, 0, 0)
    c3 = lambda i, c: (0, 0, 0)
    o, s_out = pl.pallas_call(
        functools.partial(_retention_kernel, n_seq=n_seq, seq_len=seq_len),
        grid=(n_batch // n_seq, n_chunks),
        in_specs=[pl.BlockSpec((ROWS, 512), blk), pl.BlockSpec((ROWS, 512), blk),
                  pl.BlockSpec((ROWS, 1024), blk), pl.BlockSpec((ROWS, 1024), blk),
                  pl.BlockSpec((n_seq,) + s0p.shape[1:], st)]
                 + [pl.BlockSpec(t.shape, c3) for t in tables],
        out_specs=[pl.BlockSpec((ROWS, 1024), blk), pl.BlockSpec((n_seq,) + s0p.shape[1:], st)],
        out_shape=[jax.ShapeDtypeStruct((n_batch * n_tok, 1024), F32), jax.ShapeDtypeStruct(s0p.shape, F32)],
        scratch_shapes=[pltpu.VMEM((n_seq,) + s0p.shape[1:], F32)],
        compiler_params=_cparams("parallel", "arbitrary"),
        name="retention",
    )(rq, rk, rv, rg, s0p, *tables)
    return o, s_out.reshape(s0.shape)


def _moba_prompt_kernel(qt_ref, k_ref, vt_ref, kbar_ref, o_ref, sel_scr, qtm_scr, m_scr, l_scr, acc_scr, *, n_blocks):
    qb = pl.program_id(1)
    blk = MOBA_BLOCK
    n_pairs = MOBA_HEADS // 2
    blk_row = _row_iota((n_blocks, blk))
    lo_rows = _row_iota((LANES, blk)) < MOBA_HD
    lo_lanes = _lane_iota((n_blocks, LANES)) < MOBA_HD
    key_le_query = _row_iota((blk, blk)) <= _lane_iota((blk, blk))

    for p in range(n_pairs):
        rows = slice(p * LANES, (p + 1) * LANES)
        qt2 = qt_ref[0, rows, :]
        kb2 = kbar_ref[:, rows]
        for e in range(2):
            h = 2 * p + e
            gate = _dot(jnp.where(lo_lanes if e == 0 else ~lo_lanes, kb2, 0.0), qt2)
            cnt = jnp.zeros((n_blocks, blk), F32)
            for m in range(n_blocks - 1):
                gm = gate[m:m + 1, :]
                ahead = (gm > gate) | ((gm == gate) & (blk_row > m))
                cnt = cnt + jnp.where(ahead, (qb > m).astype(F32), 0.0)
            sel_scr[h] = jnp.where((blk_row < qb) & (cnt < float(MOBA_TOPK)), 1.0, 0.0)
            head_rows = lo_rows if e == 0 else ~lo_rows
            qtm_scr[h] = (jnp.where(head_rows, qt2, 0.0) * (MOBA_HD ** -0.5)).astype(BF16)

    def attend(n, own, span=1):
        c0 = pl.multiple_of(n * blk, blk)
        keys = span * blk
        for p in range(n_pairs):
            rows = slice(p * LANES, (p + 1) * LANES)
            k2 = k_ref[pl.ds(c0, keys), rows]
            for e in range(2):
                h = 2 * p + e
                st = jnp.dot(k2, qtm_scr[h], preferred_element_type=F32)
                if own:
                    st = jnp.where(key_le_query, st, _NEG_INF)
                else:
                    picked = sel_scr[h, pl.ds(n, 1), :]
                    for j in range(1, span):
                        picked = jnp.where(_row_iota((keys, blk)) < j * blk, picked, sel_scr[h, pl.ds(n + j, 1), :])
                    st = jnp.where(picked > 0.0, st, _NEG_INF)
                mx = jnp.max(st, axis=0, keepdims=True)
                if own:
                    m_new = mx
                else:
                    m_old = m_scr[h:h + 1, :]
                    m_new = jnp.maximum(m_old, mx)
                    alpha = jnp.exp(m_old - m_new)
                pt = jnp.exp(st - m_new)
                psum = jnp.sum(pt, axis=0, keepdims=True)
                vt = vt_ref[0, h * MOBA_HD:(h + 1) * MOBA_HD, pl.ds(c0, keys)]
                pv = jnp.dot(vt, pt.astype(BF16), preferred_element_type=F32)
                l_scr[h:h + 1, :] = psum if own else l_scr[h:h + 1, :] * alpha + psum
                acc_scr[h] = pv if own else acc_scr[h] * alpha + pv
                m_scr[h:h + 1, :] = m_new

    attend(qb, True)

    def body(j, c):
        attend(2 * j, False, span=2)
        return c

    lax.fori_loop(0, lax.shift_right_logical(qb, 1), body, 0)

    @pl.when((qb & 1) == 1)
    def _():
        attend(qb - 1, False)
    for p in range(n_pairs):
        pair = [acc_scr[h] / l_scr[h:h + 1, :] for h in (2 * p, 2 * p + 1)]
        o_ref[:, p * LANES:(p + 1) * LANES] = jnp.concatenate(pair, axis=0).T


def _moba_prompt(qt, kb, vtb, kbar, n_batch, n_tok):
    n_blocks = n_tok // MOBA_BLOCK
    width = MOBA_HEADS * MOBA_HD
    return pl.pallas_call(
        functools.partial(_moba_prompt_kernel, n_blocks=n_blocks),
        grid=(n_batch, n_blocks),
        in_specs=[pl.BlockSpec((1, width, MOBA_BLOCK), lambda b, i: (b, 0, i)),
                  pl.BlockSpec((n_tok, width), lambda b, i: (b, 0)),
                  pl.BlockSpec((1, width, n_tok), lambda b, i: (b, 0, 0)),
                  pl.BlockSpec((n_blocks, width), lambda b, i: (b, 0))],
        out_specs=pl.BlockSpec((MOBA_BLOCK, width), lambda b, i: (b * n_blocks + i, 0)),
        out_shape=jax.ShapeDtypeStruct((n_batch * n_tok, width), F32),
        scratch_shapes=[pltpu.VMEM((MOBA_HEADS, n_blocks, MOBA_BLOCK), F32),
                        pltpu.VMEM((MOBA_HEADS, LANES, MOBA_BLOCK), BF16),
                        pltpu.VMEM((MOBA_HEADS, MOBA_BLOCK), F32), pltpu.VMEM((MOBA_HEADS, MOBA_BLOCK), F32),
                        pltpu.VMEM((MOBA_HEADS, MOBA_HD, MOBA_BLOCK), F32)],
        compiler_params=_cparams("parallel", "arbitrary"),
        name="moba_prompt",
    )(qt, kb, vtb, kbar)


PAGES_PER_BLOCK = MOBA_BLOCK // PAGE_SIZE
KBAR_BLOCKS_PER_STEP = 16


def _kbar_kernel(pt_ref, *refs):
    page_refs, o_ref = refs[:-1], refs[-1]
    for j in range(KBAR_BLOCKS_PER_STEP):
        tot = page_refs[j * PAGES_PER_BLOCK][0]
        for half in range(1, PAGES_PER_BLOCK):
            tot = tot + page_refs[j * PAGES_PER_BLOCK + half][0]
        o_ref[0, j] = jnp.sum(tot, axis=-1) * (1.0 / MOBA_BLOCK)


def _moba_kbar(pool_k, page_table):
    n_seq, n_pages = page_table.shape
    n_blocks = n_pages // PAGES_PER_BLOCK
    ppstep = KBAR_BLOCKS_PER_STEP * PAGES_PER_BLOCK

    def page_spec(j):
        return pl.BlockSpec((1, MOBA_HEADS, MOBA_HD, PAGE_SIZE),
                            lambda b, i, pt: (pt[b * n_pages + i * ppstep + j], 0, 0, 0))

    return pl.pallas_call(
        _kbar_kernel,
        grid_spec=pltpu.PrefetchScalarGridSpec(
            num_scalar_prefetch=1,
            grid=(n_seq, n_blocks // KBAR_BLOCKS_PER_STEP),
            in_specs=[page_spec(j) for j in range(ppstep)],
            out_specs=pl.BlockSpec((1, KBAR_BLOCKS_PER_STEP, MOBA_HEADS, MOBA_HD), lambda b, i, pt: (b, i, 0, 0)),
        ),
        out_shape=jax.ShapeDtypeStruct((n_seq, n_blocks, MOBA_HEADS, MOBA_HD), F32),
        compiler_params=_cparams("parallel", "parallel"),
        name="moba_kbar",
    )(page_table.reshape(-1), *([pool_k] * ppstep))


def _moba_select_kernel(q_ref, kbar_ref, o_ref, *, n_blocks, n_tok):
    lane = _lane_iota((n_tok, LANES))
    lane_f = lane.astype(F32)
    pad = jnp.zeros((LANES - n_blocks, LANES), F32)
    for h in range(MOBA_HEADS):
        p, e = divmod(h, 2)
        cols = slice(p * LANES, (p + 1) * LANES)
        q2 = q_ref[:, cols]
        head_lanes = (lane < MOBA_HD) if e == 0 else (lane >= MOBA_HD)
        qm = jnp.where(head_lanes, q2, 0.0)
        kb = jnp.concatenate([kbar_ref[0, :, cols], pad], axis=0)
        gate = jnp.where(lane < n_blocks, _dot_nt(qm, kb), _NEG_INF)
        res = jnp.zeros((n_tok, LANES), F32)
        for j in range(MOBA_TOPK):
            mx = jnp.max(gate, axis=1, keepdims=True)
            idx = jnp.min(jnp.where(gate == mx, lane_f, float(LANES)), axis=1, keepdims=True)
            res = jnp.where(lane == j, idx, res)
            gate = jnp.where(lane_f == idx, _NEG_INF, gate)
        o_ref[0, h * n_tok:(h + 1) * n_tok, :] = res.astype(I32)


def _moba_select(mq, kbar, n_seq, n_tok):
    n_blocks = kbar.shape[1]
    kb = kbar.reshape(n_seq, n_blocks, 512)
    out = pl.pallas_call(
        functools.partial(_moba_select_kernel, n_blocks=n_blocks, n_tok=n_tok),
        grid=(n_seq,),
        in_specs=[pl.BlockSpec((n_tok, 512), lambda b: (b, 0)), pl.BlockSpec((1, n_blocks, 512), lambda b: (b, 0, 0))],
        out_specs=pl.BlockSpec((1, MOBA_HEADS * n_tok, LANES), lambda b: (b, 0, 0)),
        out_shape=jax.ShapeDtypeStruct((n_seq, MOBA_HEADS * n_tok, LANES), I32),
        compiler_params=_cparams("parallel"),
        name="moba_select",
    )(mq, kb)
    return out[:, :, :MOBA_TOPK].reshape(-1)


def _moba_sample_kernel(sel_ref, pt_ref, q_ref, kn_ref, vn_ref, pk_ref, pv_ref, o_ref,
                        kbuf, vbuf, sem, *, n_tok, n_pages):
    s = pl.program_id(0)
    n_steps = pl.num_programs(0)
    n_sel = n_tok * MOBA_TOPK
    past = n_sel * MOBA_BLOCK

    def copies(step, slot):
        b = step // MOBA_HEADS
        h = step % MOBA_HEADS
        out = []
        for i in range(n_sel):
            blk = sel_ref[step * n_sel + i]
            for half in range(PAGES_PER_BLOCK):
                page = pt_ref[b * n_pages + blk * PAGES_PER_BLOCK + half]
                c0 = i * MOBA_BLOCK + half * PAGE_SIZE
                out.append(pltpu.make_async_copy(pk_ref.at[page, h], kbuf.at[slot, :, pl.ds(c0, PAGE_SIZE)], sem.at[slot, 0]))
                out.append(pltpu.make_async_copy(pv_ref.at[page, h], vbuf.at[slot, :, pl.ds(c0, PAGE_SIZE)], sem.at[slot, 1]))
        return out

    slot = s % 2

    @pl.when(s == 0)
    def _():
        for cp in copies(s, slot):
            cp.start()

    @pl.when(s + 1 < n_steps)
    def _():
        for cp in copies(s + 1, 1 - slot):
            cp.start()

    for cp in copies(s, slot):
        cp.wait()

    q = q_ref[0, 0] * (MOBA_HD ** -0.5)
    s_past = _dot(q, kbuf[slot])
    key = _lane_iota((n_tok, past))
    first_key = _row_iota((n_tok, past)) * (MOBA_TOPK * MOBA_BLOCK)
    s_past = jnp.where((key >= first_key) & (key < first_key + MOBA_TOPK * MOBA_BLOCK), s_past, _NEG_INF)
    s_own = _dot_nt(q, kn_ref[0, 0])
    s_own = jnp.where(_lane_iota((n_tok, n_tok)) <= _row_iota((n_tok, n_tok)), s_own, _NEG_INF)
    mx = jnp.maximum(jnp.max(s_past, axis=1, keepdims=True), jnp.max(s_own, axis=1, keepdims=True))
    p_past = jnp.exp(s_past - mx)
    p_own = jnp.exp(s_own - mx)
    den = jnp.sum(p_past, axis=1, keepdims=True) + jnp.sum(p_own, axis=1, keepdims=True)
    o = _dot_nt(p_past, vbuf[slot]) + _dot(p_own, vn_ref[0, 0])
    o_ref[0, 0] = o / den


def _moba_sample(mq, mk, mv, pool_k, pool_v, page_table, n_seq, n_tok):
    n_pages = page_table.shape[1]
    kbar = _moba_kbar(pool_k, page_table)
    sel = _moba_select(mq, kbar, n_seq, n_tok)

    def head_major(a):
        return a.reshape(n_seq, n_tok, MOBA_HEADS, MOBA_HD).transpose(0, 2, 1, 3)

    qh, kh, vh = head_major(mq), head_major(mk), head_major(mv)
    n_sel = n_tok * MOBA_TOPK
    spec = pl.BlockSpec((1, 1, n_tok, MOBA_HD), lambda s, sel, pt: (s // MOBA_HEADS, s % MOBA_HEADS, 0, 0))
    anyspec = pl.BlockSpec(memory_space=pl.ANY)
    o = pl.pallas_call(
        functools.partial(_moba_sample_kernel, n_tok=n_tok, n_pages=n_pages),
        grid_spec=pltpu.PrefetchScalarGridSpec(
            num_scalar_prefetch=2,
            grid=(n_seq * MOBA_HEADS,),
            in_specs=[spec, spec, spec, anyspec, anyspec],
            out_specs=spec,
            scratch_shapes=[pltpu.VMEM((2, MOBA_HD, n_sel * MOBA_BLOCK), F32),
                            pltpu.VMEM((2, MOBA_HD, n_sel * MOBA_BLOCK), F32),
                            pltpu.SemaphoreType.DMA((2, 2))],
        ),
        out_shape=jax.ShapeDtypeStruct(qh.shape, F32),
        compiler_params=_cparams("arbitrary"),
        name="moba_sample",
    )(sel, page_table.reshape(-1), qh, kh, vh, pool_k, pool_v)
    return o.transpose(0, 2, 1, 3).reshape(n_seq * n_tok, MOBA_HEADS * MOBA_HD)


def _outproj_kernel(*refs, n_in):
    x_ref, a_refs, w_refs, o_ref = refs[0], refs[1:1 + n_in], refs[1 + n_in:1 + 2 * n_in], refs[-1]
    acc = x_ref[...]
    for a_ref, w_ref in zip(a_refs, w_refs):
        acc = acc + jnp.dot(a_ref[...].astype(BF16), w_ref[...], preferred_element_type=F32)
    o_ref[...] = acc


def _outproj(x, acts, ws, tm):
    n = x.shape[0]
    row = lambda i: (i, 0)
    const = lambda i: (0, 0)
    return pl.pallas_call(
        functools.partial(_outproj_kernel, n_in=len(acts)),
        grid=(n // tm,),
        in_specs=[pl.BlockSpec((tm, D_MODEL), row)] + [pl.BlockSpec((tm, a.shape[1]), row) for a in acts]
                 + [pl.BlockSpec(w.shape, const) for w in ws],
        out_specs=pl.BlockSpec((tm, D_MODEL), row),
        out_shape=jax.ShapeDtypeStruct((n, D_MODEL), F32),
        compiler_params=_cparams("parallel"),
        name="outproj",
    )(x, *acts, *ws)


ROUTE_ROWS = 32


def _router_kernel(h_ref, g_ref, wr_ref, before_ref, route_ref, cnt_ref, cnt_scr):
    i = pl.program_id(0)

    @pl.when(i == 0)
    def _():
        cnt_scr[...] = jnp.zeros(cnt_scr.shape, F32)

    xn = _rmsnorm(h_ref[...], g_ref[...])
    logits = _dot(xn, wr_ref[...]).T[:ROUTE_ROWS, :]
    tm = logits.shape[1]
    lane = _row_iota((ROUTE_ROWS, tm)).astype(F32)
    big = float(LANES)

    def first_argmax(v):
        mx = jnp.max(v, axis=0, keepdims=True)
        return mx, jnp.min(jnp.where(v == mx, lane, big), axis=0, keepdims=True)

    is_group = lane < float(MOE_GROUPS)
    lg = jnp.where(is_group, logits, _NEG_INF)
    mg, gi = first_argmax(lg)
    wg = 1.0 / jnp.sum(jnp.where(is_group, jnp.exp(lg - mg), 0.0), axis=0, keepdims=True)
    e0 = float(MOE_GROUPS) + float(MOE_EPG) * gi
    in_group = (lane >= e0) & (lane < e0 + float(MOE_EPG))
    le = jnp.where(in_group, logits, _NEG_INF)
    m1, i1 = first_argmax(le)
    m2, i2 = first_argmax(jnp.where(lane == i1, _NEG_INF, le))
    r = jnp.exp(m2 - m1)
    g1 = wg / (1.0 + r)
    g2 = wg * r / (1.0 + r)
    a = jnp.minimum(i1, i2) - e0
    b = jnp.maximum(i1, i2) - e0
    ga = jnp.where(i1 < i2, g1, g2)
    gb = jnp.where(i1 < i2, g2, g1)
    pair = a * (7.0 - a) * 0.5 + (b - a - 1.0)
    cls = gi * 6.0 + pair
    onehot = jnp.where(lane == cls, 1.0, 0.0)
    before = jnp.dot(onehot.astype(BF16), before_ref[...], preferred_element_type=F32) + cnt_scr[:, 0:1]
    rank = jnp.sum(onehot * before, axis=0, keepdims=True)
    cnt_scr[...] = cnt_scr[...] + jnp.sum(onehot, axis=1, keepdims=True)
    out_row = _row_iota(route_ref.shape)
    route_ref[...] = jnp.where(out_row == 0, cls, jnp.where(out_row == 1, ga, jnp.where(out_row == 2, gb,
                               jnp.where(out_row == 3, rank, 0.0))))
    cnt_ref[...] = cnt_scr[...]


def _router(h, g, wr, tm):
    n = h.shape[0]
    earlier = (jnp.arange(tm)[:, None] < jnp.arange(tm)[None, :]).astype(BF16)
    row = lambda i: (i, 0)
    const = lambda i: (0, 0)
    return pl.pallas_call(
        _router_kernel,
        grid=(n // tm,),
        in_specs=[pl.BlockSpec((tm, D_MODEL), row), pl.BlockSpec((1, D_MODEL), const),
                  pl.BlockSpec((D_MODEL, LANES), const), pl.BlockSpec((tm, tm), const)],
        out_specs=[pl.BlockSpec((8, tm), lambda i: (0, i)), pl.BlockSpec((ROUTE_ROWS, LANES), const)],
        out_shape=[jax.ShapeDtypeStruct((8, n), F32), jax.ShapeDtypeStruct((ROUTE_ROWS, LANES), F32)],
        scratch_shapes=[pltpu.VMEM((ROUTE_ROWS, LANES), F32)],
        compiler_params=_cparams("arbitrary"),
        name="moe_router",
    )(h, g, wr, earlier)


def _ffn_kernel(ea_ref, eb_ref, nv_ref, tok_ref, gates_ref, g_ref, wga_ref, wua_ref, wda_ref, wgb_ref, wub_ref, wdb_ref,
                h_hbm, out_hbm, xbuf, ybuf, row_sized, gsem, ssem):
    i = pl.program_id(0)
    n_steps = pl.num_programs(0)
    slot = i % 2
    rows = xbuf.shape[1]

    def gather_tile(tile, buf_slot):
        for r in range(rows):
            tok = tok_ref[tile * rows + r]
            pltpu.make_async_copy(h_hbm.at[pl.ds(tok, 1), :], xbuf.at[buf_slot, pl.ds(r, 1), :],
                                  gsem.at[buf_slot]).start(priority=r % 2)

    def wait_gather(buf_slot):
        pltpu.make_async_copy(h_hbm.at[pl.ds(0, rows), :], xbuf.at[buf_slot], gsem.at[buf_slot]).wait()

    def scatter_tile(tile, buf_slot):
        def start_row(r, priority):
            tok = tok_ref[tile * rows + r]
            pltpu.make_async_copy(ybuf.at[buf_slot, pl.ds(r, 1), :], out_hbm.at[pl.ds(tok, 1), :],
                                  ssem.at[buf_slot]).start(priority=priority)

        @pl.when(nv_ref[tile] == rows)
        def _():
            for r in range(rows):
                start_row(r, r % 2)

        @pl.when(nv_ref[tile] < rows)
        def _():
            def body(r, c):
                start_row(r, 0)
                return c
            lax.fori_loop(0, nv_ref[tile], body, 0)

    def wait_scatter(tile, buf_slot):
        n = nv_ref[tile]

        @pl.when(n > 0)
        def _():
            pltpu.make_async_copy(row_sized.at[pl.ds(0, n)], row_sized.at[pl.ds(0, n)], ssem.at[buf_slot]).wait()

    @pl.when(i == 0)
    def _():
        gather_tile(0, 0)

    @pl.when((i == 0) | (nv_ref[jnp.maximum(i - 1, 0)] > 0))
    def _():
        wait_gather(slot)

    @pl.when(i >= 2)
    def _():
        wait_scatter(i - 2, slot)

    @pl.when(nv_ref[i] > 0)
    def _():
        gather_tile(i + 1, 1 - slot)
        x = xbuf[slot]
        xb = _rmsnorm(x, g_ref[...]).astype(BF16)
        y = x
        for lane, wg_ref, wu_ref, wd_ref in ((0, wga_ref, wua_ref, wda_ref), (1, wgb_ref, wub_ref, wdb_ref)):
            gate = gates_ref[:, lane:lane + 1]
            hid = jax.nn.silu(_dot(xb, wg_ref[0, 0])) * _dot(xb, wu_ref[0, 0])
            y = y + _dot(hid * gate, wd_ref[0, 0])
        ybuf[slot] = y
        scatter_tile(i, slot)

    @pl.when(i == n_steps - 1)
    def _():
        wait_scatter(i - 1, 1 - slot)


def _ffn(ea, eb, nvalid, tok, gates, h, g, wg, wu, wd, layer, rows):
    n_tiles = tok.shape[0] // rows
    tile = lambda i, *_: (i, 0)
    const = lambda i, *_: (0, 0)
    up = lambda sel: pl.BlockSpec((1, 1, D_MODEL, EXPERT_FF), sel)
    down = lambda sel: pl.BlockSpec((1, 1, EXPERT_FF, D_MODEL), sel)
    sel_a = lambda i, ea, eb, nv, tok: (layer, ea[i], 0, 0)
    sel_b = lambda i, ea, eb, nv, tok: (layer, eb[i], 0, 0)
    anyspec = pl.BlockSpec(memory_space=pl.ANY)
    return pl.pallas_call(
        _ffn_kernel,
        grid_spec=pltpu.PrefetchScalarGridSpec(
            num_scalar_prefetch=4,
            grid=(n_tiles,),
            in_specs=[pl.BlockSpec((rows, 2), tile), pl.BlockSpec((1, D_MODEL), const),
                      up(sel_a), up(sel_a), down(sel_a), up(sel_b), up(sel_b), down(sel_b), anyspec],
            out_specs=anyspec,
            scratch_shapes=[pltpu.VMEM((2, rows, D_MODEL), F32), pltpu.VMEM((2, rows, D_MODEL), F32),
                            pltpu.VMEM((rows, D_MODEL // LANES, LANES), F32),
                            pltpu.SemaphoreType.DMA((2,)), pltpu.SemaphoreType.DMA((2,))],
        ),
        out_shape=jax.ShapeDtypeStruct(h.shape, F32),
        compiler_params=_cparams("arbitrary"),
        name="moe_ffn",
    )(ea, eb, nvalid, tok, gates, g, wg, wu, wd, wg, wu, wd, h)


_PAIR_A = (0, 0, 0, 1, 1, 2)
_PAIR_B = (1, 2, 3, 2, 3, 3)


def _hier_moe(h, g, wr, wg, wu, wd, layer):
    n = h.shape[0]
    tile = MOE_TILE if n >= N_CLASSES * MOE_TILE else MOE_TILE_SMALL
    route, cnt = _router(h, g, wr, TOKEN_TILE)
    cls = route[0].astype(I32)
    rank = route[3].astype(I32)
    counts = cnt[:N_CLASSES, 0].astype(I32)
    tiles_c = (counts + tile - 1) // tile
    tile_end = jnp.cumsum(tiles_c)
    tile_start = tile_end - tiles_c
    pos = tile_start[cls] * tile + rank
    n_tiles = n // tile + N_CLASSES
    ti = jnp.arange(n_tiles, dtype=I32)
    total = tile_end[-1]
    tcls = jnp.searchsorted(tile_end, jnp.minimum(ti, total - 1), side="right").astype(I32)
    tcls = jnp.minimum(tcls, N_CLASSES - 1)
    nvalid = jnp.where(ti < total, jnp.clip(counts[tcls] - (ti - tile_start[tcls]) * tile, 0, tile), 0).astype(I32)
    grp, pair = tcls // 6, tcls % 6
    ea = grp * MOE_EPG + jnp.asarray(_PAIR_A, I32)[pair]
    eb = grp * MOE_EPG + jnp.asarray(_PAIR_B, I32)[pair]
    per_token = jnp.stack([jnp.arange(n, dtype=I32), lax.bitcast_convert_type(route[1], I32),
                           lax.bitcast_convert_type(route[2], I32)], axis=1)
    per_slot = jnp.zeros((n_tiles * tile, 3), I32).at[pos].set(per_token)
    tok = per_slot[:, 0]
    gates = lax.bitcast_convert_type(per_slot[:, 1:3], F32)
    return _ffn(ea, eb, nvalid, tok, gates, h, g, wg, wu, wd, layer, tile)


def _odd_inproj_kernel(x_ref, g_ref, wz_ref, wx_ref, wdt_ref, z_ref, xbc_ref, dt_ref):
    xb = _rmsnorm(x_ref[...], g_ref[...]).astype(BF16)
    z_ref[...] = jnp.dot(xb, wz_ref[...], preferred_element_type=F32)
    xbc_ref[...] = jnp.dot(xb, wx_ref[...], preferred_element_type=F32)
    dt_ref[...] = jnp.dot(xb, wdt_ref[...], preferred_element_type=F32)


def _odd_inproj(x, g, wz, wx, wdt, tm):
    n = x.shape[0]
    row = lambda i: (i, 0)
    const = lambda i: (0, 0)
    widths = (D_INNER, CONV_DIM, LANES)
    return pl.pallas_call(
        _odd_inproj_kernel,
        grid=(n // tm,),
        in_specs=[pl.BlockSpec((tm, D_MODEL), row), pl.BlockSpec((1, D_MODEL), const),
                  pl.BlockSpec(wz.shape, const), pl.BlockSpec(wx.shape, const), pl.BlockSpec(wdt.shape, const)],
        out_specs=[pl.BlockSpec((tm, c), row) for c in widths],
        out_shape=[jax.ShapeDtypeStruct((n, c), F32) for c in widths],
        compiler_params=_cparams("parallel"),
        name="odd_inproj",
    )(x, g, wz, wx, wdt)


CONV_COLS = 512


def _odd_inproj_conv_kernel(x_ref, g_ref, wz_ref, wx_ref, wdt_ref, cs_ref, cw_ref, cb_ref, dtb_ref,
                            z_ref, xc_ref, dts_ref, cso_ref, raw, *, tiles_per_seq):
    i = pl.program_id(0)
    tm = x_ref.shape[0]
    tail = CONV_W - 1
    xb = _rmsnorm(x_ref[...], g_ref[...]).astype(BF16)
    z_ref[...] = jnp.dot(xb, wz_ref[...], preferred_element_type=F32)
    dts_ref[...] = jax.nn.softplus(jnp.dot(xb, wdt_ref[...], preferred_element_type=F32) + dtb_ref[...])

    @pl.when(i % tiles_per_seq == 0)
    def _():
        raw[0:8, :] = jnp.zeros((8, CONV_DIM), F32)
        raw[8 - tail:8, :] = cs_ref[0]

    for j in range(CONV_DIM // CONV_COLS):
        cols = slice(j * CONV_COLS, (j + 1) * CONV_COLS)
        cur = jnp.dot(xb, wx_ref[:, cols], preferred_element_type=F32)
        raw[8:, cols] = cur
        acc = cb_ref[:, cols] + cur * cw_ref[tail:tail + 1, cols]
        for k in range(tail):
            acc = acc + raw[8 - tail + k:8 - tail + k + tm, cols] * cw_ref[k:k + 1, cols]
        xc_ref[:, cols] = jax.nn.silu(acc)
        new_tail = raw[8 + tm - tail:8 + tm, cols]
        cso_ref[0, :, cols] = new_tail
        raw[8 - tail:8, cols] = new_tail


def _odd_inproj_conv(x, g, wz, wx, wdt, conv_state, cw, cb, dtb, tm, n_tok):
    n = x.shape[0]
    tiles_per_seq = n_tok // tm
    row = lambda i: (i, 0)
    const = lambda i: (0, 0)
    st = lambda i: (i // tiles_per_seq, 0, 0)
    widths = (D_INNER, CONV_DIM, LANES)
    return pl.pallas_call(
        functools.partial(_odd_inproj_conv_kernel, tiles_per_seq=tiles_per_seq),
        grid=(n // tm,),
        in_specs=[pl.BlockSpec((tm, D_MODEL), row), pl.BlockSpec((1, D_MODEL), const),
                  pl.BlockSpec(wz.shape, const), pl.BlockSpec(wx.shape, const), pl.BlockSpec(wdt.shape, const),
                  pl.BlockSpec((1, CONV_W - 1, CONV_DIM), st),
                  pl.BlockSpec((CONV_W, CONV_DIM), const), pl.BlockSpec((1, CONV_DIM), const), pl.BlockSpec((1, LANES), const)],
        out_specs=[pl.BlockSpec((tm, c), row) for c in widths] + [pl.BlockSpec((1, CONV_W - 1, CONV_DIM), st)],
        out_shape=[jax.ShapeDtypeStruct((n, c), F32) for c in widths] + [jax.ShapeDtypeStruct(conv_state.shape, F32)],
        scratch_shapes=[pltpu.VMEM((8 + tm, CONV_DIM), F32)],
        compiler_params=_cparams("arbitrary"),
        name="odd_inproj_conv",
    )(x, g, wz, wx, wdt, conv_state, cw, cb, dtb)


def _conv_kernel(xbc_ref, dt_ref, cs_ref, w_ref, b_ref, dtb_ref, xc_ref, dts_ref, cso_ref, carry):
    c = pl.program_id(1)
    tt = xbc_ref.shape[0]
    tail = CONV_W - 1

    @pl.when(c == 0)
    def _():
        carry[...] = jnp.zeros(carry.shape, F32)
        carry[8 - tail:8, :] = cs_ref[0]

    for j in range(CONV_DIM // CONV_COLS):
        cols = slice(j * CONV_COLS, (j + 1) * CONV_COLS)
        x = xbc_ref[:, cols]
        full = jnp.concatenate([carry[:, cols], x], axis=0)
        acc = b_ref[:, cols] + x * w_ref[tail:tail + 1, cols]
        for k in range(tail):
            acc = acc + full[8 - tail + k:8 - tail + k + tt, :] * w_ref[k:k + 1, cols]
        xc_ref[:, cols] = jax.nn.silu(acc)
        new_tail = full[8 + tt - tail:8 + tt, :]
        carry[8 - tail:8, cols] = new_tail
        cso_ref[0, :, cols] = new_tail
    dts_ref[...] = jax.nn.softplus(dt_ref[...] + dtb_ref[...])


def _conv(xbc, dt, conv_state, w, b, dtb, n_batch, n_tok, tt):
    n_chunks = n_tok // tt
    blk = lambda i, c: (i * n_chunks + c, 0)
    const = lambda i, c: (0, 0)
    st = lambda i, c: (i, 0, 0)
    return pl.pallas_call(
        _conv_kernel,
        grid=(n_batch, n_chunks),
        in_specs=[pl.BlockSpec((tt, CONV_DIM), blk), pl.BlockSpec((tt, LANES), blk),
                  pl.BlockSpec((1, CONV_W - 1, CONV_DIM), st),
                  pl.BlockSpec((CONV_W, CONV_DIM), const), pl.BlockSpec((1, CONV_DIM), const), pl.BlockSpec((1, LANES), const)],
        out_specs=[pl.BlockSpec((tt, CONV_DIM), blk), pl.BlockSpec((tt, LANES), blk),
                   pl.BlockSpec((1, CONV_W - 1, CONV_DIM), st)],
        out_shape=[jax.ShapeDtypeStruct(xbc.shape, F32), jax.ShapeDtypeStruct(dt.shape, F32),
                   jax.ShapeDtypeStruct(conv_state.shape, F32)],
        scratch_shapes=[pltpu.VMEM((8, CONV_DIM), F32)],
        compiler_params=_cparams("parallel", "arbitrary"),
        name="ssd_conv",
    )(xbc, dt, conv_state, w, b, dtb)


def _ssd_kernel(xs_ref, bm_ref, cm_ref, z_ref, dt_ref, s0_ref, a_ref, dsk_ref, ng_ref, tril_ref, mask_ref, elast_ref,
                o_ref, sout_ref, s_scr, yz_scr, *, n_seq, seq_len, gps):
    c = pl.program_id(2)

    @pl.when(c == 0)
    def _():
        s_scr[...] = s0_ref[...]

    lane = _lane_iota((ROWS, LANES))
    row = _row_iota((ROWS, LANES))
    lo_half = lane < (LANES // 2)
    dt = dt_ref[...]
    cs = _dot01(tril_ref[...], dt * a_ref[...])
    cs_last = _dot01(elast_ref[...], cs)
    if gps != SSM_GROUPS:
        shift = (LANES - pl.program_id(1) * (gps * SSM_HEADS // SSM_GROUPS)) % LANES
        dt, cs, cs_last = (pltpu.roll(v, shift, 1) for v in (dt, cs, cs_last))
    cs_t = cs.T
    ecs = jnp.exp(cs)
    dt_tail = dt * jnp.exp(cs_last - cs)
    causal = mask_ref[...] > 0.0

    def per_head(v, h0):
        return jnp.where(lo_half, v[:, h0:h0 + 1], v[:, h0 + 1:h0 + 2])

    for g in range(gps):
        gcols = slice(g * D_STATE, (g + 1) * D_STATE)
        cg = cm_ref[:, gcols].astype(BF16)
        bg = bm_ref[:, gcols].astype(BF16)
        cb = _dot_nt(cg, bg)
        for pp in range(2):
            pair = 2 * g + pp
            h0 = 2 * pair
            pcols = slice(pair * LANES, (pair + 1) * LANES)
            x2 = xs_ref[:, pcols]
            xdt = x2 * per_head(dt, h0)
            ys = []
            for e in range(2):
                h = h0 + e
                seg = cs[:, h:h + 1] - cs_t[h:h + 1, :]
                wmat = jnp.exp(jnp.where(causal, seg, _NEG_INF)) * cb
                ys.append(_dot(wmat, xdt))
            y2 = jnp.where(lo_half, ys[0], ys[1])
            if n_seq == 1:
                inter = _dot_nt(cg, s_scr[0, pair])
            else:
                inter = jnp.zeros((ROWS, LANES), F32)
                for s in range(n_seq):
                    in_seq = (row >= s * seq_len) & (row < (s + 1) * seq_len)
                    inter = inter + jnp.where(in_seq, _dot_nt(cg, s_scr[s, pair]), 0.0)
            y2 = y2 + inter * per_head(ecs, h0) + dsk_ref[:, pcols] * x2
            yz_scr[:, pcols] = y2 * jax.nn.silu(z_ref[:, pcols])
            xw_t = (x2 * per_head(dt_tail, h0)).T
            for s in range(n_seq):
                last = (s + 1) * seq_len - 1
                if n_seq == 1:
                    xt = xw_t
                else:
                    xt = jnp.where((lane >= s * seq_len) & (lane <= last), xw_t, 0.0)
                dec = jnp.exp(jnp.where(row < (LANES // 2), cs_t[h0:h0 + 1, last:last + 1], cs_t[h0 + 1:h0 + 2, last:last + 1]))
                s_scr[s, pair] = s_scr[s, pair] * dec + _dot(xt, bg)
    gw = D_INNER // SSM_GROUPS
    for g in range(gps):
        cols = slice(g * gw, (g + 1) * gw)
        o_ref[:, cols] = _rmsnorm(yz_scr[:, cols], ng_ref[:, cols])

    @pl.when(c == pl.num_programs(2) - 1)
    def _():
        sout_ref[...] = s_scr[...]


def _ssd(xc, z, dts, s0, a_row, dsk_row, ng_row, n_batch, n_tok):
    seq_len = min(ROWS, n_tok)
    n_seq = ROWS // seq_len
    n_chunks = n_tok // seq_len
    r = jnp.arange(ROWS)
    li, sid = r % seq_len, r // seq_len
    mask = ((sid[:, None] == sid[None, :]) & (li[:, None] >= li[None, :]))
    elast = (r[None, :] == (sid * seq_len + seq_len - 1)[:, None])
    n_pairs = SSM_HEADS // 2
    s0p = s0.reshape(n_batch, n_pairs, LANES, D_STATE)
    gps = SSM_GROUPS if n_seq == 1 else 1
    xw = gps * (D_INNER // SSM_GROUPS)
    bw = gps * D_STATE
    rows = lambda i, j, c: i * n_chunks + c
    xblk = lambda i, j, c: (rows(i, j, c), j)
    bblk = lambda i, j, c: (rows(i, j, c), D_INNER // bw + j)
    cblk = lambda i, j, c: (rows(i, j, c), (D_INNER + SSM_GROUPS * D_STATE) // bw + j)
    dtblk = lambda i, j, c: (rows(i, j, c), 0)
    st = lambda i, j, c: (i, j, 0, 0)
    const = lambda i, j, c: (0, 0)
    gconst = lambda i, j, c: (0, j)
    state_block = (n_seq, 2 * gps, LANES, D_STATE)
    o, s_out = pl.pallas_call(
        functools.partial(_ssd_kernel, n_seq=n_seq, seq_len=seq_len, gps=gps),
        grid=(n_batch // n_seq, SSM_GROUPS // gps, n_chunks),
        in_specs=[pl.BlockSpec((ROWS, xw), xblk), pl.BlockSpec((ROWS, bw), bblk), pl.BlockSpec((ROWS, bw), cblk),
                  pl.BlockSpec((ROWS, xw), xblk), pl.BlockSpec((ROWS, LANES), dtblk),
                  pl.BlockSpec(state_block, st),
                  pl.BlockSpec((1, LANES), const), pl.BlockSpec((1, xw), gconst), pl.BlockSpec((1, xw), gconst),
                  pl.BlockSpec((ROWS, ROWS), const), pl.BlockSpec((ROWS, ROWS), const), pl.BlockSpec((ROWS, ROWS), const)],
        out_specs=[pl.BlockSpec((ROWS, xw), xblk), pl.BlockSpec(state_block, st)],
        out_shape=[jax.ShapeDtypeStruct((n_batch * n_tok, D_INNER), F32), jax.ShapeDtypeStruct(s0p.shape, F32)],
        scratch_shapes=[pltpu.VMEM(state_block, F32), pltpu.VMEM((ROWS, xw), F32)],
        compiler_params=_cparams("parallel", "parallel", "arbitrary"),
        name="ssd_scan",
    )(xc, xc, xc, z, dts, s0p, a_row, dsk_row, ng_row, mask.astype(BF16), mask.astype(F32), elast.astype(BF16))
    return o, s_out.reshape(s0.shape)


def _rope_tables(pos):
    half = MOBA_HD // 2
    inv = jnp.power(ROPE_THETA, -jnp.arange(half, dtype=F32) * 2.0 / MOBA_HD)
    ang = pos[:, None] * inv[None, :]
    cos, sin = jnp.cos(ang), jnp.sin(ang)
    reps = LANES // MOBA_HD
    return jnp.tile(jnp.concatenate([cos, cos], axis=1), (1, reps)), jnp.tile(jnp.concatenate([-sin, sin], axis=1), (1, reps))


def _pad_cols(w, width):
    return jnp.pad(w, ((0, 0), (0, width - w.shape[1])))


TOKEN_TILE = 256


def _even_layer(h, nb, nt, pos, ret0, cache, norm_g, w_in, w_out, q_g, k_g):
    tm = TOKEN_TILE
    cos, sin = _rope_tables(pos)
    if nt < tm:
        cos, sin = jnp.tile(cos, (tm // nt, 1)), jnp.tile(sin, (tm // nt, 1))
    head_tile = lambda v: jnp.tile(v, MOBA_HEADS)[None, :]
    width = MOBA_HEADS * MOBA_HD
    blockdiag = (jnp.arange(width)[:, None] // MOBA_HD == jnp.arange(width)[None, :] // MOBA_HD).astype(BF16)
    outs = _even_inproj(h, norm_g[None, :], w_in.astype(BF16), cos, sin, head_tile(q_g), head_tile(k_g), blockdiag,
                        tm, nt, key_minor=cache is None)
    rq, rk, rv, rg = outs[:4]
    if cache is None:
        qt, kb, kt, vt, vtb, kbar = outs[4:]
        mo = _moba_prompt(qt, kb, vtb, kbar.reshape(-1, width), nb, nt)
        mk, mv = (a.reshape(nb, MOBA_HEADS, MOBA_HD, nt).transpose(0, 3, 1, 2) for a in (kt, vt))
    else:
        mq, mk, mv = outs[4:]
        mo = _moba_sample(mq, mk, mv, *cache, nb, nt)
        mk, mv = (a.reshape(nb, nt, MOBA_HEADS, MOBA_HD) for a in (mk, mv))
    ro, ret_state = _retention(rq, rk, rv, rg, ret0, nb, nt)
    w_out = w_out.astype(BF16)
    h = _outproj(h, (ro, mo), (w_out[:RET_HEADS * RET_DV], w_out[RET_HEADS * RET_DV:]), tm)
    return h, ret_state, mk, mv


def _odd_layer(h, nb, nt, conv0, ssm0, norm_g, w_in, conv_w, conv_b, dt_bias, a_log, d_skip, ssd_norm, w_out):
    tm = TOKEN_TILE
    w_in = w_in.astype(BF16)
    weights = (w_in[:, :D_INNER], w_in[:, D_INNER:D_INNER + CONV_DIM], _pad_cols(w_in[:, D_INNER + CONV_DIM:], LANES))
    conv_params = (conv_w, conv_b[None, :], _pad_cols(dt_bias[None, :], LANES))
    if nt % tm == 0:
        z, xc, dts, conv_state = _odd_inproj_conv(h, norm_g[None, :], *weights, conv0, *conv_params, tm, nt)
    else:
        z, xbc, dt = _odd_inproj(h, norm_g[None, :], *weights, tm)
        xc, dts, conv_state = _conv(xbc, dt, conv0, *conv_params, nb, nt, nt)
    a_row = _pad_cols(-jnp.exp(a_log)[None, :], LANES)
    dsk_row = jnp.repeat(d_skip, D_INNER // SSM_HEADS)[None, :]
    yzn, ssm_state = _ssd(xc, z, dts, ssm0, a_row, dsk_row, ssd_norm[None, :], nb, nt)
    h = _outproj(h, (yzn,), (w_out.astype(BF16),), tm)
    return h, conv_state, ssm_state


def _moe_layer(h, norm_g, w_rg, w_re, w_g, w_u, w_d, layer):
    wr = _pad_cols(jnp.concatenate([w_rg, w_re], axis=1), LANES)
    return _hier_moe(h, norm_g[None, :], wr, w_g, w_u, w_d, layer)


def kernel(x_prompt, x_sample, state_ret, cache_k, cache_v, page_table, state_ssm, state_conv, norm_mix, norm_ffn,
           w_in_even, w_out_even, q_norm, k_norm, w_in_odd, conv_w, conv_b, dt_bias, a_log, d_skip, ssd_norm, w_out_odd,
           w_router_group, w_router_expert, w_expert_gate, w_expert_up, w_expert_down):
    bp, seq, _ = x_prompt.shape
    bs, dec_seq, _ = x_sample.shape
    past_len = page_table.shape[1] * PAGE_SIZE
    groups = ((x_prompt.reshape(bp * seq, D_MODEL), bp, seq, jnp.arange(seq, dtype=F32)),
              (x_sample.reshape(bs * dec_seq, D_MODEL), bs, dec_seq, past_len + jnp.arange(dec_seq, dtype=F32)))

    def moe(h, layer):
        return _moe_layer(h, norm_ffn[layer], w_router_group[layer], w_router_expert[layer],
                          w_expert_gate, w_expert_up, w_expert_down, layer)

    outs = []
    for gi, (h, nb, nt, pos) in enumerate(groups):
        if gi == 0:
            ret0 = jnp.zeros((nb, RET_HEADS, RET_DK, RET_DV), F32)
            cache = None
            conv0 = jnp.zeros((nb, CONV_W - 1, CONV_DIM), F32)
            ssm0 = jnp.zeros((nb, SSM_HEADS, D_INNER // SSM_HEADS, D_STATE), F32)
        else:
            ret0 = state_ret[0]
            cache = (cache_k[0].transpose(0, 2, 3, 1), cache_v[0].transpose(0, 2, 3, 1), page_table)
            conv0, ssm0 = state_conv[0], state_ssm[0]
        h, ret_state, mk, mv = _even_layer(h, nb, nt, pos, ret0, cache, norm_mix[0], w_in_even[0], w_out_even[0],
                                           q_norm[0], k_norm[0])
        h = moe(h, 0)
        h, conv_state, ssm_state = _odd_layer(h, nb, nt, conv0, ssm0, norm_mix[1], w_in_odd[0], conv_w[0], conv_b[0],
                                              dt_bias[0], a_log[0], d_skip[0], ssd_norm[0], w_out_odd[0])
        h = moe(h, 1)
        outs.append(dict(h=h.reshape(nb, nt, D_MODEL), ret=ret_state[None], k=mk[None], v=mv[None],
                         ssm=ssm_state[None], conv=conv_state[None]))
    p, s = outs
    return (p["h"], s["h"], p["ret"], s["ret"], p["k"], p["v"], s["k"], s["v"], p["ssm"], s["ssm"], p["conv"], s["conv"])
```

```python
import functools
import math

import jax
import jax.numpy as jnp
import numpy as np
from jax import lax
from jax.experimental import pallas as pl
from jax.experimental.pallas import tpu as pltpu

F32 = jnp.float32
BF16 = jnp.bfloat16
I32 = jnp.int32

D_MODEL = 1024
PAGE_SIZE = 128
RET_HEADS = 8
RET_DK = 64
RET_DV = 128
MOBA_HEADS = 8
MOBA_HD = 64
MOBA_BLOCK = 256
MOBA_TOPK = 3
ROPE_THETA = 10000.0
D_INNER = 2048
SSM_HEADS = 32
SSM_GROUPS = 8
D_STATE = 128
CONV_W = 4
CONV_DIM = 4096
MOE_GROUPS = 4
MOE_EPG = 4
N_EXPERTS = 16
EXPERT_FF = 512
NORM_EPS = 1e-6
N_CLASSES = MOE_GROUPS * 6
LANES = 128
ROWS = 128
MOE_TILE = 256
MOE_TILE_SMALL = 32
VMEM_LIMIT = 56 * 1024 * 1024

_NEG_INF = float("-inf")


def _cparams(*sem):
    return pltpu.CompilerParams(dimension_semantics=sem, vmem_limit_bytes=VMEM_LIMIT)


def _dot(a, b):
    return jnp.dot(a.astype(BF16), b.astype(BF16), preferred_element_type=F32)


def _dot_nt(a, b):
    return lax.dot_general(a.astype(BF16), b.astype(BF16), (((1,), (1,)), ((), ())), preferred_element_type=F32)


def _split3(x):
    hi = x.astype(BF16)
    r = x - hi.astype(F32)
    mid = r.astype(BF16)
    lo = (r - mid.astype(F32)).astype(BF16)
    return hi, mid, lo


def _dot01(m01, x):
    hi, mid, lo = _split3(x)
    return (jnp.dot(m01, hi, preferred_element_type=F32) + jnp.dot(m01, mid, preferred_element_type=F32)
            + jnp.dot(m01, lo, preferred_element_type=F32))


def _rmsnorm(x, g=None):
    y = x * lax.rsqrt(jnp.mean(x * x, axis=-1, keepdims=True) + NORM_EPS)
    return y if g is None else y * g


def _lane_iota(shape):
    return lax.broadcasted_iota(I32, shape, len(shape) - 1)


def _row_iota(shape):
    return lax.broadcasted_iota(I32, shape, len(shape) - 2)


def _even_inproj_kernel(x_ref, g_ref, w_ref, cos_ref, sin_ref, qg_ref, kg_ref, bd_ref,
                        rq_ref, rk_ref, rv_ref, rg_ref, *moba_refs, key_minor):
    xb = _rmsnorm(x_ref[...], g_ref[...]).astype(BF16)
    cos = cos_ref[...]
    sin = sin_ref[...]
    first_half = (_lane_iota(cos.shape) % MOBA_HD) < (MOBA_HD // 2)

    def proj(c0, c1):
        return jnp.dot(xb, w_ref[:, c0:c1], preferred_element_type=F32)

    def rope(a):
        outs = []
        for j in range(a.shape[1] // LANES):
            s = a[:, j * LANES:(j + 1) * LANES]
            rot = jnp.where(first_half, pltpu.roll(s, LANES - MOBA_HD // 2, 1), pltpu.roll(s, MOBA_HD // 2, 1))
            outs.append(s * cos + rot * sin)
        return jnp.concatenate(outs, axis=1)

    def qk_norm(a, gam):
        ss = jnp.dot((a * a).astype(BF16), bd_ref[...], preferred_element_type=F32)
        return a * lax.rsqrt(ss * (1.0 / MOBA_HD) + NORM_EPS) * gam

    rq_ref[...] = rope(proj(0, 512))
    rk_ref[...] = rope(proj(512, 1024)) * (RET_DK ** -0.5)
    rv_ref[...] = proj(1024, 2048)
    rg_ref[...] = proj(2048, 3072)
    mq = rope(qk_norm(proj(3072, 3584), qg_ref[...]))
    mk = rope(qk_norm(proj(3584, 4096), kg_ref[...]))
    mv = proj(4096, 4608)
    if not key_minor:
        mq_ref, mk_ref, mv_ref = moba_refs
        mq_ref[...] = mq
        mk_ref[...] = mk
        mv_ref[...] = mv
    else:
        qt_ref, kb_ref, kt_ref, vt_ref, vtb_ref, kbar_ref = moba_refs
        qt_ref[0] = mq.T
        kb_ref[...] = mk.astype(BF16)
        kt_ref[0] = mk.T
        vt = mv.T
        vt_ref[0] = vt
        vtb_ref[0] = vt.astype(BF16)
        kbar_ref[0] = jnp.mean(mk, axis=0, keepdims=True)


def _even_inproj(x, g, w, cos, sin, qg, kg, bd, tm, n_tok, key_minor):
    n = x.shape[0]
    n_pos_blocks = cos.shape[0] // tm
    row = lambda i: (i, 0)
    const = lambda i: (0, 0)
    width = MOBA_HEADS * MOBA_HD
    out_specs = [pl.BlockSpec((tm, c), row) for c in (512, 512, 1024, 1024)]
    out_shape = [jax.ShapeDtypeStruct((n, c), F32) for c in (512, 512, 1024, 1024)]
    if not key_minor:
        out_specs += [pl.BlockSpec((tm, width), row)] * 3
        out_shape += [jax.ShapeDtypeStruct((n, width), F32)] * 3
    else:
        assert tm == MOBA_BLOCK and n_tok % tm == 0
        tiles = n_tok // tm
        tspec = pl.BlockSpec((1, width, tm), lambda i: (i // tiles, 0, i % tiles))
        tshape = lambda dt: jax.ShapeDtypeStruct((n // n_tok, width, n_tok), dt)
        out_specs += [tspec, pl.BlockSpec((tm, width), row), tspec, tspec, tspec, pl.BlockSpec((1, 1, width), lambda i: (i, 0, 0))]
        out_shape += [tshape(F32), jax.ShapeDtypeStruct((n, width), BF16), tshape(F32), tshape(F32), tshape(BF16),
                      jax.ShapeDtypeStruct((n // tm, 1, width), F32)]
    return pl.pallas_call(
        functools.partial(_even_inproj_kernel, key_minor=key_minor),
        grid=(n // tm,),
        in_specs=[pl.BlockSpec((tm, D_MODEL), row), pl.BlockSpec((1, D_MODEL), const),
                  pl.BlockSpec(w.shape, const),
                  pl.BlockSpec((tm, LANES), lambda i: (i % n_pos_blocks, 0)),
                  pl.BlockSpec((tm, LANES), lambda i: (i % n_pos_blocks, 0)),
                  pl.BlockSpec((1, 512), const), pl.BlockSpec((1, 512), const), pl.BlockSpec((512, 512), const)],
        out_specs=out_specs,
        out_shape=out_shape,
        compiler_params=_cparams("parallel"),
        name="even_inproj",
    )(x, g, w, cos, sin, qg, kg, bd)


def _retention_kernel(rq_ref, rk_ref, rv_ref, rg_ref, s0_ref, dmat_ref, qdec_ref, kdec_ref, sdec_ref,
                      o_ref, sout_ref, s_scr, *, n_seq, seq_len):
    c = pl.program_id(1)

    @pl.when(c == 0)
    def _():
        s_scr[...] = s0_ref[...]

    lane = _lane_iota((ROWS, LANES))
    row = _row_iota((ROWS, LANES))
    for p in range(RET_HEADS // 2):
        q2 = rq_ref[:, p * LANES:(p + 1) * LANES]
        k2 = rk_ref[:, p * LANES:(p + 1) * LANES]
        kd_t = (k2 * kdec_ref[p]).T
        v_pair = []
        for e in range(2):
            h = 2 * p + e
            head_lanes = (lane < RET_DK) if e == 0 else (lane >= RET_DK)
            qm = jnp.where(head_lanes, q2, 0.0)
            scores = _dot_nt(qm, k2) * dmat_ref[h]
            vh = rv_ref[:, h * RET_DV:(h + 1) * RET_DV]
            v_pair.append(vh)
            if n_seq == 1:
                inter = _dot(qm, s_scr[0, p])
            else:
                inter = jnp.zeros((ROWS, RET_DV), F32)
                for s in range(n_seq):
                    in_seq = (row >= s * seq_len) & (row < (s + 1) * seq_len)
                    inter = inter + jnp.where(in_seq, _dot(qm, s_scr[s, p]), 0.0)
            o = _dot(scores, vh) + inter * qdec_ref[h]
            gate = rg_ref[:, h * RET_DV:(h + 1) * RET_DV]
            o_ref[:, h * RET_DV:(h + 1) * RET_DV] = jax.nn.silu(gate) * _rmsnorm(o)
        for s in range(n_seq):
            if n_seq == 1:
                kt = kd_t
            else:
                kt = jnp.where((lane >= s * seq_len) & (lane < (s + 1) * seq_len), kd_t, 0.0)
            upd = jnp.where(row < RET_DK, _dot(kt, v_pair[0]), _dot(kt, v_pair[1]))
            s_scr[s, p] = s_scr[s, p] * sdec_ref[p] + upd

    @pl.when(c == pl.num_programs(1) - 1)
    def _():
        sout_ref[...] = s_scr[...]


def _retention_tables(seq_len):
    f32 = np.float32
    log_g = np.log1p(-np.exp2(f32(-5.0) - np.arange(RET_HEADS, dtype=f32))).astype(f32)
    r = np.arange(ROWS)
    li = (r % seq_len).astype(f32)
    sid = r // seq_len
    diff = li[:, None] - li[None, :]
    same = sid[:, None] == sid[None, :]
    dmat = np.where(same[None] & (diff[None] >= 0),
                    np.exp(np.maximum(diff, f32(0.0))[None] * log_g[:, None, None]), f32(0.0)).astype(f32)
    qdec = np.broadcast_to(np.exp((li + f32(1.0))[None, :, None] * log_g[:, None, None]), (RET_HEADS, ROWS, RET_DV))
    lane_head = np.arange(LANES) // RET_DK
    pair_log_g = log_g.reshape(RET_HEADS // 2, 2)[:, lane_head]
    kdec = np.exp((f32(seq_len - 1.0) - li)[None, :, None] * pair_log_g[:, None, :])
    sdec = np.broadcast_to(np.exp(f32(seq_len) * pair_log_g)[:, :, None], (RET_HEADS // 2, LANES, RET_DV))
    return tuple(jnp.asarray(t, F32) for t in (dmat, qdec, kdec, sdec))


def _retention(rq, rk, rv, rg, s0, n_batch, n_tok):
    seq_len = min(ROWS, n_tok)
    n_seq = ROWS // seq_len
    n_chunks = n_tok // seq_len
    tables = _retention_tables(seq_len)
    s0p = s0.reshape(n_batch, RET_HEADS // 2, 2 * RET_DK, RET_DV)
    blk = lambda i, c: (i * n_chunks + c, 0)
    st = lambda i, c: (i, 0, 0, 0)
    c3 = lambda i, c: (0, 0, 0)
    o, s_out = pl.pallas_call(
        functools.partial(_retention_kernel, n_seq=n_seq, seq_len=seq_len),
        grid=(n_batch // n_seq, n_chunks),
        in_specs=[pl.BlockSpec((ROWS, 512), blk), pl.BlockSpec((ROWS, 512), blk),
                  pl.BlockSpec((ROWS, 1024), blk), pl.BlockSpec((ROWS, 1024), blk),
                  pl.BlockSpec((n_seq,) + s0p.shape[1:], st)]
                 + [pl.BlockSpec(t.shape, c3) for t in tables],
        out_specs=[pl.BlockSpec((ROWS, 1024), blk), pl.BlockSpec((n_seq,) + s0p.shape[1:], st)],
        out_shape=[jax.ShapeDtypeStruct((n_batch * n_tok, 1024), F32), jax.ShapeDtypeStruct(s0p.shape, F32)],
        scratch_shapes=[pltpu.VMEM((n_seq,) + s0p.shape[1:], F32)],
        compiler_params=_cparams("parallel", "arbitrary"),
        name="retention",
    )(rq, rk, rv, rg, s0p, *tables)
    return o, s_out.reshape(s0.shape)


def _moba_prompt_kernel(qt_ref, k_ref, vt_ref, kbar_ref, o_ref, sel_scr, qtm_scr, m_scr, l_scr, acc_scr, *, n_blocks):
    qb = pl.program_id(1)
    blk = MOBA_BLOCK
    n_pairs = MOBA_HEADS // 2
    blk_row = _row_iota((n_blocks, blk))
    lo_rows = _row_iota((LANES, blk)) < MOBA_HD
    lo_lanes = _lane_iota((n_blocks, LANES)) < MOBA_HD
    key_le_query = _row_iota((blk, blk)) <= _lane_iota((blk, blk))

    for p in range(n_pairs):
        rows = slice(p * LANES, (p + 1) * LANES)
        qt2 = qt_ref[0, rows, :]
        kb2 = kbar_ref[:, rows]
        for e in range(2):
            h = 2 * p + e
            gate = _dot(jnp.where(lo_lanes if e == 0 else ~lo_lanes, kb2, 0.0), qt2)
            cnt = jnp.zeros((n_blocks, blk), F32)
            for m in range(n_blocks - 1):
                gm = gate[m:m + 1, :]
                ahead = (gm > gate) | ((gm == gate) & (blk_row > m))
                cnt = cnt + jnp.where(ahead, (qb > m).astype(F32), 0.0)
            sel_scr[h] = jnp.where((blk_row < qb) & (cnt < float(MOBA_TOPK)), 1.0, 0.0)
            head_rows = lo_rows if e == 0 else ~lo_rows
            qtm_scr[h] = (jnp.where(head_rows, qt2, 0.0) * (MOBA_HD ** -0.5)).astype(BF16)

    def attend(n, own, span=1):
        c0 = pl.multiple_of(n * blk, blk)
        keys = span * blk
        for p in range(n_pairs):
            rows = slice(p * LANES, (p + 1) * LANES)
            k2 = k_ref[pl.ds(c0, keys), rows]
            for e in range(2):
                h = 2 * p + e
                st = jnp.dot(k2, qtm_scr[h], preferred_element_type=F32)
                if own:
                    st = jnp.where(key_le_query, st, _NEG_INF)
                else:
                    picked = sel_scr[h, pl.ds(n, 1), :]
                    for j in range(1, span):
                        picked = jnp.where(_row_iota((keys, blk)) < j * blk, picked, sel_scr[h, pl.ds(n + j, 1), :])
                    st = jnp.where(picked > 0.0, st, _NEG_INF)
                mx = jnp.max(st, axis=0, keepdims=True)
                if own:
                    m_new = mx
                else:
                    m_old = m_scr[h:h + 1, :]
                    m_new = jnp.maximum(m_old, mx)
                    alpha = jnp.exp(m_old - m_new)
                pt = jnp.exp(st - m_new)
                psum = jnp.sum(pt, axis=0, keepdims=True)
                vt = vt_ref[0, h * MOBA_HD:(h + 1) * MOBA_HD, pl.ds(c0, keys)]
                pv = jnp.dot(vt, pt.astype(BF16), preferred_element_type=F32)
                l_scr[h:h + 1, :] = psum if own else l_scr[h:h + 1, :] * alpha + psum
                acc_scr[h] = pv if own else acc_scr[h] * alpha + pv
                m_scr[h:h + 1, :] = m_new

    attend(qb, True)

    def body(j, c):
        attend(2 * j, False, span=2)
        return c

    lax.fori_loop(0, lax.shift_right_logical(qb, 1), body, 0)

    @pl.when((qb & 1) == 1)
    def _():
        attend(qb - 1, False)
    for p in range(n_pairs):
        pair = [acc_scr[h] / l_scr[h:h + 1, :] for h in (2 * p, 2 * p + 1)]
        o_ref[:, p * LANES:(p + 1) * LANES] = jnp.concatenate(pair, axis=0).T


def _moba_prompt(qt, kb, vtb, kbar, n_batch, n_tok):
    n_blocks = n_tok // MOBA_BLOCK
    width = MOBA_HEADS * MOBA_HD
    return pl.pallas_call(
        functools.partial(_moba_prompt_kernel, n_blocks=n_blocks),
        grid=(n_batch, n_blocks),
        in_specs=[pl.BlockSpec((1, width, MOBA_BLOCK), lambda b, i: (b, 0, i)),
                  pl.BlockSpec((n_tok, width), lambda b, i: (b, 0)),
                  pl.BlockSpec((1, width, n_tok), lambda b, i: (b, 0, 0)),
                  pl.BlockSpec((n_blocks, width), lambda b, i: (b, 0))],
        out_specs=pl.BlockSpec((MOBA_BLOCK, width), lambda b, i: (b * n_blocks + i, 0)),
        out_shape=jax.ShapeDtypeStruct((n_batch * n_tok, width), F32),
        scratch_shapes=[pltpu.VMEM((MOBA_HEADS, n_blocks, MOBA_BLOCK), F32),
                        pltpu.VMEM((MOBA_HEADS, LANES, MOBA_BLOCK), BF16),
                        pltpu.VMEM((MOBA_HEADS, MOBA_BLOCK), F32), pltpu.VMEM((MOBA_HEADS, MOBA_BLOCK), F32),
                        pltpu.VMEM((MOBA_HEADS, MOBA_HD, MOBA_BLOCK), F32)],
        compiler_params=_cparams("parallel", "arbitrary"),
        name="moba_prompt",
    )(qt, kb, vtb, kbar)


PAGES_PER_BLOCK = MOBA_BLOCK // PAGE_SIZE
KBAR_BLOCKS_PER_STEP = 16


def _kbar_kernel(pt_ref, *refs):
    page_refs, o_ref = refs[:-1], refs[-1]
    for j in range(KBAR_BLOCKS_PER_STEP):
        tot = page_refs[j * PAGES_PER_BLOCK][0]
        for half in range(1, PAGES_PER_BLOCK):
            tot = tot + page_refs[j * PAGES_PER_BLOCK + half][0]
        o_ref[0, j] = jnp.sum(tot, axis=-1) * (1.0 / MOBA_BLOCK)


def _moba_kbar(pool_k, page_table):
    n_seq, n_pages = page_table.shape
    n_blocks = n_pages // PAGES_PER_BLOCK
    ppstep = KBAR_BLOCKS_PER_STEP * PAGES_PER_BLOCK

    def page_spec(j):
        return pl.BlockSpec((1, MOBA_HEADS, MOBA_HD, PAGE_SIZE),
                            lambda b, i, pt: (pt[b * n_pages + i * ppstep + j], 0, 0, 0))

    return pl.pallas_call(
        _kbar_kernel,
        grid_spec=pltpu.PrefetchScalarGridSpec(
            num_scalar_prefetch=1,
            grid=(n_seq, n_blocks // KBAR_BLOCKS_PER_STEP),
            in_specs=[page_spec(j) for j in range(ppstep)],
            out_specs=pl.BlockSpec((1, KBAR_BLOCKS_PER_STEP, MOBA_HEADS, MOBA_HD), lambda b, i, pt: (b, i, 0, 0)),
        ),
        out_shape=jax.ShapeDtypeStruct((n_seq, n_blocks, MOBA_HEADS, MOBA_HD), F32),
        compiler_params=_cparams("parallel", "parallel"),
        name="moba_kbar",
    )(page_table.reshape(-1), *([pool_k] * ppstep))


def _moba_select_kernel(q_ref, kbar_ref, o_ref, *, n_blocks, n_tok):
    lane = _lane_iota((n_tok, LANES))
    lane_f = lane.astype(F32)
    pad = jnp.zeros((LANES - n_blocks, LANES), F32)
    for h in range(MOBA_HEADS):
        p, e = divmod(h, 2)
        cols = slice(p * LANES, (p + 1) * LANES)
        q2 = q_ref[:, cols]
        head_lanes = (lane < MOBA_HD) if e == 0 else (lane >= MOBA_HD)
        qm = jnp.where(head_lanes, q2, 0.0)
        kb = jnp.concatenate([kbar_ref[0, :, cols], pad], axis=0)
        gate = jnp.where(lane < n_blocks, _dot_nt(qm, kb), _NEG_INF)
        res = jnp.zeros((n_tok, LANES), F32)
        for j in range(MOBA_TOPK):
            mx = jnp.max(gate, axis=1, keepdims=True)
            idx = jnp.min(jnp.where(gate == mx, lane_f, float(LANES)), axis=1, keepdims=True)
            res = jnp.where(lane == j, idx, res)
            gate = jnp.where(lane_f == idx, _NEG_INF, gate)
        o_ref[0, h * n_tok:(h + 1) * n_tok, :] = res.astype(I32)


def _moba_select(mq, kbar, n_seq, n_tok):
    n_blocks = kbar.shape[1]
    kb = kbar.reshape(n_seq, n_blocks, 512)
    out = pl.pallas_call(
        functools.partial(_moba_select_kernel, n_blocks=n_blocks, n_tok=n_tok),
        grid=(n_seq,),
        in_specs=[pl.BlockSpec((n_tok, 512), lambda b: (b, 0)), pl.BlockSpec((1, n_blocks, 512), lambda b: (b, 0, 0))],
        out_specs=pl.BlockSpec((1, MOBA_HEADS * n_tok, LANES), lambda b: (b, 0, 0)),
        out_shape=jax.ShapeDtypeStruct((n_seq, MOBA_HEADS * n_tok, LANES), I32),
        compiler_params=_cparams("parallel"),
        name="moba_select",
    )(mq, kb)
    return out[:, :, :MOBA_TOPK].reshape(-1)


def _moba_sample_kernel(sel_ref, pt_ref, q_ref, kn_ref, vn_ref, pk_ref, pv_ref, o_ref,
                        kbuf, vbuf, sem, *, n_tok, n_pages):
    s = pl.program_id(0)
    n_steps = pl.num_programs(0)
    n_sel = n_tok * MOBA_TOPK
    past = n_sel * MOBA_BLOCK

    def copies(step, slot):
        b = step // MOBA_HEADS
        h = step % MOBA_HEADS
        out = []
        for i in range(n_sel):
            blk = sel_ref[step * n_sel + i]
            for half in range(PAGES_PER_BLOCK):
                page = pt_ref[b * n_pages + blk * PAGES_PER_BLOCK + half]
                c0 = i * MOBA_BLOCK + half * PAGE_SIZE
                out.append(pltpu.make_async_copy(pk_ref.at[page, h], kbuf.at[slot, :, pl.ds(c0, PAGE_SIZE)], sem.at[slot, 0]))
                out.append(pltpu.make_async_copy(pv_ref.at[page, h], vbuf.at[slot, :, pl.ds(c0, PAGE_SIZE)], sem.at[slot, 1]))
        return out

    slot = s % 2

    def start_all(cps):
        for j, cp in enumerate(cps):
            cp.start(priority=(j // 2) % 2)

    @pl.when(s == 0)
    def _():
        start_all(copies(s, slot))

    @pl.when(s + 1 < n_steps)
    def _():
        start_all(copies(s + 1, 1 - slot))

    for cp in copies(s, slot):
        cp.wait()

    q = q_ref[0, 0] * (MOBA_HD ** -0.5)
    s_past = _dot(q, kbuf[slot])
    key = _lane_iota((n_tok, past))
    first_key = _row_iota((n_tok, past)) * (MOBA_TOPK * MOBA_BLOCK)
    s_past = jnp.where((key >= first_key) & (key < first_key + MOBA_TOPK * MOBA_BLOCK), s_past, _NEG_INF)
    s_own = _dot_nt(q, kn_ref[0, 0])
    s_own = jnp.where(_lane_iota((n_tok, n_tok)) <= _row_iota((n_tok, n_tok)), s_own, _NEG_INF)
    mx = jnp.maximum(jnp.max(s_past, axis=1, keepdims=True), jnp.max(s_own, axis=1, keepdims=True))
    p_past = jnp.exp(s_past - mx)
    p_own = jnp.exp(s_own - mx)
    den = jnp.sum(p_past, axis=1, keepdims=True) + jnp.sum(p_own, axis=1, keepdims=True)
    o = _dot_nt(p_past, vbuf[slot]) + _dot(p_own, vn_ref[0, 0])
    o_ref[0, 0] = o / den


def _moba_sample(mq, mk, mv, pool_k, pool_v, page_table, n_seq, n_tok):
    n_pages = page_table.shape[1]
    kbar = _moba_kbar(pool_k, page_table)
    sel = _moba_select(mq, kbar, n_seq, n_tok)

    def head_major(a):
        return a.reshape(n_seq, n_tok, MOBA_HEADS, MOBA_HD).transpose(0, 2, 1, 3)

    qh, kh, vh = head_major(mq), head_major(mk), head_major(mv)
    n_sel = n_tok * MOBA_TOPK
    spec = pl.BlockSpec((1, 1, n_tok, MOBA_HD), lambda s, sel, pt: (s // MOBA_HEADS, s % MOBA_HEADS, 0, 0))
    anyspec = pl.BlockSpec(memory_space=pl.ANY)
    o = pl.pallas_call(
        functools.partial(_moba_sample_kernel, n_tok=n_tok, n_pages=n_pages),
        grid_spec=pltpu.PrefetchScalarGridSpec(
            num_scalar_prefetch=2,
            grid=(n_seq * MOBA_HEADS,),
            in_specs=[spec, spec, spec, anyspec, anyspec],
            out_specs=spec,
            scratch_shapes=[pltpu.VMEM((2, MOBA_HD, n_sel * MOBA_BLOCK), F32),
                            pltpu.VMEM((2, MOBA_HD, n_sel * MOBA_BLOCK), F32),
                            pltpu.SemaphoreType.DMA((2, 2))],
        ),
        out_shape=jax.ShapeDtypeStruct(qh.shape, F32),
        compiler_params=_cparams("arbitrary"),
        name="moba_sample",
    )(sel, page_table.reshape(-1), qh, kh, vh, pool_k, pool_v)
    return o.transpose(0, 2, 1, 3).reshape(n_seq * n_tok, MOBA_HEADS * MOBA_HD)


def _outproj_kernel(*refs, n_in):
    x_ref, a_refs, w_refs, o_ref = refs[0], refs[1:1 + n_in], refs[1 + n_in:1 + 2 * n_in], refs[-1]
    acc = x_ref[...]
    for a_ref, w_ref in zip(a_refs, w_refs):
        acc = acc + jnp.dot(a_ref[...].astype(BF16), w_ref[...], preferred_element_type=F32)
    o_ref[...] = acc


def _outproj(x, acts, ws, tm):
    n = x.shape[0]
    row = lambda i: (i, 0)
    const = lambda i: (0, 0)
    return pl.pallas_call(
        functools.partial(_outproj_kernel, n_in=len(acts)),
        grid=(n // tm,),
        in_specs=[pl.BlockSpec((tm, D_MODEL), row)] + [pl.BlockSpec((tm, a.shape[1]), row) for a in acts]
                 + [pl.BlockSpec(w.shape, const) for w in ws],
        out_specs=pl.BlockSpec((tm, D_MODEL), row),
        out_shape=jax.ShapeDtypeStruct((n, D_MODEL), F32),
        compiler_params=_cparams("parallel"),
        name="outproj",
    )(x, *acts, *ws)


ROUTE_ROWS = 32


def _router_kernel(h_ref, g_ref, wr_ref, before_ref, route_ref, cnt_ref, cnt_scr):
    i = pl.program_id(0)

    @pl.when(i == 0)
    def _():
        cnt_scr[...] = jnp.zeros(cnt_scr.shape, F32)

    xn = _rmsnorm(h_ref[...], g_ref[...])
    logits = _dot(xn, wr_ref[...]).T[:ROUTE_ROWS, :]
    tm = logits.shape[1]
    lane = _row_iota((ROUTE_ROWS, tm)).astype(F32)
    big = float(LANES)

    def first_argmax(v):
        mx = jnp.max(v, axis=0, keepdims=True)
        return mx, jnp.min(jnp.where(v == mx, lane, big), axis=0, keepdims=True)

    is_group = lane < float(MOE_GROUPS)
    lg = jnp.where(is_group, logits, _NEG_INF)
    mg, gi = first_argmax(lg)
    wg = 1.0 / jnp.sum(jnp.where(is_group, jnp.exp(lg - mg), 0.0), axis=0, keepdims=True)
    e0 = float(MOE_GROUPS) + float(MOE_EPG) * gi
    in_group = (lane >= e0) & (lane < e0 + float(MOE_EPG))
    le = jnp.where(in_group, logits, _NEG_INF)
    m1, i1 = first_argmax(le)
    m2, i2 = first_argmax(jnp.where(lane == i1, _NEG_INF, le))
    r = jnp.exp(m2 - m1)
    g1 = wg / (1.0 + r)
    g2 = wg * r / (1.0 + r)
    a = jnp.minimum(i1, i2) - e0
    b = jnp.maximum(i1, i2) - e0
    ga = jnp.where(i1 < i2, g1, g2)
    gb = jnp.where(i1 < i2, g2, g1)
    pair = a * (7.0 - a) * 0.5 + (b - a - 1.0)
    cls = gi * 6.0 + pair
    onehot = jnp.where(lane == cls, 1.0, 0.0)
    before = jnp.dot(onehot.astype(BF16), before_ref[...], preferred_element_type=F32) + cnt_scr[:, 0:1]
    rank = jnp.sum(onehot * before, axis=0, keepdims=True)
    cnt_scr[...] = cnt_scr[...] + jnp.sum(onehot, axis=1, keepdims=True)
    out_row = _row_iota(route_ref.shape)
    route_ref[...] = jnp.where(out_row == 0, cls, jnp.where(out_row == 1, ga, jnp.where(out_row == 2, gb,
                               jnp.where(out_row == 3, rank, 0.0))))
    cnt_ref[...] = cnt_scr[...]


def _router(h, g, wr, tm):
    n = h.shape[0]
    earlier = jnp.asarray(np.arange(tm)[:, None] < np.arange(tm)[None, :], BF16)
    row = lambda i: (i, 0)
    const = lambda i: (0, 0)
    return pl.pallas_call(
        _router_kernel,
        grid=(n // tm,),
        in_specs=[pl.BlockSpec((tm, D_MODEL), row), pl.BlockSpec((1, D_MODEL), const),
                  pl.BlockSpec((D_MODEL, LANES), const), pl.BlockSpec((tm, tm), const)],
        out_specs=[pl.BlockSpec((8, tm), lambda i: (0, i)), pl.BlockSpec((ROUTE_ROWS, LANES), const)],
        out_shape=[jax.ShapeDtypeStruct((8, n), F32), jax.ShapeDtypeStruct((ROUTE_ROWS, LANES), F32)],
        scratch_shapes=[pltpu.VMEM((ROUTE_ROWS, LANES), F32)],
        compiler_params=_cparams("arbitrary"),
        name="moe_router",
    )(h, g, wr, earlier)


def _ffn_kernel(ea_ref, eb_ref, nv_ref, tok_ref, gates_ref, g_ref, wga_ref, wua_ref, wda_ref, wgb_ref, wub_ref, wdb_ref,
                h_hbm, out_hbm, xbuf, ybuf, row_sized, gsem, ssem):
    i = pl.program_id(0)
    n_steps = pl.num_programs(0)
    slot = i % 2
    rows = xbuf.shape[1]

    def gather_tile(tile, buf_slot):
        for r in range(rows):
            tok = tok_ref[tile * rows + r]
            pltpu.make_async_copy(h_hbm.at[pl.ds(tok, 1), :], xbuf.at[buf_slot, pl.ds(r, 1), :],
                                  gsem.at[buf_slot]).start(priority=r % 2)

    def wait_gather(buf_slot):
        pltpu.make_async_copy(h_hbm.at[pl.ds(0, rows), :], xbuf.at[buf_slot], gsem.at[buf_slot]).wait()

    def scatter_tile(tile, buf_slot):
        def start_row(r, priority):
            tok = tok_ref[tile * rows + r]
            pltpu.make_async_copy(ybuf.at[buf_slot, pl.ds(r, 1), :], out_hbm.at[pl.ds(tok, 1), :],
                                  ssem.at[buf_slot]).start(priority=priority)

        @pl.when(nv_ref[tile] == rows)
        def _():
            for r in range(rows):
                start_row(r, r % 2)

        @pl.when(nv_ref[tile] < rows)
        def _():
            def body(r, c):
                start_row(r, 0)
                return c
            lax.fori_loop(0, nv_ref[tile], body, 0)

    def wait_scatter(tile, buf_slot):
        n = nv_ref[tile]

        @pl.when(n > 0)
        def _():
            pltpu.make_async_copy(row_sized.at[pl.ds(0, n)], row_sized.at[pl.ds(0, n)], ssem.at[buf_slot]).wait()

    @pl.when(i == 0)
    def _():
        gather_tile(0, 0)

    @pl.when((i == 0) | (nv_ref[jnp.maximum(i - 1, 0)] > 0))
    def _():
        wait_gather(slot)

    @pl.when(i >= 2)
    def _():
        wait_scatter(i - 2, slot)

    @pl.when(nv_ref[i] > 0)
    def _():
        gather_tile(i + 1, 1 - slot)
        x = xbuf[slot]
        xb = _rmsnorm(x, g_ref[...]).astype(BF16)
        y = x
        for lane, wg_ref, wu_ref, wd_ref in ((0, wga_ref, wua_ref, wda_ref), (1, wgb_ref, wub_ref, wdb_ref)):
            gate = gates_ref[:, lane:lane + 1]
            hid = jax.nn.silu(_dot(xb, wg_ref[0, 0])) * _dot(xb, wu_ref[0, 0])
            y = y + _dot(hid * gate, wd_ref[0, 0])
        ybuf[slot] = y
        scatter_tile(i, slot)

    @pl.when(i == n_steps - 1)
    def _():
        wait_scatter(i - 1, 1 - slot)


def _ffn(ea, eb, nvalid, tok, gates, h, g, wg, wu, wd, layer, rows):
    n_tiles = tok.shape[0] // rows
    tile = lambda i, *_: (i, 0)
    const = lambda i, *_: (0, 0)
    up = lambda sel: pl.BlockSpec((1, 1, D_MODEL, EXPERT_FF), sel)
    down = lambda sel: pl.BlockSpec((1, 1, EXPERT_FF, D_MODEL), sel)
    sel_a = lambda i, ea, eb, nv, tok: (layer, ea[i], 0, 0)
    sel_b = lambda i, ea, eb, nv, tok: (layer, eb[i], 0, 0)
    anyspec = pl.BlockSpec(memory_space=pl.ANY)
    return pl.pallas_call(
        _ffn_kernel,
        grid_spec=pltpu.PrefetchScalarGridSpec(
            num_scalar_prefetch=4,
            grid=(n_tiles,),
            in_specs=[pl.BlockSpec((rows, 2), tile), pl.BlockSpec((1, D_MODEL), const),
                      up(sel_a), up(sel_a), down(sel_a), up(sel_b), up(sel_b), down(sel_b), anyspec],
            out_specs=anyspec,
            scratch_shapes=[pltpu.VMEM((2, rows, D_MODEL), F32), pltpu.VMEM((2, rows, D_MODEL), F32),
                            pltpu.VMEM((rows, D_MODEL // LANES, LANES), F32),
                            pltpu.SemaphoreType.DMA((2,)), pltpu.SemaphoreType.DMA((2,))],
        ),
        out_shape=jax.ShapeDtypeStruct(h.shape, F32),
        compiler_params=_cparams("arbitrary"),
        name="moe_ffn",
    )(ea, eb, nvalid, tok, gates, g, wg, wu, wd, wg, wu, wd, h)


_PAIR_A = (0, 0, 0, 1, 1, 2)
_PAIR_B = (1, 2, 3, 2, 3, 3)


def _hier_moe(h, g, wr, wg, wu, wd, layer):
    n = h.shape[0]
    tile = MOE_TILE if n >= N_CLASSES * MOE_TILE else MOE_TILE_SMALL
    route, cnt = _router(h, g, wr, TOKEN_TILE)
    cls = route[0].astype(I32)
    rank = route[3].astype(I32)
    counts = cnt[:N_CLASSES, 0].astype(I32)
    tiles_c = (counts + tile - 1) // tile
    tile_end = jnp.cumsum(tiles_c)
    tile_start = tile_end - tiles_c
    class_ids = jnp.arange(N_CLASSES, dtype=I32)[:, None]
    pos = jnp.sum(jnp.where(cls[None, :] == class_ids, (tile_start * tile)[:, None], 0), axis=0) + rank
    n_tiles = n // tile + N_CLASSES
    ti = jnp.arange(n_tiles, dtype=I32)
    total = tile_end[-1]
    tcls = jnp.sum((tile_end[None, :] <= jnp.minimum(ti, total - 1)[:, None]).astype(I32), axis=1)
    tcls = jnp.minimum(tcls, N_CLASSES - 1)
    nvalid = jnp.where(ti < total, jnp.clip(counts[tcls] - (ti - tile_start[tcls]) * tile, 0, tile), 0).astype(I32)
    grp, pair = tcls // 6, tcls % 6
    ea = grp * MOE_EPG + jnp.asarray(_PAIR_A, I32)[pair]
    eb = grp * MOE_EPG + jnp.asarray(_PAIR_B, I32)[pair]
    per_token = jnp.stack([jnp.arange(n, dtype=I32), lax.bitcast_convert_type(route[1], I32),
                           lax.bitcast_convert_type(route[2], I32)], axis=1)
    per_slot = jnp.zeros((n_tiles * tile, 3), I32).at[pos].set(per_token)
    tok = per_slot[:, 0]
    gates = lax.bitcast_convert_type(per_slot[:, 1:3], F32)
    return _ffn(ea, eb, nvalid, tok, gates, h, g, wg, wu, wd, layer, tile)


def _odd_inproj_kernel(x_ref, g_ref, wz_ref, wx_ref, wdt_ref, z_ref, xbc_ref, dt_ref):
    xb = _rmsnorm(x_ref[...], g_ref[...]).astype(BF16)
    z_ref[...] = jnp.dot(xb, wz_ref[...], preferred_element_type=F32)
    xbc_ref[...] = jnp.dot(xb, wx_ref[...], preferred_element_type=F32)
    dt_ref[...] = jnp.dot(xb, wdt_ref[...], preferred_element_type=F32)


def _odd_inproj(x, g, wz, wx, wdt, tm):
    n = x.shape[0]
    row = lambda i: (i, 0)
    const = lambda i: (0, 0)
    widths = (D_INNER, CONV_DIM, LANES)
    return pl.pallas_call(
        _odd_inproj_kernel,
        grid=(n // tm,),
        in_specs=[pl.BlockSpec((tm, D_MODEL), row), pl.BlockSpec((1, D_MODEL), const),
                  pl.BlockSpec(wz.shape, const), pl.BlockSpec(wx.shape, const), pl.BlockSpec(wdt.shape, const)],
        out_specs=[pl.BlockSpec((tm, c), row) for c in widths],
        out_shape=[jax.ShapeDtypeStruct((n, c), F32) for c in widths],
        compiler_params=_cparams("parallel"),
        name="odd_inproj",
    )(x, g, wz, wx, wdt)


CONV_COLS = 512


def _odd_inproj_conv_kernel(x_ref, g_ref, wz_ref, wx_ref, wdt_ref, cs_ref, cw_ref, cb_ref, dtb_ref,
                            z_ref, xc_ref, dts_ref, cso_ref, raw, *, tiles_per_seq):
    i = pl.program_id(0)
    tm = x_ref.shape[0]
    tail = CONV_W - 1
    xb = _rmsnorm(x_ref[...], g_ref[...]).astype(BF16)
    z_ref[...] = jnp.dot(xb, wz_ref[...], preferred_element_type=F32)
    dts_ref[...] = jax.nn.softplus(jnp.dot(xb, wdt_ref[...], preferred_element_type=F32) + dtb_ref[...])

    @pl.when(i % tiles_per_seq == 0)
    def _():
        raw[0:8, :] = jnp.zeros((8, CONV_DIM), F32)
        raw[8 - tail:8, :] = cs_ref[0]

    for j in range(CONV_DIM // CONV_COLS):
        cols = slice(j * CONV_COLS, (j + 1) * CONV_COLS)
        cur = jnp.dot(xb, wx_ref[:, cols], preferred_element_type=F32)
        raw[8:, cols] = cur
        acc = cb_ref[:, cols] + cur * cw_ref[tail:tail + 1, cols]
        for k in range(tail):
            acc = acc + raw[8 - tail + k:8 - tail + k + tm, cols] * cw_ref[k:k + 1, cols]
        xc_ref[:, cols] = jax.nn.silu(acc)
        new_tail = raw[8 + tm - tail:8 + tm, cols]
        cso_ref[0, :, cols] = new_tail
        raw[8 - tail:8, cols] = new_tail


def _odd_inproj_conv(x, g, wz, wx, wdt, conv_state, cw, cb, dtb, tm, n_tok):
    n = x.shape[0]
    tiles_per_seq = n_tok // tm
    row = lambda i: (i, 0)
    const = lambda i: (0, 0)
    st = lambda i: (i // tiles_per_seq, 0, 0)
    widths = (D_INNER, CONV_DIM, LANES)
    return pl.pallas_call(
        functools.partial(_odd_inproj_conv_kernel, tiles_per_seq=tiles_per_seq),
        grid=(n // tm,),
        in_specs=[pl.BlockSpec((tm, D_MODEL), row), pl.BlockSpec((1, D_MODEL), const),
                  pl.BlockSpec(wz.shape, const), pl.BlockSpec(wx.shape, const), pl.BlockSpec(wdt.shape, const),
                  pl.BlockSpec((1, CONV_W - 1, CONV_DIM), st),
                  pl.BlockSpec((CONV_W, CONV_DIM), const), pl.BlockSpec((1, CONV_DIM), const), pl.BlockSpec((1, LANES), const)],
        out_specs=[pl.BlockSpec((tm, c), row) for c in widths] + [pl.BlockSpec((1, CONV_W - 1, CONV_DIM), st)],
        out_shape=[jax.ShapeDtypeStruct((n, c), F32) for c in widths] + [jax.ShapeDtypeStruct(conv_state.shape, F32)],
        scratch_shapes=[pltpu.VMEM((8 + tm, CONV_DIM), F32)],
        compiler_params=_cparams("arbitrary"),
        name="odd_inproj_conv",
    )(x, g, wz, wx, wdt, conv_state, cw, cb, dtb)


def _conv_kernel(xbc_ref, dt_ref, cs_ref, w_ref, b_ref, dtb_ref, xc_ref, dts_ref, cso_ref, carry):
    c = pl.program_id(1)
    tt = xbc_ref.shape[0]
    tail = CONV_W - 1

    @pl.when(c == 0)
    def _():
        carry[...] = jnp.zeros(carry.shape, F32)
        carry[8 - tail:8, :] = cs_ref[0]

    for j in range(CONV_DIM // CONV_COLS):
        cols = slice(j * CONV_COLS, (j + 1) * CONV_COLS)
        x = xbc_ref[:, cols]
        full = jnp.concatenate([carry[:, cols], x], axis=0)
        acc = b_ref[:, cols] + x * w_ref[tail:tail + 1, cols]
        for k in range(tail):
            acc = acc + full[8 - tail + k:8 - tail + k + tt, :] * w_ref[k:k + 1, cols]
        xc_ref[:, cols] = jax.nn.silu(acc)
        new_tail = full[8 + tt - tail:8 + tt, :]
        carry[8 - tail:8, cols] = new_tail
        cso_ref[0, :, cols] = new_tail
    dts_ref[...] = jax.nn.softplus(dt_ref[...] + dtb_ref[...])


def _conv(xbc, dt, conv_state, w, b, dtb, n_batch, n_tok, tt):
    n_chunks = n_tok // tt
    blk = lambda i, c: (i * n_chunks + c, 0)
    const = lambda i, c: (0, 0)
    st = lambda i, c: (i, 0, 0)
    return pl.pallas_call(
        _conv_kernel,
        grid=(n_batch, n_chunks),
        in_specs=[pl.BlockSpec((tt, CONV_DIM), blk), pl.BlockSpec((tt, LANES), blk),
                  pl.BlockSpec((1, CONV_W - 1, CONV_DIM), st),
                  pl.BlockSpec((CONV_W, CONV_DIM), const), pl.BlockSpec((1, CONV_DIM), const), pl.BlockSpec((1, LANES), const)],
        out_specs=[pl.BlockSpec((tt, CONV_DIM), blk), pl.BlockSpec((tt, LANES), blk),
                   pl.BlockSpec((1, CONV_W - 1, CONV_DIM), st)],
        out_shape=[jax.ShapeDtypeStruct(xbc.shape, F32), jax.ShapeDtypeStruct(dt.shape, F32),
                   jax.ShapeDtypeStruct(conv_state.shape, F32)],
        scratch_shapes=[pltpu.VMEM((8, CONV_DIM), F32)],
        compiler_params=_cparams("parallel", "arbitrary"),
        name="ssd_conv",
    )(xbc, dt, conv_state, w, b, dtb)


def _ssd_kernel(xs_ref, bm_ref, cm_ref, z_ref, dt_ref, s0_ref, a_ref, dsk_ref, ng_ref, tril_ref, mask_ref, elast_ref,
                o_ref, sout_ref, s_scr, yz_scr, *, n_seq, seq_len, gps):
    c = pl.program_id(2)

    @pl.when(c == 0)
    def _():
        s_scr[...] = s0_ref[...]

    lane = _lane_iota((ROWS, LANES))
    row = _row_iota((ROWS, LANES))
    lo_half = lane < (LANES // 2)
    dt = dt_ref[...]
    cs = _dot01(tril_ref[...], dt * a_ref[...])
    cs_last = _dot01(elast_ref[...], cs)
    if gps != SSM_GROUPS:
        shift = (LANES - pl.program_id(1) * (gps * SSM_HEADS // SSM_GROUPS)) % LANES
        dt, cs, cs_last = (pltpu.roll(v, shift, 1) for v in (dt, cs, cs_last))
    cs_t = cs.T
    ecs = jnp.exp(cs)
    dt_tail = dt * jnp.exp(cs_last - cs)
    causal = mask_ref[...] > 0.0

    def per_head(v, h0):
        return jnp.where(lo_half, v[:, h0:h0 + 1], v[:, h0 + 1:h0 + 2])

    for g in range(gps):
        gcols = slice(g * D_STATE, (g + 1) * D_STATE)
        cg = cm_ref[:, gcols].astype(BF16)
        bg = bm_ref[:, gcols].astype(BF16)
        cb = _dot_nt(cg, bg)
        for pp in range(2):
            pair = 2 * g + pp
            h0 = 2 * pair
            pcols = slice(pair * LANES, (pair + 1) * LANES)
            x2 = xs_ref[:, pcols]
            xdt = x2 * per_head(dt, h0)
            ys = []
            for e in range(2):
                h = h0 + e
                seg = cs[:, h:h + 1] - cs_t[h:h + 1, :]
                wmat = jnp.exp(jnp.where(causal, seg, _NEG_INF)) * cb
                ys.append(_dot(wmat, xdt))
            y2 = jnp.where(lo_half, ys[0], ys[1])
            if n_seq == 1:
                inter = _dot_nt(cg, s_scr[0, pair])
            else:
                inter = jnp.zeros((ROWS, LANES), F32)
                for s in range(n_seq):
                    in_seq = (row >= s * seq_len) & (row < (s + 1) * seq_len)
                    inter = inter + jnp.where(in_seq, _dot_nt(cg, s_scr[s, pair]), 0.0)
            y2 = y2 + inter * per_head(ecs, h0) + dsk_ref[:, pcols] * x2
            yz_scr[:, pcols] = y2 * jax.nn.silu(z_ref[:, pcols])
            xw_t = (x2 * per_head(dt_tail, h0)).T
            for s in range(n_seq):
                last = (s + 1) * seq_len - 1
                if n_seq == 1:
                    xt = xw_t
                else:
                    xt = jnp.where((lane >= s * seq_len) & (lane <= last), xw_t, 0.0)
                dec = jnp.exp(jnp.where(row < (LANES // 2), cs_t[h0:h0 + 1, last:last + 1], cs_t[h0 + 1:h0 + 2, last:last + 1]))
                s_scr[s, pair] = s_scr[s, pair] * dec + _dot(xt, bg)
    gw = D_INNER // SSM_GROUPS
    for g in range(gps):
        cols = slice(g * gw, (g + 1) * gw)
        o_ref[:, cols] = _rmsnorm(yz_scr[:, cols], ng_ref[:, cols])

    @pl.when(c == pl.num_programs(2) - 1)
    def _():
        sout_ref[...] = s_scr[...]


def _ssd(xc, z, dts, s0, a_row, dsk_row, ng_row, n_batch, n_tok):
    seq_len = min(ROWS, n_tok)
    n_seq = ROWS // seq_len
    n_chunks = n_tok // seq_len
    r = np.arange(ROWS)
    li, sid = r % seq_len, r // seq_len
    mask = ((sid[:, None] == sid[None, :]) & (li[:, None] >= li[None, :]))
    elast = (r[None, :] == (sid * seq_len + seq_len - 1)[:, None])
    n_pairs = SSM_HEADS // 2
    s0p = s0.reshape(n_batch, n_pairs, LANES, D_STATE)
    gps = SSM_GROUPS if n_seq == 1 else 1
    xw = gps * (D_INNER // SSM_GROUPS)
    bw = gps * D_STATE
    rows = lambda i, j, c: i * n_chunks + c
    xblk = lambda i, j, c: (rows(i, j, c), j)
    bblk = lambda i, j, c: (rows(i, j, c), D_INNER // bw + j)
    cblk = lambda i, j, c: (rows(i, j, c), (D_INNER + SSM_GROUPS * D_STATE) // bw + j)
    dtblk = lambda i, j, c: (rows(i, j, c), 0)
    st = lambda i, j, c: (i, j, 0, 0)
    const = lambda i, j, c: (0, 0)
    gconst = lambda i, j, c: (0, j)
    state_block = (n_seq, 2 * gps, LANES, D_STATE)
    o, s_out = pl.pallas_call(
        functools.partial(_ssd_kernel, n_seq=n_seq, seq_len=seq_len, gps=gps),
        grid=(n_batch // n_seq, SSM_GROUPS // gps, n_chunks),
        in_specs=[pl.BlockSpec((ROWS, xw), xblk), pl.BlockSpec((ROWS, bw), bblk), pl.BlockSpec((ROWS, bw), cblk),
                  pl.BlockSpec((ROWS, xw), xblk), pl.BlockSpec((ROWS, LANES), dtblk),
                  pl.BlockSpec(state_block, st),
                  pl.BlockSpec((1, LANES), const), pl.BlockSpec((1, xw), gconst), pl.BlockSpec((1, xw), gconst),
                  pl.BlockSpec((ROWS, ROWS), const), pl.BlockSpec((ROWS, ROWS), const), pl.BlockSpec((ROWS, ROWS), const)],
        out_specs=[pl.BlockSpec((ROWS, xw), xblk), pl.BlockSpec(state_block, st)],
        out_shape=[jax.ShapeDtypeStruct((n_batch * n_tok, D_INNER), F32), jax.ShapeDtypeStruct(s0p.shape, F32)],
        scratch_shapes=[pltpu.VMEM(state_block, F32), pltpu.VMEM((ROWS, xw), F32)],
        compiler_params=_cparams("parallel", "parallel", "arbitrary"),
        name="ssd_scan",
    )(xc, xc, xc, z, dts, s0p, a_row, dsk_row, ng_row, jnp.asarray(mask, BF16), jnp.asarray(mask, F32),
      jnp.asarray(elast, BF16))
    return o, s_out.reshape(s0.shape)


def _rope_tables(pos, min_rows):
    half = MOBA_HD // 2
    inv = jnp.power(ROPE_THETA, -jnp.arange(half, dtype=F32) * 2.0 / MOBA_HD)
    ang = jnp.asarray(pos, F32)[:, None] * inv[None, :]
    cos, sin = jnp.cos(ang), jnp.sin(ang)
    reps = (max(1, min_rows // ang.shape[0]), LANES // MOBA_HD)
    return jnp.tile(jnp.concatenate([cos, cos], axis=1), reps), jnp.tile(jnp.concatenate([-sin, sin], axis=1), reps)


def _pad_cols(w, width):
    return jnp.pad(w, ((0, 0), (0, width - w.shape[1])))


TOKEN_TILE = 256


def _even_layer(h, nb, nt, pos, ret0, cache, norm_g, w_in, w_out, q_g, k_g):
    tm = TOKEN_TILE
    cos, sin = _rope_tables(pos, tm)
    head_tile = lambda v: jnp.tile(v, MOBA_HEADS)[None, :]
    width = MOBA_HEADS * MOBA_HD
    blockdiag = jnp.asarray(np.arange(width)[:, None] // MOBA_HD == np.arange(width)[None, :] // MOBA_HD, BF16)
    outs = _even_inproj(h, norm_g[None, :], w_in.astype(BF16), cos, sin, head_tile(q_g), head_tile(k_g), blockdiag,
                        tm, nt, key_minor=cache is None)
    rq, rk, rv, rg = outs[:4]
    if cache is None:
        qt, kb, kt, vt, vtb, kbar = outs[4:]
        mo = _moba_prompt(qt, kb, vtb, kbar.reshape(-1, width), nb, nt)
        mk, mv = (a.reshape(nb, MOBA_HEADS, MOBA_HD, nt).transpose(0, 3, 1, 2) for a in (kt, vt))
    else:
        mq, mk, mv = outs[4:]
        mo = _moba_sample(mq, mk, mv, *cache, nb, nt)
        mk, mv = (a.reshape(nb, nt, MOBA_HEADS, MOBA_HD) for a in (mk, mv))
    ro, ret_state = _retention(rq, rk, rv, rg, ret0, nb, nt)
    w_out = w_out.astype(BF16)
    h = _outproj(h, (ro, mo), (w_out[:RET_HEADS * RET_DV], w_out[RET_HEADS * RET_DV:]), tm)
    return h, ret_state, mk, mv


def _odd_layer(h, nb, nt, conv0, ssm0, norm_g, w_in, conv_w, conv_b, dt_bias, a_log, d_skip, ssd_norm, w_out):
    tm = TOKEN_TILE
    w_in = w_in.astype(BF16)
    weights = (w_in[:, :D_INNER], w_in[:, D_INNER:D_INNER + CONV_DIM], _pad_cols(w_in[:, D_INNER + CONV_DIM:], LANES))
    conv_params = (conv_w, conv_b[None, :], _pad_cols(dt_bias[None, :], LANES))
    if nt % tm == 0:
        z, xc, dts, conv_state = _odd_inproj_conv(h, norm_g[None, :], *weights, conv0, *conv_params, tm, nt)
    else:
        z, xbc, dt = _odd_inproj(h, norm_g[None, :], *weights, tm)
        xc, dts, conv_state = _conv(xbc, dt, conv0, *conv_params, nb, nt, nt)
    a_row = _pad_cols(-jnp.exp(a_log)[None, :], LANES)
    dsk_row = jnp.repeat(d_skip, D_INNER // SSM_HEADS)[None, :]
    yzn, ssm_state = _ssd(xc, z, dts, ssm0, a_row, dsk_row, ssd_norm[None, :], nb, nt)
    h = _outproj(h, (yzn,), (w_out.astype(BF16),), tm)
    return h, conv_state, ssm_state


def _moe_layer(h, norm_g, w_rg, w_re, w_g, w_u, w_d, layer):
    wr = _pad_cols(jnp.concatenate([w_rg, w_re], axis=1), LANES)
    return _hier_moe(h, norm_g[None, :], wr, w_g, w_u, w_d, layer)


def kernel(x_prompt, x_sample, state_ret, cache_k, cache_v, page_table, state_ssm, state_conv, norm_mix, norm_ffn,
           w_in_even, w_out_even, q_norm, k_norm, w_in_odd, conv_w, conv_b, dt_bias, a_log, d_skip, ssd_norm, w_out_odd,
           w_router_group, w_router_expert, w_expert_gate, w_expert_up, w_expert_down):
    bp, seq, _ = x_prompt.shape
    bs, dec_seq, _ = x_sample.shape
    past_len = page_table.shape[1] * PAGE_SIZE
    groups = ((x_prompt.reshape(bp * seq, D_MODEL), bp, seq, np.arange(seq, dtype=np.float32)),
              (x_sample.reshape(bs * dec_seq, D_MODEL), bs, dec_seq, past_len + np.arange(dec_seq, dtype=np.float32)))

    def moe(h, layer):
        return _moe_layer(h, norm_ffn[layer], w_router_group[layer], w_router_expert[layer],
                          w_expert_gate, w_expert_up, w_expert_down, layer)

    outs = []
    for gi, (h, nb, nt, pos) in enumerate(groups):
        if gi == 0:
            ret0 = jnp.zeros((nb, RET_HEADS, RET_DK, RET_DV), F32)
            cache = None
            conv0 = jnp.zeros((nb, CONV_W - 1, CONV_DIM), F32)
            ssm0 = jnp.zeros((nb, SSM_HEADS, D_INNER // SSM_HEADS, D_STATE), F32)
        else:
            ret0 = state_ret[0]
            cache = (cache_k[0].transpose(0, 2, 3, 1), cache_v[0].transpose(0, 2, 3, 1), page_table)
            conv0, ssm0 = state_conv[0], state_ssm[0]
        h, ret_state, mk, mv = _even_layer(h, nb, nt, pos, ret0, cache, norm_mix[0], w_in_even[0], w_out_even[0],
                                           q_norm[0], k_norm[0])
        h = moe(h, 0)
        h, conv_state, ssm_state = _odd_layer(h, nb, nt, conv0, ssm0, norm_mix[1], w_in_odd[0], conv_w[0], conv_b[0],
                                              dt_bias[0], a_log[0], d_skip[0], ssd_norm[0], w_out_odd[0])
        h = moe(h, 1)
        outs.append(dict(h=h.reshape(nb, nt, D_MODEL), ret=ret_state[None], k=mk[None], v=mv[None],
                         ssm=ssm_state[None], conv=conv_state[None]))
    p, s = outs
    return (p["h"], s["h"], p["ret"], s["ret"], p["k"], p["v"], s["k"], s["v"], p["ssm"], s["ssm"], p["conv"], s["conv"])
```

```python
import functools
import math

import jax
import jax.numpy as jnp
import numpy as np
from jax import lax
from jax.experimental import pallas as pl
from jax.experimental.pallas import tpu as pltpu

F32 = jnp.float32
BF16 = jnp.bfloat16
I32 = jnp.int32

D_MODEL = 1024
PAGE_SIZE = 128
RET_HEADS = 8
RET_DK = 64
RET_DV = 128
MOBA_HEADS = 8
MOBA_HD = 64
MOBA_BLOCK = 256
MOBA_TOPK = 3
ROPE_THETA = 10000.0
D_INNER = 2048
SSM_HEADS = 32
SSM_GROUPS = 8
D_STATE = 128
CONV_W = 4
CONV_DIM = 4096
MOE_GROUPS = 4
MOE_EPG = 4
N_EXPERTS = 16
EXPERT_FF = 512
NORM_EPS = 1e-6
N_CLASSES = MOE_GROUPS * 6
LANES = 128
ROWS = 128
MOE_TILE = 256
MOE_TILE_SMALL = 32
VMEM_LIMIT = 56 * 1024 * 1024

_NEG_INF = float("-inf")


def _cparams(*sem):
    return pltpu.CompilerParams(dimension_semantics=sem, vmem_limit_bytes=VMEM_LIMIT)


def _dot(a, b):
    return jnp.dot(a.astype(BF16), b.astype(BF16), preferred_element_type=F32)


def _dot_nt(a, b):
    return lax.dot_general(a.astype(BF16), b.astype(BF16), (((1,), (1,)), ((), ())), preferred_element_type=F32)


def _split3(x):
    hi = x.astype(BF16)
    r = x - hi.astype(F32)
    mid = r.astype(BF16)
    lo = (r - mid.astype(F32)).astype(BF16)
    return hi, mid, lo


def _dot01(m01, x):
    hi, mid, lo = _split3(x)
    return (jnp.dot(m01, hi, preferred_element_type=F32) + jnp.dot(m01, mid, preferred_element_type=F32)
            + jnp.dot(m01, lo, preferred_element_type=F32))


def _rmsnorm(x, g=None):
    y = x * lax.rsqrt(jnp.mean(x * x, axis=-1, keepdims=True) + NORM_EPS)
    return y if g is None else y * g


def _lane_iota(shape):
    return lax.broadcasted_iota(I32, shape, len(shape) - 1)


def _row_iota(shape):
    return lax.broadcasted_iota(I32, shape, len(shape) - 2)


def _even_inproj_kernel(x_ref, g_ref, w_ref, cos_ref, sin_ref, qg_ref, kg_ref, bd_ref,
                        rq_ref, rk_ref, rv_ref, rg_ref, *moba_refs, key_minor):
    xb = _rmsnorm(x_ref[...], g_ref[...]).astype(BF16)
    cos = cos_ref[...]
    sin = sin_ref[...]
    first_half = (_lane_iota(cos.shape) % MOBA_HD) < (MOBA_HD // 2)

    def proj(c0, c1):
        return jnp.dot(xb, w_ref[:, c0:c1], preferred_element_type=F32)

    def rope(a):
        outs = []
        for j in range(a.shape[1] // LANES):
            s = a[:, j * LANES:(j + 1) * LANES]
            rot = jnp.where(first_half, pltpu.roll(s, LANES - MOBA_HD // 2, 1), pltpu.roll(s, MOBA_HD // 2, 1))
            outs.append(s * cos + rot * sin)
        return jnp.concatenate(outs, axis=1)

    def qk_norm(a, gam):
        ss = jnp.dot((a * a).astype(BF16), bd_ref[...], preferred_element_type=F32)
        return a * lax.rsqrt(ss * (1.0 / MOBA_HD) + NORM_EPS) * gam

    rq_ref[...] = rope(proj(0, 512))
    rk_ref[...] = rope(proj(512, 1024)) * (RET_DK ** -0.5)
    rv_ref[...] = proj(1024, 2048)
    rg_ref[...] = proj(2048, 3072)
    mq = rope(qk_norm(proj(3072, 3584), qg_ref[...]))
    mk = rope(qk_norm(proj(3584, 4096), kg_ref[...]))
    mv = proj(4096, 4608)
    if not key_minor:
        mq_ref, mk_ref, mv_ref = moba_refs
        mq_ref[...] = mq
        mk_ref[...] = mk
        mv_ref[...] = mv
    else:
        qt_ref, kb_ref, kt_ref, vt_ref, vtb_ref, kbar_ref = moba_refs
        qt_ref[0] = mq.T
        kb_ref[...] = mk.astype(BF16)
        kt_ref[0] = mk.T
        vt = mv.T
        vt_ref[0] = vt
        vtb_ref[0] = vt.astype(BF16)
        kbar_ref[0] = jnp.mean(mk, axis=0, keepdims=True)


def _even_inproj(x, g, w, cos, sin, qg, kg, bd, tm, n_tok, key_minor):
    n = x.shape[0]
    n_pos_blocks = cos.shape[0] // tm
    row = lambda i: (i, 0)
    const = lambda i: (0, 0)
    width = MOBA_HEADS * MOBA_HD
    out_specs = [pl.BlockSpec((tm, c), row) for c in (512, 512, 1024, 1024)]
    out_shape = [jax.ShapeDtypeStruct((n, c), F32) for c in (512, 512, 1024, 1024)]
    if not key_minor:
        out_specs += [pl.BlockSpec((tm, width), row)] * 3
        out_shape += [jax.ShapeDtypeStruct((n, width), F32)] * 3
    else:
        assert tm == MOBA_BLOCK and n_tok % tm == 0
        tiles = n_tok // tm
        tspec = pl.BlockSpec((1, width, tm), lambda i: (i // tiles, 0, i % tiles))
        tshape = lambda dt: jax.ShapeDtypeStruct((n // n_tok, width, n_tok), dt)
        out_specs += [tspec, pl.BlockSpec((tm, width), row), tspec, tspec, tspec, pl.BlockSpec((1, 1, width), lambda i: (i, 0, 0))]
        out_shape += [tshape(F32), jax.ShapeDtypeStruct((n, width), BF16), tshape(F32), tshape(F32), tshape(BF16),
                      jax.ShapeDtypeStruct((n // tm, 1, width), F32)]
    return pl.pallas_call(
        functools.partial(_even_inproj_kernel, key_minor=key_minor),
        grid=(n // tm,),
        in_specs=[pl.BlockSpec((tm, D_MODEL), row), pl.BlockSpec((1, D_MODEL), const),
                  pl.BlockSpec(w.shape, const),
                  pl.BlockSpec((tm, LANES), lambda i: (i % n_pos_blocks, 0)),
                  pl.BlockSpec((tm, LANES), lambda i: (i % n_pos_blocks, 0)),
                  pl.BlockSpec((1, 512), const), pl.BlockSpec((1, 512), const), pl.BlockSpec((512, 512), const)],
        out_specs=out_specs,
        out_shape=out_shape,
        compiler_params=_cparams("parallel"),
        name="even_inproj",
    )(x, g, w, cos, sin, qg, kg, bd)


def _retention_kernel(rq_ref, rk_ref, rv_ref, rg_ref, s0_ref, dmat_ref, qdec_ref, kdec_ref, sdec_ref,
                      o_ref, sout_ref, s_scr, *, n_seq, seq_len):
    c = pl.program_id(1)

    @pl.when(c == 0)
    def _():
        s_scr[...] = s0_ref[...]

    lane = _lane_iota((ROWS, LANES))
    row = _row_iota((ROWS, LANES))
    for p in range(RET_HEADS // 2):
        q2 = rq_ref[:, p * LANES:(p + 1) * LANES]
        k2 = rk_ref[:, p * LANES:(p + 1) * LANES]
        kd_t = (k2 * kdec_ref[p]).T
        v_pair = []
        for e in range(2):
            h = 2 * p + e
            head_lanes = (lane < RET_DK) if e == 0 else (lane >= RET_DK)
            qm = jnp.where(head_lanes, q2, 0.0)
            scores = _dot_nt(qm, k2) * dmat_ref[h]
            vh = rv_ref[:, h * RET_DV:(h + 1) * RET_DV]
            v_pair.append(vh)
            if n_seq == 1:
                inter = _dot(qm, s_scr[0, p])
            else:
                inter = jnp.zeros((ROWS, RET_DV), F32)
                for s in range(n_seq):
                    in_seq = (row >= s * seq_len) & (row < (s + 1) * seq_len)
                    inter = inter + jnp.where(in_seq, _dot(qm, s_scr[s, p]), 0.0)
            o = _dot(scores, vh) + inter * qdec_ref[h]
            gate = rg_ref[:, h * RET_DV:(h + 1) * RET_DV]
            o_ref[:, h * RET_DV:(h + 1) * RET_DV] = jax.nn.silu(gate) * _rmsnorm(o)
        for s in range(n_seq):
            if n_seq == 1:
                kt = kd_t
            else:
                kt = jnp.where((lane >= s * seq_len) & (lane < (s + 1) * seq_len), kd_t, 0.0)
            upd = jnp.where(row < RET_DK, _dot(kt, v_pair[0]), _dot(kt, v_pair[1]))
            s_scr[s, p] = s_scr[s, p] * sdec_ref[p] + upd

    @pl.when(c == pl.num_programs(1) - 1)
    def _():
        sout_ref[...] = s_scr[...]


def _retention_tables(seq_len):
    f32 = np.float32
    log_g = np.log1p(-np.exp2(f32(-5.0) - np.arange(RET_HEADS, dtype=f32))).astype(f32)
    r = np.arange(ROWS)
    li = (r % seq_len).astype(f32)
    sid = r // seq_len
    diff = li[:, None] - li[None, :]
    same = sid[:, None] == sid[None, :]
    dmat = np.where(same[None] & (diff[None] >= 0),
                    np.exp(np.maximum(diff, f32(0.0))[None] * log_g[:, None, None]), f32(0.0)).astype(f32)
    qdec = np.broadcast_to(np.exp((li + f32(1.0))[None, :, None] * log_g[:, None, None]), (RET_HEADS, ROWS, RET_DV))
    lane_head = np.arange(LANES) // RET_DK
    pair_log_g = log_g.reshape(RET_HEADS // 2, 2)[:, lane_head]
    kdec = np.exp((f32(seq_len - 1.0) - li)[None, :, None] * pair_log_g[:, None, :])
    sdec = np.broadcast_to(np.exp(f32(seq_len) * pair_log_g)[:, :, None], (RET_HEADS // 2, LANES, RET_DV))
    return tuple(jnp.asarray(t, F32) for t in (dmat, qdec, kdec, sdec))


def _retention(rq, rk, rv, rg, s0, n_batch, n_tok):
    seq_len = min(ROWS, n_tok)
    n_seq = ROWS // seq_len
    n_chunks = n_tok // seq_len
    tables = _retention_tables(seq_len)
    s0p = s0.reshape(n_batch, RET_HEADS // 2, 2 * RET_DK, RET_DV)
    blk = lambda i, c: (i * n_chunks + c, 0)
    st = lambda i, c: (i, 0, 0, 0)
    c3 = lambda i, c: (0, 0, 0)
    o, s_out = pl.pallas_call(
        functools.partial(_retention_kernel, n_seq=n_seq, seq_len=seq_len),
        grid=(n_batch // n_seq, n_chunks),
        in_specs=[pl.BlockSpec((ROWS, 512), blk), pl.BlockSpec((ROWS, 512), blk),
                  pl.BlockSpec((ROWS, 1024), blk), pl.BlockSpec((ROWS, 1024), blk),
                  pl.BlockSpec((n_seq,) + s0p.shape[1:], st)]
                 + [pl.BlockSpec(t.shape, c3) for t in tables],
        out_specs=[pl.BlockSpec((ROWS, 1024), blk), pl.BlockSpec((n_seq,) + s0p.shape[1:], st)],
        out_shape=[jax.ShapeDtypeStruct((n_batch * n_tok, 1024), F32), jax.ShapeDtypeStruct(s0p.shape, F32)],
        scratch_shapes=[pltpu.VMEM((n_seq,) + s0p.shape[1:], F32)],
        compiler_params=_cparams("parallel", "arbitrary"),
        name="retention",
    )(rq, rk, rv, rg, s0p, *tables)
    return o, s_out.reshape(s0.shape)


def _moba_prompt_kernel(qt_ref, k_ref, vt_ref, kbar_ref, o_ref, sel_scr, qtm_scr, m_scr, l_scr, acc_scr, *, n_blocks):
    qb = pl.program_id(1)
    blk = MOBA_BLOCK
    n_pairs = MOBA_HEADS // 2
    blk_row = _row_iota((n_blocks, blk))
    lo_rows = _row_iota((LANES, blk)) < MOBA_HD
    lo_lanes = _lane_iota((n_blocks, LANES)) < MOBA_HD
    key_le_query = _row_iota((blk, blk)) <= _lane_iota((blk, blk))

    for p in range(n_pairs):
        rows = slice(p * LANES, (p + 1) * LANES)
        qt2 = qt_ref[0, rows, :]
        kb2 = kbar_ref[:, rows]
        for e in range(2):
            h = 2 * p + e
            gate = _dot(jnp.where(lo_lanes if e == 0 else ~lo_lanes, kb2, 0.0), qt2)
            cnt = jnp.zeros((n_blocks, blk), F32)
            for m in range(n_blocks - 1):
                gm = gate[m:m + 1, :]
                ahead = (gm > gate) | ((gm == gate) & (blk_row > m))
                cnt = cnt + jnp.where(ahead, (qb > m).astype(F32), 0.0)
            sel_scr[h] = jnp.where((blk_row < qb) & (cnt < float(MOBA_TOPK)), 1.0, 0.0)
            head_rows = lo_rows if e == 0 else ~lo_rows
            qtm_scr[h] = (jnp.where(head_rows, qt2, 0.0) * (MOBA_HD ** -0.5)).astype(BF16)

    def attend(n, own, span=1):
        c0 = pl.multiple_of(n * blk, blk)
        keys = span * blk
        for p in range(n_pairs):
            rows = slice(p * LANES, (p + 1) * LANES)
            k2 = k_ref[pl.ds(c0, keys), rows]
            for e in range(2):
                h = 2 * p + e
                st = jnp.dot(k2, qtm_scr[h], preferred_element_type=F32)
                if own:
                    st = jnp.where(key_le_query, st, _NEG_INF)
                else:
                    picked = sel_scr[h, pl.ds(n, 1), :]
                    for j in range(1, span):
                        picked = jnp.where(_row_iota((keys, blk)) < j * blk, picked, sel_scr[h, pl.ds(n + j, 1), :])
                    st = jnp.where(picked > 0.0, st, _NEG_INF)
                mx = jnp.max(st, axis=0, keepdims=True)
                if own:
                    m_new = mx
                else:
                    m_old = m_scr[h:h + 1, :]
                    m_new = jnp.maximum(m_old, mx)
                    alpha = jnp.exp(m_old - m_new)
                pt = jnp.exp(st - m_new)
                psum = jnp.sum(pt, axis=0, keepdims=True)
                vt = vt_ref[0, h * MOBA_HD:(h + 1) * MOBA_HD, pl.ds(c0, keys)]
                pv = jnp.dot(vt, pt.astype(BF16), preferred_element_type=F32)
                l_scr[h:h + 1, :] = psum if own else l_scr[h:h + 1, :] * alpha + psum
                acc_scr[h] = pv if own else acc_scr[h] * alpha + pv
                m_scr[h:h + 1, :] = m_new

    attend(qb, True)

    def body(j, c):
        attend(2 * j, False, span=2)
        return c

    lax.fori_loop(0, lax.shift_right_logical(qb, 1), body, 0)

    @pl.when((qb & 1) == 1)
    def _():
        attend(qb - 1, False)
    for p in range(n_pairs):
        pair = [acc_scr[h] / l_scr[h:h + 1, :] for h in (2 * p, 2 * p + 1)]
        o_ref[:, p * LANES:(p + 1) * LANES] = jnp.concatenate(pair, axis=0).T


def _moba_prompt(qt, kb, vtb, kbar, n_batch, n_tok):
    n_blocks = n_tok // MOBA_BLOCK
    width = MOBA_HEADS * MOBA_HD
    return pl.pallas_call(
        functools.partial(_moba_prompt_kernel, n_blocks=n_blocks),
        grid=(n_batch, n_blocks),
        in_specs=[pl.BlockSpec((1, width, MOBA_BLOCK), lambda b, i: (b, 0, i)),
                  pl.BlockSpec((n_tok, width), lambda b, i: (b, 0)),
                  pl.BlockSpec((1, width, n_tok), lambda b, i: (b, 0, 0)),
                  pl.BlockSpec((n_blocks, width), lambda b, i: (b, 0))],
        out_specs=pl.BlockSpec((MOBA_BLOCK, width), lambda b, i: (b * n_blocks + i, 0)),
        out_shape=jax.ShapeDtypeStruct((n_batch * n_tok, width), F32),
        scratch_shapes=[pltpu.VMEM((MOBA_HEADS, n_blocks, MOBA_BLOCK), F32),
                        pltpu.VMEM((MOBA_HEADS, LANES, MOBA_BLOCK), BF16),
                        pltpu.VMEM((MOBA_HEADS, MOBA_BLOCK), F32), pltpu.VMEM((MOBA_HEADS, MOBA_BLOCK), F32),
                        pltpu.VMEM((MOBA_HEADS, MOBA_HD, MOBA_BLOCK), F32)],
        compiler_params=_cparams("parallel", "arbitrary"),
        name="moba_prompt",
    )(qt, kb, vtb, kbar)


PAGES_PER_BLOCK = MOBA_BLOCK // PAGE_SIZE
KBAR_BLOCKS_PER_STEP = 16


def _kbar_kernel(pt_ref, *refs):
    page_refs, o_ref = refs[:-1], refs[-1]
    for j in range(KBAR_BLOCKS_PER_STEP):
        tot = page_refs[j * PAGES_PER_BLOCK][0]
        for half in range(1, PAGES_PER_BLOCK):
            tot = tot + page_refs[j * PAGES_PER_BLOCK + half][0]
        o_ref[0, j] = jnp.sum(tot, axis=-1) * (1.0 / MOBA_BLOCK)


def _moba_kbar(pool_k, page_table):
    n_seq, n_pages = page_table.shape
    n_blocks = n_pages // PAGES_PER_BLOCK
    ppstep = KBAR_BLOCKS_PER_STEP * PAGES_PER_BLOCK

    def page_spec(j):
        return pl.BlockSpec((1, MOBA_HEADS, MOBA_HD, PAGE_SIZE),
                            lambda b, i, pt: (pt[b * n_pages + i * ppstep + j], 0, 0, 0))

    return pl.pallas_call(
        _kbar_kernel,
        grid_spec=pltpu.PrefetchScalarGridSpec(
            num_scalar_prefetch=1,
            grid=(n_seq, n_blocks // KBAR_BLOCKS_PER_STEP),
            in_specs=[page_spec(j) for j in range(ppstep)],
            out_specs=pl.BlockSpec((1, KBAR_BLOCKS_PER_STEP, MOBA_HEADS, MOBA_HD), lambda b, i, pt: (b, i, 0, 0)),
        ),
        out_shape=jax.ShapeDtypeStruct((n_seq, n_blocks, MOBA_HEADS, MOBA_HD), F32),
        compiler_params=_cparams("parallel", "parallel"),
        name="moba_kbar",
    )(page_table.reshape(-1), *([pool_k] * ppstep))


def _moba_select_kernel(q_ref, kbar_ref, o_ref, *, n_blocks, n_tok):
    lane = _lane_iota((n_tok, LANES))
    lane_f = lane.astype(F32)
    pad = jnp.zeros((LANES - n_blocks, LANES), F32)
    for h in range(MOBA_HEADS):
        p, e = divmod(h, 2)
        cols = slice(p * LANES, (p + 1) * LANES)
        q2 = q_ref[:, cols]
        head_lanes = (lane < MOBA_HD) if e == 0 else (lane >= MOBA_HD)
        qm = jnp.where(head_lanes, q2, 0.0)
        kb = jnp.concatenate([kbar_ref[0, :, cols], pad], axis=0)
        gate = jnp.where(lane < n_blocks, _dot_nt(qm, kb), _NEG_INF)
        res = jnp.zeros((n_tok, LANES), F32)
        for j in range(MOBA_TOPK):
            mx = jnp.max(gate, axis=1, keepdims=True)
            idx = jnp.min(jnp.where(gate == mx, lane_f, float(LANES)), axis=1, keepdims=True)
            res = jnp.where(lane == j, idx, res)
            gate = jnp.where(lane_f == idx, _NEG_INF, gate)
        o_ref[0, h * n_tok:(h + 1) * n_tok, :] = res.astype(I32)


def _moba_select(mq, kbar, n_seq, n_tok):
    n_blocks = kbar.shape[1]
    kb = kbar.reshape(n_seq, n_blocks, 512)
    out = pl.pallas_call(
        functools.partial(_moba_select_kernel, n_blocks=n_blocks, n_tok=n_tok),
        grid=(n_seq,),
        in_specs=[pl.BlockSpec((n_tok, 512), lambda b: (b, 0)), pl.BlockSpec((1, n_blocks, 512), lambda b: (b, 0, 0))],
        out_specs=pl.BlockSpec((1, MOBA_HEADS * n_tok, LANES), lambda b: (b, 0, 0)),
        out_shape=jax.ShapeDtypeStruct((n_seq, MOBA_HEADS * n_tok, LANES), I32),
        compiler_params=_cparams("parallel"),
        name="moba_select",
    )(mq, kb)
    return out[:, :, :MOBA_TOPK].reshape(-1)


def _moba_sample_kernel(sel_ref, pt_ref, q_ref, kn_ref, vn_ref, pk_ref, pv_ref, o_ref,
                        kbuf, vbuf, sem, *, n_tok, n_pages):
    s = pl.program_id(0)
    n_steps = pl.num_programs(0)
    n_sel = n_tok * MOBA_TOPK
    past = n_sel * MOBA_BLOCK

    def copies(step, slot):
        b = step // MOBA_HEADS
        h = step % MOBA_HEADS
        out = []
        for i in range(n_sel):
            blk = sel_ref[step * n_sel + i]
            for half in range(PAGES_PER_BLOCK):
                page = pt_ref[b * n_pages + blk * PAGES_PER_BLOCK + half]
                c0 = i * MOBA_BLOCK + half * PAGE_SIZE
                out.append(pltpu.make_async_copy(pk_ref.at[page, h], kbuf.at[slot, :, pl.ds(c0, PAGE_SIZE)], sem.at[slot, 0]))
                out.append(pltpu.make_async_copy(pv_ref.at[page, h], vbuf.at[slot, :, pl.ds(c0, PAGE_SIZE)], sem.at[slot, 1]))
        return out

    slot = s % 2

    def start_all(cps):
        for j, cp in enumerate(cps):
            cp.start(priority=(j // 2) % 2)

    @pl.when(s == 0)
    def _():
        start_all(copies(s, slot))

    @pl.when(s + 1 < n_steps)
    def _():
        start_all(copies(s + 1, 1 - slot))

    for cp in copies(s, slot):
        cp.wait()

    q = q_ref[0, 0] * (MOBA_HD ** -0.5)
    s_past = _dot(q, kbuf[slot])
    key = _lane_iota((n_tok, past))
    first_key = _row_iota((n_tok, past)) * (MOBA_TOPK * MOBA_BLOCK)
    s_past = jnp.where((key >= first_key) & (key < first_key + MOBA_TOPK * MOBA_BLOCK), s_past, _NEG_INF)
    s_own = _dot_nt(q, kn_ref[0, 0])
    s_own = jnp.where(_lane_iota((n_tok, n_tok)) <= _row_iota((n_tok, n_tok)), s_own, _NEG_INF)
    mx = jnp.maximum(jnp.max(s_past, axis=1, keepdims=True), jnp.max(s_own, axis=1, keepdims=True))
    p_past = jnp.exp(s_past - mx)
    p_own = jnp.exp(s_own - mx)
    den = jnp.sum(p_past, axis=1, keepdims=True) + jnp.sum(p_own, axis=1, keepdims=True)
    o = _dot_nt(p_past, vbuf[slot]) + _dot(p_own, vn_ref[0, 0])
    o_ref[0, 0] = o / den


def _moba_sample(mq, mk, mv, pool_k, pool_v, page_table, n_seq, n_tok):
    n_pages = page_table.shape[1]
    kbar = _moba_kbar(pool_k, page_table)
    sel = _moba_select(mq, kbar, n_seq, n_tok)

    def head_major(a):
        return a.reshape(n_seq, n_tok, MOBA_HEADS, MOBA_HD).transpose(0, 2, 1, 3)

    qh, kh, vh = head_major(mq), head_major(mk), head_major(mv)
    n_sel = n_tok * MOBA_TOPK
    spec = pl.BlockSpec((1, 1, n_tok, MOBA_HD), lambda s, sel, pt: (s // MOBA_HEADS, s % MOBA_HEADS, 0, 0))
    anyspec = pl.BlockSpec(memory_space=pl.ANY)
    o = pl.pallas_call(
        functools.partial(_moba_sample_kernel, n_tok=n_tok, n_pages=n_pages),
        grid_spec=pltpu.PrefetchScalarGridSpec(
            num_scalar_prefetch=2,
            grid=(n_seq * MOBA_HEADS,),
            in_specs=[spec, spec, spec, anyspec, anyspec],
            out_specs=spec,
            scratch_shapes=[pltpu.VMEM((2, MOBA_HD, n_sel * MOBA_BLOCK), F32),
                            pltpu.VMEM((2, MOBA_HD, n_sel * MOBA_BLOCK), F32),
                            pltpu.SemaphoreType.DMA((2, 2))],
        ),
        out_shape=jax.ShapeDtypeStruct(qh.shape, F32),
        compiler_params=_cparams("arbitrary"),
        name="moba_sample",
    )(sel, page_table.reshape(-1), qh, kh, vh, pool_k, pool_v)
    return o.transpose(0, 2, 1, 3).reshape(n_seq * n_tok, MOBA_HEADS * MOBA_HD)


def _outproj_kernel(*refs, n_in):
    x_ref, a_refs, w_refs, o_ref = refs[0], refs[1:1 + n_in], refs[1 + n_in:1 + 2 * n_in], refs[-1]
    acc = x_ref[...]
    for a_ref, w_ref in zip(a_refs, w_refs):
        acc = acc + jnp.dot(a_ref[...].astype(BF16), w_ref[...], preferred_element_type=F32)
    o_ref[...] = acc


def _outproj(x, acts, ws, tm):
    n = x.shape[0]
    row = lambda i: (i, 0)
    const = lambda i: (0, 0)
    return pl.pallas_call(
        functools.partial(_outproj_kernel, n_in=len(acts)),
        grid=(n // tm,),
        in_specs=[pl.BlockSpec((tm, D_MODEL), row)] + [pl.BlockSpec((tm, a.shape[1]), row) for a in acts]
                 + [pl.BlockSpec(w.shape, const) for w in ws],
        out_specs=pl.BlockSpec((tm, D_MODEL), row),
        out_shape=jax.ShapeDtypeStruct((n, D_MODEL), F32),
        compiler_params=_cparams("parallel"),
        name="outproj",
    )(x, *acts, *ws)


ROUTE_ROWS = 32


def _router_kernel(h_ref, g_ref, wr_ref, before_ref, route_ref, cnt_ref, cnt_scr):
    i = pl.program_id(0)

    @pl.when(i == 0)
    def _():
        cnt_scr[...] = jnp.zeros(cnt_scr.shape, F32)

    xn = _rmsnorm(h_ref[...], g_ref[...])
    logits = _dot(xn, wr_ref[...]).T[:ROUTE_ROWS, :]
    tm = logits.shape[1]
    lane = _row_iota((ROUTE_ROWS, tm)).astype(F32)
    big = float(LANES)

    def first_argmax(v):
        mx = jnp.max(v, axis=0, keepdims=True)
        return mx, jnp.min(jnp.where(v == mx, lane, big), axis=0, keepdims=True)

    is_group = lane < float(MOE_GROUPS)
    lg = jnp.where(is_group, logits, _NEG_INF)
    mg, gi = first_argmax(lg)
    wg = 1.0 / jnp.sum(jnp.where(is_group, jnp.exp(lg - mg), 0.0), axis=0, keepdims=True)
    e0 = float(MOE_GROUPS) + float(MOE_EPG) * gi
    in_group = (lane >= e0) & (lane < e0 + float(MOE_EPG))
    le = jnp.where(in_group, logits, _NEG_INF)
    m1, i1 = first_argmax(le)
    m2, i2 = first_argmax(jnp.where(lane == i1, _NEG_INF, le))
    r = jnp.exp(m2 - m1)
    g1 = wg / (1.0 + r)
    g2 = wg * r / (1.0 + r)
    a = jnp.minimum(i1, i2) - e0
    b = jnp.maximum(i1, i2) - e0
    ga = jnp.where(i1 < i2, g1, g2)
    gb = jnp.where(i1 < i2, g2, g1)
    pair = a * (7.0 - a) * 0.5 + (b - a - 1.0)
    cls = gi * 6.0 + pair
    onehot = jnp.where(lane == cls, 1.0, 0.0)
    before = jnp.dot(onehot.astype(BF16), before_ref[...], preferred_element_type=F32) + cnt_scr[:, 0:1]
    rank = jnp.sum(onehot * before, axis=0, keepdims=True)
    cnt_scr[...] = cnt_scr[...] + jnp.sum(onehot, axis=1, keepdims=True)
    out_row = _row_iota(route_ref.shape)
    route_ref[...] = jnp.where(out_row == 0, cls, jnp.where(out_row == 1, ga, jnp.where(out_row == 2, gb,
                               jnp.where(out_row == 3, rank, 0.0))))
    cnt_ref[...] = cnt_scr[...]


def _router(h, g, wr, tm):
    n = h.shape[0]
    earlier = jnp.asarray(np.arange(tm)[:, None] < np.arange(tm)[None, :], BF16)
    row = lambda i: (i, 0)
    const = lambda i: (0, 0)
    return pl.pallas_call(
        _router_kernel,
        grid=(n // tm,),
        in_specs=[pl.BlockSpec((tm, D_MODEL), row), pl.BlockSpec((1, D_MODEL), const),
                  pl.BlockSpec((D_MODEL, LANES), const), pl.BlockSpec((tm, tm), const)],
        out_specs=[pl.BlockSpec((8, tm), lambda i: (0, i)), pl.BlockSpec((ROUTE_ROWS, LANES), const)],
        out_shape=[jax.ShapeDtypeStruct((8, n), F32), jax.ShapeDtypeStruct((ROUTE_ROWS, LANES), F32)],
        scratch_shapes=[pltpu.VMEM((ROUTE_ROWS, LANES), F32)],
        compiler_params=_cparams("arbitrary"),
        name="moe_router",
    )(h, g, wr, earlier)


def _ffn_kernel(ea_ref, eb_ref, nv_ref, tok_ref, gates_ref, g_ref, wga_ref, wua_ref, wda_ref, wgb_ref, wub_ref, wdb_ref,
                h_hbm, out_hbm, xbuf, ybuf, row_sized, gsem, ssem):
    i = pl.program_id(0)
    n_steps = pl.num_programs(0)
    slot = i % 2
    rows = xbuf.shape[1]

    def gather_tile(tile, buf_slot):
        for r in range(rows):
            tok = tok_ref[tile * rows + r]
            pltpu.make_async_copy(h_hbm.at[pl.ds(tok, 1), :], xbuf.at[buf_slot, pl.ds(r, 1), :],
                                  gsem.at[buf_slot]).start(priority=r % 2)

    def wait_gather(buf_slot):
        pltpu.make_async_copy(h_hbm.at[pl.ds(0, rows), :], xbuf.at[buf_slot], gsem.at[buf_slot]).wait()

    def scatter_tile(tile, buf_slot):
        def start_row(r, priority):
            tok = tok_ref[tile * rows + r]
            pltpu.make_async_copy(ybuf.at[buf_slot, pl.ds(r, 1), :], out_hbm.at[pl.ds(tok, 1), :],
                                  ssem.at[buf_slot]).start(priority=priority)

        @pl.when(nv_ref[tile] == rows)
        def _():
            for r in range(rows):
                start_row(r, r % 2)

        @pl.when(nv_ref[tile] < rows)
        def _():
            def body(r, c):
                start_row(r, 0)
                return c
            lax.fori_loop(0, nv_ref[tile], body, 0)

    def wait_scatter(tile, buf_slot):
        n = nv_ref[tile]

        @pl.when(n > 0)
        def _():
            pltpu.make_async_copy(row_sized.at[pl.ds(0, n)], row_sized.at[pl.ds(0, n)], ssem.at[buf_slot]).wait()

    @pl.when(i == 0)
    def _():
        gather_tile(0, 0)

    @pl.when((i == 0) | (nv_ref[jnp.maximum(i - 1, 0)] > 0))
    def _():
        wait_gather(slot)

    @pl.when(i >= 2)
    def _():
        wait_scatter(i - 2, slot)

    @pl.when(nv_ref[i] > 0)
    def _():
        gather_tile(i + 1, 1 - slot)
        x = xbuf[slot]
        xb = _rmsnorm(x, g_ref[...]).astype(BF16)
        y = x
        for lane, wg_ref, wu_ref, wd_ref in ((0, wga_ref, wua_ref, wda_ref), (1, wgb_ref, wub_ref, wdb_ref)):
            gate = gates_ref[:, lane:lane + 1]
            hid = jax.nn.silu(_dot(xb, wg_ref[0, 0])) * _dot(xb, wu_ref[0, 0])
            y = y + _dot(hid * gate, wd_ref[0, 0])
        ybuf[slot] = y
        scatter_tile(i, slot)

    @pl.when(i == n_steps - 1)
    def _():
        wait_scatter(i - 1, 1 - slot)


def _ffn(ea, eb, nvalid, tok, gates, h, g, wg, wu, wd, layer, rows):
    n_tiles = tok.shape[0] // rows
    tile = lambda i, *_: (i, 0)
    const = lambda i, *_: (0, 0)
    up = lambda sel: pl.BlockSpec((1, 1, D_MODEL, EXPERT_FF), sel)
    down = lambda sel: pl.BlockSpec((1, 1, EXPERT_FF, D_MODEL), sel)
    sel_a = lambda i, ea, eb, nv, tok: (layer, ea[i], 0, 0)
    sel_b = lambda i, ea, eb, nv, tok: (layer, eb[i], 0, 0)
    anyspec = pl.BlockSpec(memory_space=pl.ANY)
    return pl.pallas_call(
        _ffn_kernel,
        grid_spec=pltpu.PrefetchScalarGridSpec(
            num_scalar_prefetch=4,
            grid=(n_tiles,),
            in_specs=[pl.BlockSpec((rows, 2), tile), pl.BlockSpec((1, D_MODEL), const),
                      up(sel_a), up(sel_a), down(sel_a), up(sel_b), up(sel_b), down(sel_b), anyspec],
            out_specs=anyspec,
            scratch_shapes=[pltpu.VMEM((2, rows, D_MODEL), F32), pltpu.VMEM((2, rows, D_MODEL), F32),
                            pltpu.VMEM((rows, D_MODEL // LANES, LANES), F32),
                            pltpu.SemaphoreType.DMA((2,)), pltpu.SemaphoreType.DMA((2,))],
        ),
        out_shape=jax.ShapeDtypeStruct(h.shape, F32),
        compiler_params=_cparams("arbitrary"),
        name="moe_ffn",
    )(ea, eb, nvalid, tok, gates, g, wg, wu, wd, wg, wu, wd, h)


_PAIR_A = (0, 0, 0, 1, 1, 2)
_PAIR_B = (1, 2, 3, 2, 3, 3)


def _hier_moe(h, g, wr, wg, wu, wd, layer):
    n = h.shape[0]
    tile = MOE_TILE if n >= N_CLASSES * MOE_TILE else MOE_TILE_SMALL
    route, cnt = _router(h, g, wr, TOKEN_TILE)
    cls = route[0].astype(I32)
    rank = route[3].astype(I32)
    counts = cnt[:N_CLASSES, 0].astype(I32)
    tiles_c = (counts + tile - 1) // tile
    tile_end = jnp.cumsum(tiles_c)
    tile_start = tile_end - tiles_c
    class_ids = jnp.arange(N_CLASSES, dtype=I32)[:, None]
    pos = jnp.sum(jnp.where(cls[None, :] == class_ids, (tile_start * tile)[:, None], 0), axis=0) + rank
    n_tiles = n // tile + N_CLASSES
    ti = jnp.arange(n_tiles, dtype=I32)
    total = tile_end[-1]
    tcls = jnp.sum((tile_end[None, :] <= jnp.minimum(ti, total - 1)[:, None]).astype(I32), axis=1)
    tcls = jnp.minimum(tcls, N_CLASSES - 1)
    nvalid = jnp.where(ti < total, jnp.clip(counts[tcls] - (ti - tile_start[tcls]) * tile, 0, tile), 0).astype(I32)
    grp, pair = tcls // 6, tcls % 6
    ea = grp * MOE_EPG + jnp.asarray(_PAIR_A, I32)[pair]
    eb = grp * MOE_EPG + jnp.asarray(_PAIR_B, I32)[pair]
    per_token = jnp.stack([jnp.arange(n, dtype=I32), lax.bitcast_convert_type(route[1], I32),
                           lax.bitcast_convert_type(route[2], I32)], axis=1)
    per_slot = jnp.zeros((n_tiles * tile, 3), I32).at[pos].set(per_token)
    tok = per_slot[:, 0]
    gates = lax.bitcast_convert_type(per_slot[:, 1:3], F32)
    return _ffn(ea, eb, nvalid, tok, gates, h, g, wg, wu, wd, layer, tile)


def _odd_inproj_kernel(x_ref, g_ref, wz_ref, wx_ref, wdt_ref, z_ref, xbc_ref, dt_ref):
    xb = _rmsnorm(x_ref[...], g_ref[...]).astype(BF16)
    z_ref[...] = jnp.dot(xb, wz_ref[...], preferred_element_type=F32)
    xbc_ref[...] = jnp.dot(xb, wx_ref[...], preferred_element_type=F32)
    dt_ref[...] = jnp.dot(xb, wdt_ref[...], preferred_element_type=F32)


def _odd_inproj(x, g, wz, wx, wdt, tm):
    n = x.shape[0]
    row = lambda i: (i, 0)
    const = lambda i: (0, 0)
    widths = (D_INNER, CONV_DIM, LANES)
    return pl.pallas_call(
        _odd_inproj_kernel,
        grid=(n // tm,),
        in_specs=[pl.BlockSpec((tm, D_MODEL), row), pl.BlockSpec((1, D_MODEL), const),
                  pl.BlockSpec(wz.shape, const), pl.BlockSpec(wx.shape, const), pl.BlockSpec(wdt.shape, const)],
        out_specs=[pl.BlockSpec((tm, c), row) for c in widths],
        out_shape=[jax.ShapeDtypeStruct((n, c), F32) for c in widths],
        compiler_params=_cparams("parallel"),
        name="odd_inproj",
    )(x, g, wz, wx, wdt)


CONV_COLS = 512


def _odd_inproj_conv_kernel(x_ref, g_ref, wz_ref, wx_ref, wdt_ref, cs_ref, cw_ref, cb_ref, dtb_ref,
                            z_ref, xc_ref, dts_ref, cso_ref, raw, *, tiles_per_seq):
    i = pl.program_id(0)
    tm = x_ref.shape[0]
    tail = CONV_W - 1
    xb = _rmsnorm(x_ref[...], g_ref[...]).astype(BF16)
    z_ref[...] = jnp.dot(xb, wz_ref[...], preferred_element_type=F32)
    dts_ref[...] = jax.nn.softplus(jnp.dot(xb, wdt_ref[...], preferred_element_type=F32) + dtb_ref[...])

    @pl.when(i % tiles_per_seq == 0)
    def _():
        raw[0:8, :] = jnp.zeros((8, CONV_DIM), F32)
        raw[8 - tail:8, :] = cs_ref[0]

    for j in range(CONV_DIM // CONV_COLS):
        cols = slice(j * CONV_COLS, (j + 1) * CONV_COLS)
        cur = jnp.dot(xb, wx_ref[:, cols], preferred_element_type=F32)
        raw[8:, cols] = cur
        acc = cb_ref[:, cols] + cur * cw_ref[tail:tail + 1, cols]
        for k in range(tail):
            acc = acc + raw[8 - tail + k:8 - tail + k + tm, cols] * cw_ref[k:k + 1, cols]
        xc_ref[:, cols] = jax.nn.silu(acc)
        new_tail = raw[8 + tm - tail:8 + tm, cols]
        cso_ref[0, :, cols] = new_tail
        raw[8 - tail:8, cols] = new_tail


def _odd_inproj_conv(x, g, wz, wx, wdt, conv_state, cw, cb, dtb, tm, n_tok):
    n = x.shape[0]
    tiles_per_seq = n_tok // tm
    row = lambda i: (i, 0)
    const = lambda i: (0, 0)
    st = lambda i: (i // tiles_per_seq, 0, 0)
    widths = (D_INNER, CONV_DIM, LANES)
    return pl.pallas_call(
        functools.partial(_odd_inproj_conv_kernel, tiles_per_seq=tiles_per_seq),
        grid=(n // tm,),
        in_specs=[pl.BlockSpec((tm, D_MODEL), row), pl.BlockSpec((1, D_MODEL), const),
                  pl.BlockSpec(wz.shape, const), pl.BlockSpec(wx.shape, const), pl.BlockSpec(wdt.shape, const),
                  pl.BlockSpec((1, CONV_W - 1, CONV_DIM), st),
                  pl.BlockSpec((CONV_W, CONV_DIM), const), pl.BlockSpec((1, CONV_DIM), const), pl.BlockSpec((1, LANES), const)],
        out_specs=[pl.BlockSpec((tm, c), row) for c in widths] + [pl.BlockSpec((1, CONV_W - 1, CONV_DIM), st)],
        out_shape=[jax.ShapeDtypeStruct((n, c), F32) for c in widths] + [jax.ShapeDtypeStruct(conv_state.shape, F32)],
        scratch_shapes=[pltpu.VMEM((8 + tm, CONV_DIM), F32)],
        compiler_params=_cparams("arbitrary"),
        name="odd_inproj_conv",
    )(x, g, wz, wx, wdt, conv_state, cw, cb, dtb)


def _conv_kernel(xbc_ref, dt_ref, cs_ref, w_ref, b_ref, dtb_ref, xc_ref, dts_ref, cso_ref, carry):
    c = pl.program_id(1)
    tt = xbc_ref.shape[0]
    tail = CONV_W - 1

    @pl.when(c == 0)
    def _():
        carry[...] = jnp.zeros(carry.shape, F32)
        carry[8 - tail:8, :] = cs_ref[0]

    for j in range(CONV_DIM // CONV_COLS):
        cols = slice(j * CONV_COLS, (j + 1) * CONV_COLS)
        x = xbc_ref[:, cols]
        full = jnp.concatenate([carry[:, cols], x], axis=0)
        acc = b_ref[:, cols] + x * w_ref[tail:tail + 1, cols]
        for k in range(tail):
            acc = acc + full[8 - tail + k:8 - tail + k + tt, :] * w_ref[k:k + 1, cols]
        xc_ref[:, cols] = jax.nn.silu(acc)
        new_tail = full[8 + tt - tail:8 + tt, :]
        carry[8 - tail:8, cols] = new_tail
        cso_ref[0, :, cols] = new_tail
    dts_ref[...] = jax.nn.softplus(dt_ref[...] + dtb_ref[...])


def _conv(xbc, dt, conv_state, w, b, dtb, n_batch, n_tok, tt):
    n_chunks = n_tok // tt
    blk = lambda i, c: (i * n_chunks + c, 0)
    const = lambda i, c: (0, 0)
    st = lambda i, c: (i, 0, 0)
    return pl.pallas_call(
        _conv_kernel,
        grid=(n_batch, n_chunks),
        in_specs=[pl.BlockSpec((tt, CONV_DIM), blk), pl.BlockSpec((tt, LANES), blk),
                  pl.BlockSpec((1, CONV_W - 1, CONV_DIM), st),
                  pl.BlockSpec((CONV_W, CONV_DIM), const), pl.BlockSpec((1, CONV_DIM), const), pl.BlockSpec((1, LANES), const)],
        out_specs=[pl.BlockSpec((tt, CONV_DIM), blk), pl.BlockSpec((tt, LANES), blk),
                   pl.BlockSpec((1, CONV_W - 1, CONV_DIM), st)],
        out_shape=[jax.ShapeDtypeStruct(xbc.shape, F32), jax.ShapeDtypeStruct(dt.shape, F32),
                   jax.ShapeDtypeStruct(conv_state.shape, F32)],
        scratch_shapes=[pltpu.VMEM((8, CONV_DIM), F32)],
        compiler_params=_cparams("parallel", "arbitrary"),
        name="ssd_conv",
    )(xbc, dt, conv_state, w, b, dtb)


def _ssd_kernel(xs_ref, bm_ref, cm_ref, z_ref, dt_ref, s0_ref, a_ref, dsk_ref, ng_ref, tril_ref, mask_ref, elast_ref,
                o_ref, sout_ref, s_scr, yz_scr, *, n_seq, seq_len, gps):
    c = pl.program_id(2)

    @pl.when(c == 0)
    def _():
        s_scr[...] = s0_ref[...]

    lane = _lane_iota((ROWS, LANES))
    row = _row_iota((ROWS, LANES))
    lo_half = lane < (LANES // 2)
    dt = dt_ref[...]
    cs = _dot01(tril_ref[...], dt * a_ref[...])
    cs_last = _dot01(elast_ref[...], cs)
    if gps != SSM_GROUPS:
        shift = (LANES - pl.program_id(1) * (gps * SSM_HEADS // SSM_GROUPS)) % LANES
        dt, cs, cs_last = (pltpu.roll(v, shift, 1) for v in (dt, cs, cs_last))
    cs_t = cs.T
    ecs = jnp.exp(cs)
    dt_tail = dt * jnp.exp(cs_last - cs)
    causal = mask_ref[...] > 0.0

    def per_head(v, h0):
        return jnp.where(lo_half, v[:, h0:h0 + 1], v[:, h0 + 1:h0 + 2])

    for g in range(gps):
        gcols = slice(g * D_STATE, (g + 1) * D_STATE)
        cg = cm_ref[:, gcols].astype(BF16)
        bg = bm_ref[:, gcols].astype(BF16)
        cb = _dot_nt(cg, bg)
        for pp in range(2):
            pair = 2 * g + pp
            h0 = 2 * pair
            pcols = slice(pair * LANES, (pair + 1) * LANES)
            x2 = xs_ref[:, pcols]
            xdt = x2 * per_head(dt, h0)
            ys = []
            for e in range(2):
                h = h0 + e
                seg = cs[:, h:h + 1] - cs_t[h:h + 1, :]
                wmat = jnp.exp(jnp.where(causal, seg, _NEG_INF)) * cb
                ys.append(_dot(wmat, xdt))
            y2 = jnp.where(lo_half, ys[0], ys[1])
            if n_seq == 1:
                inter = _dot_nt(cg, s_scr[0, pair])
            else:
                inter = jnp.zeros((ROWS, LANES), F32)
                for s in range(n_seq):
                    in_seq = (row >= s * seq_len) & (row < (s + 1) * seq_len)
                    inter = inter + jnp.where(in_seq, _dot_nt(cg, s_scr[s, pair]), 0.0)
            y2 = y2 + inter * per_head(ecs, h0) + dsk_ref[:, pcols] * x2
            yz_scr[:, pcols] = y2 * jax.nn.silu(z_ref[:, pcols])
            xw_t = (x2 * per_head(dt_tail, h0)).T
            for s in range(n_seq):
                last = (s + 1) * seq_len - 1
                if n_seq == 1:
                    xt = xw_t
                else:
                    xt = jnp.where((lane >= s * seq_len) & (lane <= last), xw_t, 0.0)
                dec = jnp.exp(jnp.where(row < (LANES // 2), cs_t[h0:h0 + 1, last:last + 1], cs_t[h0 + 1:h0 + 2, last:last + 1]))
                s_scr[s, pair] = s_scr[s, pair] * dec + _dot(xt, bg)
    gw = D_INNER // SSM_GROUPS
    for g in range(gps):
        cols = slice(g * gw, (g + 1) * gw)
        o_ref[:, cols] = _rmsnorm(yz_scr[:, cols], ng_ref[:, cols])

    @pl.when(c == pl.num_programs(2) - 1)
    def _():
        sout_ref[...] = s_scr[...]


def _ssd(xc, z, dts, s0, a_row, dsk_row, ng_row, n_batch, n_tok):
    seq_len = min(ROWS, n_tok)
    n_seq = ROWS // seq_len
    n_chunks = n_tok // seq_len
    r = np.arange(ROWS)
    li, sid = r % seq_len, r // seq_len
    mask = ((sid[:, None] == sid[None, :]) & (li[:, None] >= li[None, :]))
    elast = (r[None, :] == (sid * seq_len + seq_len - 1)[:, None])
    n_pairs = SSM_HEADS // 2
    s0p = s0.reshape(n_batch, n_pairs, LANES, D_STATE)
    gps = SSM_GROUPS if n_seq == 1 else 1
    xw = gps * (D_INNER // SSM_GROUPS)
    bw = gps * D_STATE
    rows = lambda i, j, c: i * n_chunks + c
    xblk = lambda i, j, c: (rows(i, j, c), j)
    bblk = lambda i, j, c: (rows(i, j, c), D_INNER // bw + j)
    cblk = lambda i, j, c: (rows(i, j, c), (D_INNER + SSM_GROUPS * D_STATE) // bw + j)
    dtblk = lambda i, j, c: (rows(i, j, c), 0)
    st = lambda i, j, c: (i, j, 0, 0)
    const = lambda i, j, c: (0, 0)
    gconst = lambda i, j, c: (0, j)
    state_block = (n_seq, 2 * gps, LANES, D_STATE)
    o, s_out = pl.pallas_call(
        functools.partial(_ssd_kernel, n_seq=n_seq, seq_len=seq_len, gps=gps),
        grid=(n_batch // n_seq, SSM_GROUPS // gps, n_chunks),
        in_specs=[pl.BlockSpec((ROWS, xw), xblk), pl.BlockSpec((ROWS, bw), bblk), pl.BlockSpec((ROWS, bw), cblk),
                  pl.BlockSpec((ROWS, xw), xblk), pl.BlockSpec((ROWS, LANES), dtblk),
                  pl.BlockSpec(state_block, st),
                  pl.BlockSpec((1, LANES), const), pl.BlockSpec((1, xw), gconst), pl.BlockSpec((1, xw), gconst),
                  pl.BlockSpec((ROWS, ROWS), const), pl.BlockSpec((ROWS, ROWS), const), pl.BlockSpec((ROWS, ROWS), const)],
        out_specs=[pl.BlockSpec((ROWS, xw), xblk), pl.BlockSpec(state_block, st)],
        out_shape=[jax.ShapeDtypeStruct((n_batch * n_tok, D_INNER), F32), jax.ShapeDtypeStruct(s0p.shape, F32)],
        scratch_shapes=[pltpu.VMEM(state_block, F32), pltpu.VMEM((ROWS, xw), F32)],
        compiler_params=_cparams("parallel", "parallel", "arbitrary"),
        name="ssd_scan",
    )(xc, xc, xc, z, dts, s0p, a_row, dsk_row, ng_row, jnp.asarray(mask, BF16), jnp.asarray(mask, F32),
      jnp.asarray(elast, BF16))
    return o, s_out.reshape(s0.shape)


def _rope_tables(pos, min_rows):
    half = MOBA_HD // 2
    inv = jnp.power(ROPE_THETA, -jnp.arange(half, dtype=F32) * 2.0 / MOBA_HD)
    ang = jnp.asarray(pos, F32)[:, None] * inv[None, :]
    cos, sin = jnp.cos(ang), jnp.sin(ang)
    reps = (max(1, min_rows // ang.shape[0]), LANES // MOBA_HD)
    return jnp.tile(jnp.concatenate([cos, cos], axis=1), reps), jnp.tile(jnp.concatenate([-sin, sin], axis=1), reps)


def _pad_cols(w, width):
    return jnp.pad(w, ((0, 0), (0, width - w.shape[1])))


TOKEN_TILE = 256
OUTPROJ_TILE = 512


def _even_layer(h, nb, nt, pos, ret0, cache, norm_g, w_in, w_out, q_g, k_g):
    tm = TOKEN_TILE
    cos, sin = _rope_tables(pos, tm)
    head_tile = lambda v: jnp.tile(v, MOBA_HEADS)[None, :]
    width = MOBA_HEADS * MOBA_HD
    blockdiag = jnp.asarray(np.arange(width)[:, None] // MOBA_HD == np.arange(width)[None, :] // MOBA_HD, BF16)
    outs = _even_inproj(h, norm_g[None, :], w_in.astype(BF16), cos, sin, head_tile(q_g), head_tile(k_g), blockdiag,
                        tm, nt, key_minor=cache is None)
    rq, rk, rv, rg = outs[:4]
    if cache is None:
        qt, kb, kt, vt, vtb, kbar = outs[4:]
        mo = _moba_prompt(qt, kb, vtb, kbar.reshape(-1, width), nb, nt)
        mk, mv = (a.reshape(nb, MOBA_HEADS, MOBA_HD, nt).transpose(0, 3, 1, 2) for a in (kt, vt))
    else:
        mq, mk, mv = outs[4:]
        mo = _moba_sample(mq, mk, mv, *cache, nb, nt)
        mk, mv = (a.reshape(nb, nt, MOBA_HEADS, MOBA_HD) for a in (mk, mv))
    ro, ret_state = _retention(rq, rk, rv, rg, ret0, nb, nt)
    w_out = w_out.astype(BF16)
    h = _outproj(h, (ro, mo), (w_out[:RET_HEADS * RET_DV], w_out[RET_HEADS * RET_DV:]), min(OUTPROJ_TILE, h.shape[0]))
    return h, ret_state, mk, mv


def _odd_layer(h, nb, nt, conv0, ssm0, norm_g, w_in, conv_w, conv_b, dt_bias, a_log, d_skip, ssd_norm, w_out):
    tm = TOKEN_TILE
    w_in = w_in.astype(BF16)
    weights = (w_in[:, :D_INNER], w_in[:, D_INNER:D_INNER + CONV_DIM], _pad_cols(w_in[:, D_INNER + CONV_DIM:], LANES))
    conv_params = (conv_w, conv_b[None, :], _pad_cols(dt_bias[None, :], LANES))
    if nt % tm == 0:
        z, xc, dts, conv_state = _odd_inproj_conv(h, norm_g[None, :], *weights, conv0, *conv_params, tm, nt)
    else:
        z, xbc, dt = _odd_inproj(h, norm_g[None, :], *weights, tm)
        xc, dts, conv_state = _conv(xbc, dt, conv0, *conv_params, nb, nt, nt)
    a_row = _pad_cols(-jnp.exp(a_log)[None, :], LANES)
    dsk_row = jnp.repeat(d_skip, D_INNER // SSM_HEADS)[None, :]
    yzn, ssm_state = _ssd(xc, z, dts, ssm0, a_row, dsk_row, ssd_norm[None, :], nb, nt)
    h = _outproj(h, (yzn,), (w_out.astype(BF16),), min(OUTPROJ_TILE, h.shape[0]))
    return h, conv_state, ssm_state


def _moe_layer(h, norm_g, w_rg, w_re, w_g, w_u, w_d, layer):
    wr = _pad_cols(jnp.concatenate([w_rg, w_re], axis=1), LANES)
    return _hier_moe(h, norm_g[None, :], wr, w_g, w_u, w_d, layer)


def kernel(x_prompt, x_sample, state_ret, cache_k, cache_v, page_table, state_ssm, state_conv, norm_mix, norm_ffn,
           w_in_even, w_out_even, q_norm, k_norm, w_in_odd, conv_w, conv_b, dt_bias, a_log, d_skip, ssd_norm, w_out_odd,
           w_router_group, w_router_expert, w_expert_gate, w_expert_up, w_expert_down):
    bp, seq, _ = x_prompt.shape
    bs, dec_seq, _ = x_sample.shape
    past_len = page_table.shape[1] * PAGE_SIZE
    groups = ((x_prompt.reshape(bp * seq, D_MODEL), bp, seq, np.arange(seq, dtype=np.float32)),
              (x_sample.reshape(bs * dec_seq, D_MODEL), bs, dec_seq, past_len + np.arange(dec_seq, dtype=np.float32)))

    def moe(h, layer):
        return _moe_layer(h, norm_ffn[layer], w_router_group[layer], w_router_expert[layer],
                          w_expert_gate, w_expert_up, w_expert_down, layer)

    outs = []
    for gi, (h, nb, nt, pos) in enumerate(groups):
        if gi == 0:
            ret0 = jnp.zeros((nb, RET_HEADS, RET_DK, RET_DV), F32)
            cache = None
            conv0 = jnp.zeros((nb, CONV_W - 1, CONV_DIM), F32)
            ssm0 = jnp.zeros((nb, SSM_HEADS, D_INNER // SSM_HEADS, D_STATE), F32)
        else:
            ret0 = state_ret[0]
            cache = (cache_k[0].transpose(0, 2, 3, 1), cache_v[0].transpose(0, 2, 3, 1), page_table)
            conv0, ssm0 = state_conv[0], state_ssm[0]
        h, ret_state, mk, mv = _even_layer(h, nb, nt, pos, ret0, cache, norm_mix[0], w_in_even[0], w_out_even[0],
                                           q_norm[0], k_norm[0])
        h = moe(h, 0)
        h, conv_state, ssm_state = _odd_layer(h, nb, nt, conv0, ssm0, norm_mix[1], w_in_odd[0], conv_w[0], conv_b[0],
                                              dt_bias[0], a_log[0], d_skip[0], ssd_norm[0], w_out_odd[0])
        h = moe(h, 1)
        outs.append(dict(h=h.reshape(nb, nt, D_MODEL), ret=ret_state[None], k=mk[None], v=mv[None],
                         ssm=ssm_state[None], conv=conv_state[None]))
    p, s = outs
    return (p["h"], s["h"], p["ret"], s["ret"], p["k"], p["v"], s["k"], s["v"], p["ssm"], s["ssm"], p["conv"], s["conv"])
```

```python
import functools
import math

import jax
import jax.numpy as jnp
import numpy as np
from jax import lax
from jax.experimental import pallas as pl
from jax.experimental.pallas import tpu as pltpu

F32 = jnp.float32
BF16 = jnp.bfloat16
I32 = jnp.int32

D_MODEL = 1024
PAGE_SIZE = 128
RET_HEADS = 8
RET_DK = 64
RET_DV = 128
MOBA_HEADS = 8
MOBA_HD = 64
MOBA_BLOCK = 256
MOBA_TOPK = 3
ROPE_THETA = 10000.0
D_INNER = 2048
SSM_HEADS = 32
SSM_GROUPS = 8
D_STATE = 128
CONV_W = 4
CONV_DIM = 4096
MOE_GROUPS = 4
MOE_EPG = 4
N_EXPERTS = 16
EXPERT_FF = 512
NORM_EPS = 1e-6
N_CLASSES = MOE_GROUPS * 6
LANES = 128
ROWS = 128
MOE_TILE = 256
MOE_TILE_SMALL = 32
VMEM_LIMIT = 56 * 1024 * 1024

_NEG_INF = float("-inf")


def _cparams(*sem):
    return pltpu.CompilerParams(dimension_semantics=sem, vmem_limit_bytes=VMEM_LIMIT)


def _dot(a, b):
    return jnp.dot(a.astype(BF16), b.astype(BF16), preferred_element_type=F32)


def _dot_nt(a, b):
    return lax.dot_general(a.astype(BF16), b.astype(BF16), (((1,), (1,)), ((), ())), preferred_element_type=F32)


def _split3(x):
    hi = x.astype(BF16)
    r = x - hi.astype(F32)
    mid = r.astype(BF16)
    lo = (r - mid.astype(F32)).astype(BF16)
    return hi, mid, lo


def _dot01(m01, x):
    hi, mid, lo = _split3(x)
    return (jnp.dot(m01, hi, preferred_element_type=F32) + jnp.dot(m01, mid, preferred_element_type=F32)
            + jnp.dot(m01, lo, preferred_element_type=F32))


def _rmsnorm(x, g=None):
    y = x * lax.rsqrt(jnp.mean(x * x, axis=-1, keepdims=True) + NORM_EPS)
    return y if g is None else y * g


def _lane_iota(shape):
    return lax.broadcasted_iota(I32, shape, len(shape) - 1)


def _row_iota(shape):
    return lax.broadcasted_iota(I32, shape, len(shape) - 2)


def _even_inproj_kernel(x_ref, g_ref, w_ref, cos_ref, sin_ref, qg_ref, kg_ref, bd_ref,
                        rq_ref, rk_ref, rv_ref, rg_ref, *moba_refs, key_minor):
    xb = _rmsnorm(x_ref[...], g_ref[...]).astype(BF16)
    cos = cos_ref[...]
    sin = sin_ref[...]
    first_half = (_lane_iota(cos.shape) % MOBA_HD) < (MOBA_HD // 2)

    def proj(c0, c1):
        return jnp.dot(xb, w_ref[:, c0:c1], preferred_element_type=F32)

    def rope(a):
        outs = []
        for j in range(a.shape[1] // LANES):
            s = a[:, j * LANES:(j + 1) * LANES]
            rot = jnp.where(first_half, pltpu.roll(s, LANES - MOBA_HD // 2, 1), pltpu.roll(s, MOBA_HD // 2, 1))
            outs.append(s * cos + rot * sin)
        return jnp.concatenate(outs, axis=1)

    def qk_norm(a, gam):
        ss = jnp.dot((a * a).astype(BF16), bd_ref[...], preferred_element_type=F32)
        return a * lax.rsqrt(ss * (1.0 / MOBA_HD) + NORM_EPS) * gam

    rq_ref[...] = rope(proj(0, 512)).astype(BF16)
    rk_ref[...] = rope(proj(512, 1024)) * (RET_DK ** -0.5)
    rv_ref[...] = proj(1024, 2048).astype(BF16)
    rg_ref[...] = proj(2048, 3072)
    mq = rope(qk_norm(proj(3072, 3584), qg_ref[...]))
    mk = rope(qk_norm(proj(3584, 4096), kg_ref[...]))
    mv = proj(4096, 4608)
    if not key_minor:
        mq_ref, mk_ref, mv_ref = moba_refs
        mq_ref[...] = mq
        mk_ref[...] = mk
        mv_ref[...] = mv
    else:
        qt_ref, kb_ref, kt_ref, vt_ref, vtb_ref, kbar_ref = moba_refs
        qt_ref[0] = mq.T.astype(BF16)
        kb_ref[...] = mk.astype(BF16)
        kt_ref[0] = mk.T
        vt = mv.T
        vt_ref[0] = vt
        vtb_ref[0] = vt.astype(BF16)
        kbar_ref[0] = jnp.mean(mk, axis=0, keepdims=True)


def _even_inproj(x, g, w, cos, sin, qg, kg, bd, tm, n_tok, key_minor):
    n = x.shape[0]
    n_pos_blocks = cos.shape[0] // tm
    row = lambda i: (i, 0)
    const = lambda i: (0, 0)
    width = MOBA_HEADS * MOBA_HD
    out_specs = [pl.BlockSpec((tm, c), row) for c in (512, 512, 1024, 1024)]
    out_shape = [jax.ShapeDtypeStruct((n, c), dt) for c, dt in ((512, BF16), (512, F32), (1024, BF16), (1024, F32))]
    if not key_minor:
        out_specs += [pl.BlockSpec((tm, width), row)] * 3
        out_shape += [jax.ShapeDtypeStruct((n, width), F32)] * 3
    else:
        assert tm == MOBA_BLOCK and n_tok % tm == 0
        tiles = n_tok // tm
        tspec = pl.BlockSpec((1, width, tm), lambda i: (i // tiles, 0, i % tiles))
        tshape = lambda dt: jax.ShapeDtypeStruct((n // n_tok, width, n_tok), dt)
        out_specs += [tspec, pl.BlockSpec((tm, width), row), tspec, tspec, tspec, pl.BlockSpec((1, 1, width), lambda i: (i, 0, 0))]
        out_shape += [tshape(BF16), jax.ShapeDtypeStruct((n, width), BF16), tshape(F32), tshape(F32), tshape(BF16),
                      jax.ShapeDtypeStruct((n // tm, 1, width), F32)]
    return pl.pallas_call(
        functools.partial(_even_inproj_kernel, key_minor=key_minor),
        grid=(n // tm,),
        in_specs=[pl.BlockSpec((tm, D_MODEL), row), pl.BlockSpec((1, D_MODEL), const),
                  pl.BlockSpec(w.shape, const),
                  pl.BlockSpec((tm, LANES), lambda i: (i % n_pos_blocks, 0)),
                  pl.BlockSpec((tm, LANES), lambda i: (i % n_pos_blocks, 0)),
                  pl.BlockSpec((1, 512), const), pl.BlockSpec((1, 512), const), pl.BlockSpec((512, 512), const)],
        out_specs=out_specs,
        out_shape=out_shape,
        compiler_params=_cparams("parallel"),
        name="even_inproj",
    )(x, g, w, cos, sin, qg, kg, bd)


def _retention_kernel(rq_ref, rk_ref, rv_ref, rg_ref, s0_ref, dmat_ref, qdec_ref, kdec_ref, sdec_ref,
                      o_ref, sout_ref, s_scr, *, n_seq, seq_len):
    c = pl.program_id(1)

    @pl.when(c == 0)
    def _():
        s_scr[...] = s0_ref[...]

    lane = _lane_iota((ROWS, LANES))
    row = _row_iota((ROWS, LANES))
    for p in range(RET_HEADS // 2):
        q2 = rq_ref[:, p * LANES:(p + 1) * LANES]
        k2 = rk_ref[:, p * LANES:(p + 1) * LANES]
        kd_t = (k2 * kdec_ref[p]).T
        v_pair = []
        for e in range(2):
            h = 2 * p + e
            head_lanes = (lane < RET_DK) if e == 0 else (lane >= RET_DK)
            qm = jnp.where(head_lanes, q2, 0.0)
            scores = _dot_nt(qm, k2) * dmat_ref[h]
            vh = rv_ref[:, h * RET_DV:(h + 1) * RET_DV]
            v_pair.append(vh)
            if n_seq == 1:
                inter = _dot(qm, s_scr[0, p])
            else:
                inter = jnp.zeros((ROWS, RET_DV), F32)
                for s in range(n_seq):
                    in_seq = (row >= s * seq_len) & (row < (s + 1) * seq_len)
                    inter = inter + jnp.where(in_seq, _dot(qm, s_scr[s, p]), 0.0)
            o = _dot(scores, vh) + inter * qdec_ref[h]
            gate = rg_ref[:, h * RET_DV:(h + 1) * RET_DV]
            o_ref[:, h * RET_DV:(h + 1) * RET_DV] = jax.nn.silu(gate) * _rmsnorm(o)
        for s in range(n_seq):
            if n_seq == 1:
                kt = kd_t
            else:
                kt = jnp.where((lane >= s * seq_len) & (lane < (s + 1) * seq_len), kd_t, 0.0)
            upd = jnp.where(row < RET_DK, _dot(kt, v_pair[0]), _dot(kt, v_pair[1]))
            s_scr[s, p] = s_scr[s, p] * sdec_ref[p] + upd

    @pl.when(c == pl.num_programs(1) - 1)
    def _():
        sout_ref[...] = s_scr[...]


def _retention_tables(seq_len):
    f32 = np.float32
    log_g = np.log1p(-np.exp2(f32(-5.0) - np.arange(RET_HEADS, dtype=f32))).astype(f32)
    r = np.arange(ROWS)
    li = (r % seq_len).astype(f32)
    sid = r // seq_len
    diff = li[:, None] - li[None, :]
    same = sid[:, None] == sid[None, :]
    dmat = np.where(same[None] & (diff[None] >= 0),
                    np.exp(np.maximum(diff, f32(0.0))[None] * log_g[:, None, None]), f32(0.0)).astype(f32)
    qdec = np.broadcast_to(np.exp((li + f32(1.0))[None, :, None] * log_g[:, None, None]), (RET_HEADS, ROWS, RET_DV))
    lane_head = np.arange(LANES) // RET_DK
    pair_log_g = log_g.reshape(RET_HEADS // 2, 2)[:, lane_head]
    kdec = np.exp((f32(seq_len - 1.0) - li)[None, :, None] * pair_log_g[:, None, :])
    sdec = np.broadcast_to(np.exp(f32(seq_len) * pair_log_g)[:, :, None], (RET_HEADS // 2, LANES, RET_DV))
    return tuple(jnp.asarray(t, F32) for t in (dmat, qdec, kdec, sdec))


def _retention(rq, rk, rv, rg, s0, n_batch, n_tok):
    seq_len = min(ROWS, n_tok)
    n_seq = ROWS // seq_len
    n_chunks = n_tok // seq_len
    tables = _retention_tables(seq_len)
    s0p = s0.reshape(n_batch, RET_HEADS // 2, 2 * RET_DK, RET_DV)
    blk = lambda i, c: (i * n_chunks + c, 0)
    st = lambda i, c: (i, 0, 0, 0)
    c3 = lambda i, c: (0, 0, 0)
    o, s_out = pl.pallas_call(
        functools.partial(_retention_kernel, n_seq=n_seq, seq_len=seq_len),
        grid=(n_batch // n_seq, n_chunks),
        in_specs=[pl.BlockSpec((ROWS, 512), blk), pl.BlockSpec((ROWS, 512), blk),
                  pl.BlockSpec((ROWS, 1024), blk), pl.BlockSpec((ROWS, 1024), blk),
                  pl.BlockSpec((n_seq,) + s0p.shape[1:], st)]
                 + [pl.BlockSpec(t.shape, c3) for t in tables],
        out_specs=[pl.BlockSpec((ROWS, 1024), blk), pl.BlockSpec((n_seq,) + s0p.shape[1:], st)],
        out_shape=[jax.ShapeDtypeStruct((n_batch * n_tok, 1024), F32), jax.ShapeDtypeStruct(s0p.shape, F32)],
        scratch_shapes=[pltpu.VMEM((n_seq,) + s0p.shape[1:], F32)],
        compiler_params=_cparams("parallel", "arbitrary"),
        name="retention",
    )(rq, rk, rv, rg, s0p, *tables)
    return o, s_out.reshape(s0.shape)


def _moba_prompt_kernel(qt_ref, k_ref, vt_ref, kbar_ref, o_ref, sel_scr, qtm_scr, m_scr, l_scr, acc_scr, *, n_blocks):
    qb = pl.program_id(1)
    blk = MOBA_BLOCK
    n_pairs = MOBA_HEADS // 2
    blk_row = _row_iota((n_blocks, blk))
    lo_rows = _row_iota((LANES, blk)) < MOBA_HD
    lo_lanes = _lane_iota((n_blocks, LANES)) < MOBA_HD
    key_le_query = _row_iota((blk, blk)) <= _lane_iota((blk, blk))

    for p in range(n_pairs):
        rows = slice(p * LANES, (p + 1) * LANES)
        qt2 = qt_ref[0, rows, :]
        kb2 = kbar_ref[:, rows]
        for e in range(2):
            h = 2 * p + e
            gate = _dot(jnp.where(lo_lanes if e == 0 else ~lo_lanes, kb2, 0.0), qt2)
            cnt = jnp.zeros((n_blocks, blk), F32)
            for m in range(n_blocks - 1):
                gm = gate[m:m + 1, :]
                ahead = (gm > gate) | ((gm == gate) & (blk_row > m))
                cnt = cnt + jnp.where(ahead, (qb > m).astype(F32), 0.0)
            sel_scr[h] = jnp.where((blk_row < qb) & (cnt < float(MOBA_TOPK)), 1.0, 0.0)
            head_rows = lo_rows if e == 0 else ~lo_rows
            qtm_scr[h] = (jnp.where(head_rows, qt2, 0.0) * (MOBA_HD ** -0.5)).astype(BF16)

    def attend(n, own, span=1):
        c0 = pl.multiple_of(n * blk, blk)
        keys = span * blk
        for p in range(n_pairs):
            rows = slice(p * LANES, (p + 1) * LANES)
            k2 = k_ref[pl.ds(c0, keys), rows]
            for e in range(2):
                h = 2 * p + e
                st = jnp.dot(k2, qtm_scr[h], preferred_element_type=F32)
                if own:
                    st = jnp.where(key_le_query, st, _NEG_INF)
                else:
                    picked = sel_scr[h, pl.ds(n, 1), :]
                    for j in range(1, span):
                        picked = jnp.where(_row_iota((keys, blk)) < j * blk, picked, sel_scr[h, pl.ds(n + j, 1), :])
                    st = jnp.where(picked > 0.0, st, _NEG_INF)
                mx = jnp.max(st, axis=0, keepdims=True)
                if own:
                    m_new = mx
                else:
                    m_old = m_scr[h:h + 1, :]
                    m_new = jnp.maximum(m_old, mx)
                    alpha = jnp.exp(m_old - m_new)
                pt = jnp.exp(st - m_new)
                psum = jnp.sum(pt, axis=0, keepdims=True)
                vt = vt_ref[0, h * MOBA_HD:(h + 1) * MOBA_HD, pl.ds(c0, keys)]
                pv = jnp.dot(vt, pt.astype(BF16), preferred_element_type=F32)
                l_scr[h:h + 1, :] = psum if own else l_scr[h:h + 1, :] * alpha + psum
                acc_scr[h] = pv if own else acc_scr[h] * alpha + pv
                m_scr[h:h + 1, :] = m_new

    attend(qb, True)

    def body(j, c):
        attend(2 * j, False, span=2)
        return c

    lax.fori_loop(0, lax.shift_right_logical(qb, 1), body, 0)

    @pl.when((qb & 1) == 1)
    def _():
        attend(qb - 1, False)
    for p in range(n_pairs):
        pair = [acc_scr[h] / l_scr[h:h + 1, :] for h in (2 * p, 2 * p + 1)]
        o_ref[:, p * LANES:(p + 1) * LANES] = jnp.concatenate(pair, axis=0).T


def _moba_prompt(qt, kb, vtb, kbar, n_batch, n_tok):
    n_blocks = n_tok // MOBA_BLOCK
    width = MOBA_HEADS * MOBA_HD
    return pl.pallas_call(
        functools.partial(_moba_prompt_kernel, n_blocks=n_blocks),
        grid=(n_batch, n_blocks),
        in_specs=[pl.BlockSpec((1, width, MOBA_BLOCK), lambda b, i: (b, 0, i)),
                  pl.BlockSpec((n_tok, width), lambda b, i: (b, 0)),
                  pl.BlockSpec((1, width, n_tok), lambda b, i: (b, 0, 0)),
                  pl.BlockSpec((n_blocks, width), lambda b, i: (b, 0))],
        out_specs=pl.BlockSpec((MOBA_BLOCK, width), lambda b, i: (b * n_blocks + i, 0)),
        out_shape=jax.ShapeDtypeStruct((n_batch * n_tok, width), F32),
        scratch_shapes=[pltpu.VMEM((MOBA_HEADS, n_blocks, MOBA_BLOCK), F32),
                        pltpu.VMEM((MOBA_HEADS, LANES, MOBA_BLOCK), BF16),
                        pltpu.VMEM((MOBA_HEADS, MOBA_BLOCK), F32), pltpu.VMEM((MOBA_HEADS, MOBA_BLOCK), F32),
                        pltpu.VMEM((MOBA_HEADS, MOBA_HD, MOBA_BLOCK), F32)],
        compiler_params=_cparams("parallel", "arbitrary"),
        name="moba_prompt",
    )(qt, kb, vtb, kbar)


PAGES_PER_BLOCK = MOBA_BLOCK // PAGE_SIZE
KBAR_BLOCKS_PER_STEP = 16


def _kbar_kernel(pt_ref, *refs):
    page_refs, o_ref = refs[:-1], refs[-1]
    for j in range(KBAR_BLOCKS_PER_STEP):
        tot = page_refs[j * PAGES_PER_BLOCK][0]
        for half in range(1, PAGES_PER_BLOCK):
            tot = tot + page_refs[j * PAGES_PER_BLOCK + half][0]
        o_ref[0, j] = jnp.sum(tot, axis=-1) * (1.0 / MOBA_BLOCK)


def _moba_kbar(pool_k, page_table):
    n_seq, n_pages = page_table.shape
    n_blocks = n_pages // PAGES_PER_BLOCK
    ppstep = KBAR_BLOCKS_PER_STEP * PAGES_PER_BLOCK

    def page_spec(j):
        return pl.BlockSpec((1, MOBA_HEADS, MOBA_HD, PAGE_SIZE),
                            lambda b, i, pt: (pt[b * n_pages + i * ppstep + j], 0, 0, 0))

    return pl.pallas_call(
        _kbar_kernel,
        grid_spec=pltpu.PrefetchScalarGridSpec(
            num_scalar_prefetch=1,
            grid=(n_seq, n_blocks // KBAR_BLOCKS_PER_STEP),
            in_specs=[page_spec(j) for j in range(ppstep)],
            out_specs=pl.BlockSpec((1, KBAR_BLOCKS_PER_STEP, MOBA_HEADS, MOBA_HD), lambda b, i, pt: (b, i, 0, 0)),
        ),
        out_shape=jax.ShapeDtypeStruct((n_seq, n_blocks, MOBA_HEADS, MOBA_HD), F32),
        compiler_params=_cparams("parallel", "parallel"),
        name="moba_kbar",
    )(page_table.reshape(-1), *([pool_k] * ppstep))


def _moba_select_kernel(q_ref, kbar_ref, o_ref, *, n_blocks, n_tok):
    lane = _lane_iota((n_tok, LANES))
    lane_f = lane.astype(F32)
    pad = jnp.zeros((LANES - n_blocks, LANES), F32)
    for h in range(MOBA_HEADS):
        p, e = divmod(h, 2)
        cols = slice(p * LANES, (p + 1) * LANES)
        q2 = q_ref[:, cols]
        head_lanes = (lane < MOBA_HD) if e == 0 else (lane >= MOBA_HD)
        qm = jnp.where(head_lanes, q2, 0.0)
        kb = jnp.concatenate([kbar_ref[0, :, cols], pad], axis=0)
        gate = jnp.where(lane < n_blocks, _dot_nt(qm, kb), _NEG_INF)
        res = jnp.zeros((n_tok, LANES), F32)
        for j in range(MOBA_TOPK):
            mx = jnp.max(gate, axis=1, keepdims=True)
            idx = jnp.min(jnp.where(gate == mx, lane_f, float(LANES)), axis=1, keepdims=True)
            res = jnp.where(lane == j, idx, res)
            gate = jnp.where(lane_f == idx, _NEG_INF, gate)
        o_ref[0, h * n_tok:(h + 1) * n_tok, :] = res.astype(I32)


def _moba_select(mq, kbar, n_seq, n_tok):
    n_blocks = kbar.shape[1]
    kb = kbar.reshape(n_seq, n_blocks, 512)
    out = pl.pallas_call(
        functools.partial(_moba_select_kernel, n_blocks=n_blocks, n_tok=n_tok),
        grid=(n_seq,),
        in_specs=[pl.BlockSpec((n_tok, 512), lambda b: (b, 0)), pl.BlockSpec((1, n_blocks, 512), lambda b: (b, 0, 0))],
        out_specs=pl.BlockSpec((1, MOBA_HEADS * n_tok, LANES), lambda b: (b, 0, 0)),
        out_shape=jax.ShapeDtypeStruct((n_seq, MOBA_HEADS * n_tok, LANES), I32),
        compiler_params=_cparams("parallel"),
        name="moba_select",
    )(mq, kb)
    return out[:, :, :MOBA_TOPK].reshape(-1)


def _moba_sample_kernel(sel_ref, pt_ref, q_ref, kn_ref, vn_ref, pk_ref, pv_ref, o_ref,
                        kbuf, vbuf, sem, *, n_tok, n_pages):
    s = pl.program_id(0)
    n_steps = pl.num_programs(0)
    n_sel = n_tok * MOBA_TOPK
    past = n_sel * MOBA_BLOCK

    def copies(step, slot):
        b = step // MOBA_HEADS
        h = step % MOBA_HEADS
        out = []
        for i in range(n_sel):
            blk = sel_ref[step * n_sel + i]
            for half in range(PAGES_PER_BLOCK):
                page = pt_ref[b * n_pages + blk * PAGES_PER_BLOCK + half]
                c0 = i * MOBA_BLOCK + half * PAGE_SIZE
                out.append(pltpu.make_async_copy(pk_ref.at[page, h], kbuf.at[slot, :, pl.ds(c0, PAGE_SIZE)], sem.at[slot, 0]))
                out.append(pltpu.make_async_copy(pv_ref.at[page, h], vbuf.at[slot, :, pl.ds(c0, PAGE_SIZE)], sem.at[slot, 1]))
        return out

    slot = s % 2

    def start_all(cps):
        for j, cp in enumerate(cps):
            cp.start(priority=(j // 2) % 2)

    @pl.when(s == 0)
    def _():
        start_all(copies(s, slot))

    @pl.when(s + 1 < n_steps)
    def _():
        start_all(copies(s + 1, 1 - slot))

    for cp in copies(s, slot):
        cp.wait()

    q = q_ref[0, 0] * (MOBA_HD ** -0.5)
    s_past = _dot(q, kbuf[slot])
    key = _lane_iota((n_tok, past))
    first_key = _row_iota((n_tok, past)) * (MOBA_TOPK * MOBA_BLOCK)
    s_past = jnp.where((key >= first_key) & (key < first_key + MOBA_TOPK * MOBA_BLOCK), s_past, _NEG_INF)
    s_own = _dot_nt(q, kn_ref[0, 0])
    s_own = jnp.where(_lane_iota((n_tok, n_tok)) <= _row_iota((n_tok, n_tok)), s_own, _NEG_INF)
    mx = jnp.maximum(jnp.max(s_past, axis=1, keepdims=True), jnp.max(s_own, axis=1, keepdims=True))
    p_past = jnp.exp(s_past - mx)
    p_own = jnp.exp(s_own - mx)
    den = jnp.sum(p_past, axis=1, keepdims=True) + jnp.sum(p_own, axis=1, keepdims=True)
    o = _dot_nt(p_past, vbuf[slot]) + _dot(p_own, vn_ref[0, 0])
    o_ref[0, 0] = o / den


def _moba_sample(mq, mk, mv, pool_k, pool_v, page_table, n_seq, n_tok):
    n_pages = page_table.shape[1]
    kbar = _moba_kbar(pool_k, page_table)
    sel = _moba_select(mq, kbar, n_seq, n_tok)

    def head_major(a):
        return a.reshape(n_seq, n_tok, MOBA_HEADS, MOBA_HD).transpose(0, 2, 1, 3)

    qh, kh, vh = head_major(mq), head_major(mk), head_major(mv)
    n_sel = n_tok * MOBA_TOPK
    spec = pl.BlockSpec((1, 1, n_tok, MOBA_HD), lambda s, sel, pt: (s // MOBA_HEADS, s % MOBA_HEADS, 0, 0))
    anyspec = pl.BlockSpec(memory_space=pl.ANY)
    o = pl.pallas_call(
        functools.partial(_moba_sample_kernel, n_tok=n_tok, n_pages=n_pages),
        grid_spec=pltpu.PrefetchScalarGridSpec(
            num_scalar_prefetch=2,
            grid=(n_seq * MOBA_HEADS,),
            in_specs=[spec, spec, spec, anyspec, anyspec],
            out_specs=spec,
            scratch_shapes=[pltpu.VMEM((2, MOBA_HD, n_sel * MOBA_BLOCK), F32),
                            pltpu.VMEM((2, MOBA_HD, n_sel * MOBA_BLOCK), F32),
                            pltpu.SemaphoreType.DMA((2, 2))],
        ),
        out_shape=jax.ShapeDtypeStruct(qh.shape, F32),
        compiler_params=_cparams("arbitrary"),
        name="moba_sample",
    )(sel, page_table.reshape(-1), qh, kh, vh, pool_k, pool_v)
    return o.transpose(0, 2, 1, 3).reshape(n_seq * n_tok, MOBA_HEADS * MOBA_HD)


def _outproj_kernel(*refs, n_in):
    x_ref, a_refs, w_refs, o_ref = refs[0], refs[1:1 + n_in], refs[1 + n_in:1 + 2 * n_in], refs[-1]
    acc = x_ref[...]
    for a_ref, w_ref in zip(a_refs, w_refs):
        acc = acc + jnp.dot(a_ref[...].astype(BF16), w_ref[...], preferred_element_type=F32)
    o_ref[...] = acc


def _outproj(x, acts, ws, tm):
    n = x.shape[0]
    row = lambda i: (i, 0)
    const = lambda i: (0, 0)
    return pl.pallas_call(
        functools.partial(_outproj_kernel, n_in=len(acts)),
        grid=(n // tm,),
        in_specs=[pl.BlockSpec((tm, D_MODEL), row)] + [pl.BlockSpec((tm, a.shape[1]), row) for a in acts]
                 + [pl.BlockSpec(w.shape, const) for w in ws],
        out_specs=pl.BlockSpec((tm, D_MODEL), row),
        out_shape=jax.ShapeDtypeStruct((n, D_MODEL), F32),
        compiler_params=_cparams("parallel"),
        name="outproj",
    )(x, *acts, *ws)


ROUTE_ROWS = 32


def _router_kernel(h_ref, g_ref, wr_ref, before_ref, route_ref, cnt_ref, cnt_scr):
    i = pl.program_id(0)

    @pl.when(i == 0)
    def _():
        cnt_scr[...] = jnp.zeros(cnt_scr.shape, F32)

    xn = _rmsnorm(h_ref[...], g_ref[...])
    logits = _dot(xn, wr_ref[...]).T[:ROUTE_ROWS, :]
    tm = logits.shape[1]
    lane = _row_iota((ROUTE_ROWS, tm)).astype(F32)
    big = float(LANES)

    def first_argmax(v):
        mx = jnp.max(v, axis=0, keepdims=True)
        return mx, jnp.min(jnp.where(v == mx, lane, big), axis=0, keepdims=True)

    is_group = lane < float(MOE_GROUPS)
    lg = jnp.where(is_group, logits, _NEG_INF)
    mg, gi = first_argmax(lg)
    wg = 1.0 / jnp.sum(jnp.where(is_group, jnp.exp(lg - mg), 0.0), axis=0, keepdims=True)
    e0 = float(MOE_GROUPS) + float(MOE_EPG) * gi
    in_group = (lane >= e0) & (lane < e0 + float(MOE_EPG))
    le = jnp.where(in_group, logits, _NEG_INF)
    m1, i1 = first_argmax(le)
    m2, i2 = first_argmax(jnp.where(lane == i1, _NEG_INF, le))
    r = jnp.exp(m2 - m1)
    g1 = wg / (1.0 + r)
    g2 = wg * r / (1.0 + r)
    a = jnp.minimum(i1, i2) - e0
    b = jnp.maximum(i1, i2) - e0
    ga = jnp.where(i1 < i2, g1, g2)
    gb = jnp.where(i1 < i2, g2, g1)
    pair = a * (7.0 - a) * 0.5 + (b - a - 1.0)
    cls = gi * 6.0 + pair
    onehot = jnp.where(lane == cls, 1.0, 0.0)
    before = jnp.dot(onehot.astype(BF16), before_ref[...], preferred_element_type=F32) + cnt_scr[:, 0:1]
    rank = jnp.sum(onehot * before, axis=0, keepdims=True)
    cnt_scr[...] = cnt_scr[...] + jnp.sum(onehot, axis=1, keepdims=True)
    out_row = _row_iota(route_ref.shape)
    route_ref[...] = jnp.where(out_row == 0, cls, jnp.where(out_row == 1, ga, jnp.where(out_row == 2, gb,
                               jnp.where(out_row == 3, rank, 0.0))))
    cnt_ref[...] = cnt_scr[...]


def _router(h, g, wr, tm):
    n = h.shape[0]
    earlier = jnp.asarray(np.arange(tm)[:, None] < np.arange(tm)[None, :], BF16)
    row = lambda i: (i, 0)
    const = lambda i: (0, 0)
    return pl.pallas_call(
        _router_kernel,
        grid=(n // tm,),
        in_specs=[pl.BlockSpec((tm, D_MODEL), row), pl.BlockSpec((1, D_MODEL), const),
                  pl.BlockSpec((D_MODEL, LANES), const), pl.BlockSpec((tm, tm), const)],
        out_specs=[pl.BlockSpec((8, tm), lambda i: (0, i)), pl.BlockSpec((ROUTE_ROWS, LANES), const)],
        out_shape=[jax.ShapeDtypeStruct((8, n), F32), jax.ShapeDtypeStruct((ROUTE_ROWS, LANES), F32)],
        scratch_shapes=[pltpu.VMEM((ROUTE_ROWS, LANES), F32)],
        compiler_params=_cparams("arbitrary"),
        name="moe_router",
    )(h, g, wr, earlier)


def _ffn_kernel(ea_ref, eb_ref, nv_ref, tok_ref, gates_ref, g_ref, wga_ref, wua_ref, wda_ref, wgb_ref, wub_ref, wdb_ref,
                h_hbm, out_hbm, xbuf, ybuf, row_sized, gsem, ssem):
    i = pl.program_id(0)
    n_steps = pl.num_programs(0)
    slot = i % 2
    rows = xbuf.shape[1]

    def gather_tile(tile, buf_slot):
        for r in range(rows):
            tok = tok_ref[tile * rows + r]
            pltpu.make_async_copy(h_hbm.at[pl.ds(tok, 1), :], xbuf.at[buf_slot, pl.ds(r, 1), :],
                                  gsem.at[buf_slot]).start(priority=r % 2)

    def wait_gather(buf_slot):
        pltpu.make_async_copy(h_hbm.at[pl.ds(0, rows), :], xbuf.at[buf_slot], gsem.at[buf_slot]).wait()

    def scatter_tile(tile, buf_slot):
        def start_row(r, priority):
            tok = tok_ref[tile * rows + r]
            pltpu.make_async_copy(ybuf.at[buf_slot, pl.ds(r, 1), :], out_hbm.at[pl.ds(tok, 1), :],
                                  ssem.at[buf_slot]).start(priority=priority)

        @pl.when(nv_ref[tile] == rows)
        def _():
            for r in range(rows):
                start_row(r, r % 2)

        @pl.when(nv_ref[tile] < rows)
        def _():
            def body(r, c):
                start_row(r, 0)
                return c
            lax.fori_loop(0, nv_ref[tile], body, 0)

    def wait_scatter(tile, buf_slot):
        n = nv_ref[tile]

        @pl.when(n > 0)
        def _():
            pltpu.make_async_copy(row_sized.at[pl.ds(0, n)], row_sized.at[pl.ds(0, n)], ssem.at[buf_slot]).wait()

    @pl.when(i == 0)
    def _():
        gather_tile(0, 0)

    @pl.when((i == 0) | (nv_ref[jnp.maximum(i - 1, 0)] > 0))
    def _():
        wait_gather(slot)

    @pl.when(i >= 2)
    def _():
        wait_scatter(i - 2, slot)

    @pl.when(nv_ref[i] > 0)
    def _():
        gather_tile(i + 1, 1 - slot)
        x = xbuf[slot]
        xb = _rmsnorm(x, g_ref[...]).astype(BF16)
        y = x
        for lane, wg_ref, wu_ref, wd_ref in ((0, wga_ref, wua_ref, wda_ref), (1, wgb_ref, wub_ref, wdb_ref)):
            gate = gates_ref[:, lane:lane + 1]
            hid = jax.nn.silu(_dot(xb, wg_ref[0, 0])) * _dot(xb, wu_ref[0, 0])
            y = y + _dot(hid * gate, wd_ref[0, 0])
        ybuf[slot] = y
        scatter_tile(i, slot)

    @pl.when(i == n_steps - 1)
    def _():
        wait_scatter(i - 1, 1 - slot)


def _ffn(ea, eb, nvalid, tok, gates, h, g, wg, wu, wd, layer, rows):
    n_tiles = tok.shape[0] // rows
    tile = lambda i, *_: (i, 0)
    const = lambda i, *_: (0, 0)
    up = lambda sel: pl.BlockSpec((1, 1, D_MODEL, EXPERT_FF), sel)
    down = lambda sel: pl.BlockSpec((1, 1, EXPERT_FF, D_MODEL), sel)
    sel_a = lambda i, ea, eb, nv, tok: (layer, ea[i], 0, 0)
    sel_b = lambda i, ea, eb, nv, tok: (layer, eb[i], 0, 0)
    anyspec = pl.BlockSpec(memory_space=pl.ANY)
    return pl.pallas_call(
        _ffn_kernel,
        grid_spec=pltpu.PrefetchScalarGridSpec(
            num_scalar_prefetch=4,
            grid=(n_tiles,),
            in_specs=[pl.BlockSpec((rows, 2), tile), pl.BlockSpec((1, D_MODEL), const),
                      up(sel_a), up(sel_a), down(sel_a), up(sel_b), up(sel_b), down(sel_b), anyspec],
            out_specs=anyspec,
            scratch_shapes=[pltpu.VMEM((2, rows, D_MODEL), F32), pltpu.VMEM((2, rows, D_MODEL), F32),
                            pltpu.VMEM((rows, D_MODEL // LANES, LANES), F32),
                            pltpu.SemaphoreType.DMA((2,)), pltpu.SemaphoreType.DMA((2,))],
        ),
        out_shape=jax.ShapeDtypeStruct(h.shape, F32),
        compiler_params=_cparams("arbitrary"),
        name="moe_ffn",
    )(ea, eb, nvalid, tok, gates, g, wg, wu, wd, wg, wu, wd, h)


_PAIR_A = (0, 0, 0, 1, 1, 2)
_PAIR_B = (1, 2, 3, 2, 3, 3)


def _hier_moe(h, g, wr, wg, wu, wd, layer):
    n = h.shape[0]
    tile = MOE_TILE if n >= N_CLASSES * MOE_TILE else MOE_TILE_SMALL
    route, cnt = _router(h, g, wr, TOKEN_TILE)
    cls = route[0].astype(I32)
    rank = route[3].astype(I32)
    counts = cnt[:N_CLASSES, 0].astype(I32)
    tiles_c = (counts + tile - 1) // tile
    tile_end = jnp.cumsum(tiles_c)
    tile_start = tile_end - tiles_c
    class_ids = jnp.arange(N_CLASSES, dtype=I32)[:, None]
    pos = jnp.sum(jnp.where(cls[None, :] == class_ids, (tile_start * tile)[:, None], 0), axis=0) + rank
    n_tiles = n // tile + N_CLASSES
    ti = jnp.arange(n_tiles, dtype=I32)
    total = tile_end[-1]
    tcls = jnp.sum((tile_end[None, :] <= jnp.minimum(ti, total - 1)[:, None]).astype(I32), axis=1)
    tcls = jnp.minimum(tcls, N_CLASSES - 1)
    nvalid = jnp.where(ti < total, jnp.clip(counts[tcls] - (ti - tile_start[tcls]) * tile, 0, tile), 0).astype(I32)
    grp, pair = tcls // 6, tcls % 6
    ea = grp * MOE_EPG + jnp.asarray(_PAIR_A, I32)[pair]
    eb = grp * MOE_EPG + jnp.asarray(_PAIR_B, I32)[pair]
    per_token = jnp.stack([jnp.arange(n, dtype=I32), lax.bitcast_convert_type(route[1], I32),
                           lax.bitcast_convert_type(route[2], I32)], axis=1)
    per_slot = jnp.zeros((n_tiles * tile, 3), I32).at[pos].set(per_token)
    tok = per_slot[:, 0]
    gates = lax.bitcast_convert_type(per_slot[:, 1:3], F32)
    return _ffn(ea, eb, nvalid, tok, gates, h, g, wg, wu, wd, layer, tile)


def _odd_inproj_kernel(x_ref, g_ref, wz_ref, wx_ref, wdt_ref, z_ref, xbc_ref, dt_ref):
    xb = _rmsnorm(x_ref[...], g_ref[...]).astype(BF16)
    z_ref[...] = jnp.dot(xb, wz_ref[...], preferred_element_type=F32)
    xbc_ref[...] = jnp.dot(xb, wx_ref[...], preferred_element_type=F32)
    dt_ref[...] = jnp.dot(xb, wdt_ref[...], preferred_element_type=F32)


def _odd_inproj(x, g, wz, wx, wdt, tm):
    n = x.shape[0]
    row = lambda i: (i, 0)
    const = lambda i: (0, 0)
    widths = (D_INNER, CONV_DIM, LANES)
    return pl.pallas_call(
        _odd_inproj_kernel,
        grid=(n // tm,),
        in_specs=[pl.BlockSpec((tm, D_MODEL), row), pl.BlockSpec((1, D_MODEL), const),
                  pl.BlockSpec(wz.shape, const), pl.BlockSpec(wx.shape, const), pl.BlockSpec(wdt.shape, const)],
        out_specs=[pl.BlockSpec((tm, c), row) for c in widths],
        out_shape=[jax.ShapeDtypeStruct((n, c), F32) for c in widths],
        compiler_params=_cparams("parallel"),
        name="odd_inproj",
    )(x, g, wz, wx, wdt)


CONV_COLS = 512


def _odd_inproj_conv_kernel(x_ref, g_ref, wz_ref, wx_ref, wdt_ref, cs_ref, cw_ref, cb_ref, dtb_ref,
                            z_ref, xc_ref, dts_ref, cso_ref, raw, *, tiles_per_seq):
    i = pl.program_id(0)
    tm = x_ref.shape[0]
    tail = CONV_W - 1
    xb = _rmsnorm(x_ref[...], g_ref[...]).astype(BF16)
    z_ref[...] = jnp.dot(xb, wz_ref[...], preferred_element_type=F32)
    dts_ref[...] = jax.nn.softplus(jnp.dot(xb, wdt_ref[...], preferred_element_type=F32) + dtb_ref[...])

    @pl.when(i % tiles_per_seq == 0)
    def _():
        raw[0:8, :] = jnp.zeros((8, CONV_DIM), F32)
        raw[8 - tail:8, :] = cs_ref[0]

    for j in range(CONV_DIM // CONV_COLS):
        cols = slice(j * CONV_COLS, (j + 1) * CONV_COLS)
        cur = jnp.dot(xb, wx_ref[:, cols], preferred_element_type=F32)
        raw[8:, cols] = cur
        acc = cb_ref[:, cols] + cur * cw_ref[tail:tail + 1, cols]
        for k in range(tail):
            acc = acc + raw[8 - tail + k:8 - tail + k + tm, cols] * cw_ref[k:k + 1, cols]
        xc_ref[:, cols] = jax.nn.silu(acc)
        new_tail = raw[8 + tm - tail:8 + tm, cols]
        cso_ref[0, :, cols] = new_tail
        raw[8 - tail:8, cols] = new_tail


def _odd_inproj_conv(x, g, wz, wx, wdt, conv_state, cw, cb, dtb, tm, n_tok):
    n = x.shape[0]
    tiles_per_seq = n_tok // tm
    row = lambda i: (i, 0)
    const = lambda i: (0, 0)
    st = lambda i: (i // tiles_per_seq, 0, 0)
    widths = (D_INNER, CONV_DIM, LANES)
    return pl.pallas_call(
        functools.partial(_odd_inproj_conv_kernel, tiles_per_seq=tiles_per_seq),
        grid=(n // tm,),
        in_specs=[pl.BlockSpec((tm, D_MODEL), row), pl.BlockSpec((1, D_MODEL), const),
                  pl.BlockSpec(wz.shape, const), pl.BlockSpec(wx.shape, const), pl.BlockSpec(wdt.shape, const),
                  pl.BlockSpec((1, CONV_W - 1, CONV_DIM), st),
                  pl.BlockSpec((CONV_W, CONV_DIM), const), pl.BlockSpec((1, CONV_DIM), const), pl.BlockSpec((1, LANES), const)],
        out_specs=[pl.BlockSpec((tm, c), row) for c in widths] + [pl.BlockSpec((1, CONV_W - 1, CONV_DIM), st)],
        out_shape=[jax.ShapeDtypeStruct((n, c), F32) for c in widths] + [jax.ShapeDtypeStruct(conv_state.shape, F32)],
        scratch_shapes=[pltpu.VMEM((8 + tm, CONV_DIM), F32)],
        compiler_params=_cparams("arbitrary"),
        name="odd_inproj_conv",
    )(x, g, wz, wx, wdt, conv_state, cw, cb, dtb)


def _conv_kernel(xbc_ref, dt_ref, cs_ref, w_ref, b_ref, dtb_ref, xc_ref, dts_ref, cso_ref, carry):
    c = pl.program_id(1)
    tt = xbc_ref.shape[0]
    tail = CONV_W - 1

    @pl.when(c == 0)
    def _():
        carry[...] = jnp.zeros(carry.shape, F32)
        carry[8 - tail:8, :] = cs_ref[0]

    for j in range(CONV_DIM // CONV_COLS):
        cols = slice(j * CONV_COLS, (j + 1) * CONV_COLS)
        x = xbc_ref[:, cols]
        full = jnp.concatenate([carry[:, cols], x], axis=0)
        acc = b_ref[:, cols] + x * w_ref[tail:tail + 1, cols]
        for k in range(tail):
            acc = acc + full[8 - tail + k:8 - tail + k + tt, :] * w_ref[k:k + 1, cols]
        xc_ref[:, cols] = jax.nn.silu(acc)
        new_tail = full[8 + tt - tail:8 + tt, :]
        carry[8 - tail:8, cols] = new_tail
        cso_ref[0, :, cols] = new_tail
    dts_ref[...] = jax.nn.softplus(dt_ref[...] + dtb_ref[...])


def _conv(xbc, dt, conv_state, w, b, dtb, n_batch, n_tok, tt):
    n_chunks = n_tok // tt
    blk = lambda i, c: (i * n_chunks + c, 0)
    const = lambda i, c: (0, 0)
    st = lambda i, c: (i, 0, 0)
    return pl.pallas_call(
        _conv_kernel,
        grid=(n_batch, n_chunks),
        in_specs=[pl.BlockSpec((tt, CONV_DIM), blk), pl.BlockSpec((tt, LANES), blk),
                  pl.BlockSpec((1, CONV_W - 1, CONV_DIM), st),
                  pl.BlockSpec((CONV_W, CONV_DIM), const), pl.BlockSpec((1, CONV_DIM), const), pl.BlockSpec((1, LANES), const)],
        out_specs=[pl.BlockSpec((tt, CONV_DIM), blk), pl.BlockSpec((tt, LANES), blk),
                   pl.BlockSpec((1, CONV_W - 1, CONV_DIM), st)],
        out_shape=[jax.ShapeDtypeStruct(xbc.shape, F32), jax.ShapeDtypeStruct(dt.shape, F32),
                   jax.ShapeDtypeStruct(conv_state.shape, F32)],
        scratch_shapes=[pltpu.VMEM((8, CONV_DIM), F32)],
        compiler_params=_cparams("parallel", "arbitrary"),
        name="ssd_conv",
    )(xbc, dt, conv_state, w, b, dtb)


def _ssd_kernel(xs_ref, bm_ref, cm_ref, z_ref, dt_ref, s0_ref, a_ref, dsk_ref, ng_ref, tril_ref, mask_ref, elast_ref,
                o_ref, sout_ref, s_scr, yz_scr, *, n_seq, seq_len, gps):
    c = pl.program_id(2)

    @pl.when(c == 0)
    def _():
        s_scr[...] = s0_ref[...]

    lane = _lane_iota((ROWS, LANES))
    row = _row_iota((ROWS, LANES))
    lo_half = lane < (LANES // 2)
    dt = dt_ref[...]
    cs = _dot01(tril_ref[...], dt * a_ref[...])
    cs_last = _dot01(elast_ref[...], cs)
    if gps != SSM_GROUPS:
        shift = (LANES - pl.program_id(1) * (gps * SSM_HEADS // SSM_GROUPS)) % LANES
        dt, cs, cs_last = (pltpu.roll(v, shift, 1) for v in (dt, cs, cs_last))
    cs_t = cs.T
    ecs = jnp.exp(cs)
    dt_tail = dt * jnp.exp(cs_last - cs)
    causal = mask_ref[...] > 0.0

    def per_head(v, h0):
        return jnp.where(lo_half, v[:, h0:h0 + 1], v[:, h0 + 1:h0 + 2])

    for g in range(gps):
        gcols = slice(g * D_STATE, (g + 1) * D_STATE)
        cg = cm_ref[:, gcols].astype(BF16)
        bg = bm_ref[:, gcols].astype(BF16)
        cb = _dot_nt(cg, bg)
        for pp in range(2):
            pair = 2 * g + pp
            h0 = 2 * pair
            pcols = slice(pair * LANES, (pair + 1) * LANES)
            x2 = xs_ref[:, pcols]
            xdt = x2 * per_head(dt, h0)
            ys = []
            for e in range(2):
                h = h0 + e
                seg = cs[:, h:h + 1] - cs_t[h:h + 1, :]
                wmat = jnp.exp(jnp.where(causal, seg, _NEG_INF)) * cb
                ys.append(_dot(wmat, xdt))
            y2 = jnp.where(lo_half, ys[0], ys[1])
            if n_seq == 1:
                inter = _dot_nt(cg, s_scr[0, pair])
            else:
                inter = jnp.zeros((ROWS, LANES), F32)
                for s in range(n_seq):
                    in_seq = (row >= s * seq_len) & (row < (s + 1) * seq_len)
                    inter = inter + jnp.where(in_seq, _dot_nt(cg, s_scr[s, pair]), 0.0)
            y2 = y2 + inter * per_head(ecs, h0) + dsk_ref[:, pcols] * x2
            yz_scr[:, pcols] = y2 * jax.nn.silu(z_ref[:, pcols])
            xw_t = (x2 * per_head(dt_tail, h0)).T
            for s in range(n_seq):
                last = (s + 1) * seq_len - 1
                if n_seq == 1:
                    xt = xw_t
                else:
                    xt = jnp.where((lane >= s * seq_len) & (lane <= last), xw_t, 0.0)
                dec = jnp.exp(jnp.where(row < (LANES // 2), cs_t[h0:h0 + 1, last:last + 1], cs_t[h0 + 1:h0 + 2, last:last + 1]))
                s_scr[s, pair] = s_scr[s, pair] * dec + _dot(xt, bg)
    gw = D_INNER // SSM_GROUPS
    for g in range(gps):
        cols = slice(g * gw, (g + 1) * gw)
        o_ref[:, cols] = _rmsnorm(yz_scr[:, cols], ng_ref[:, cols])

    @pl.when(c == pl.num_programs(2) - 1)
    def _():
        sout_ref[...] = s_scr[...]


def _ssd(xc, z, dts, s0, a_row, dsk_row, ng_row, n_batch, n_tok):
    seq_len = min(ROWS, n_tok)
    n_seq = ROWS // seq_len
    n_chunks = n_tok // seq_len
    r = np.arange(ROWS)
    li, sid = r % seq_len, r // seq_len
    mask = ((sid[:, None] == sid[None, :]) & (li[:, None] >= li[None, :]))
    elast = (r[None, :] == (sid * seq_len + seq_len - 1)[:, None])
    n_pairs = SSM_HEADS // 2
    s0p = s0.reshape(n_batch, n_pairs, LANES, D_STATE)
    gps = SSM_GROUPS if n_seq == 1 else 1
    xw = gps * (D_INNER // SSM_GROUPS)
    bw = gps * D_STATE
    rows = lambda i, j, c: i * n_chunks + c
    xblk = lambda i, j, c: (rows(i, j, c), j)
    bblk = lambda i, j, c: (rows(i, j, c), D_INNER // bw + j)
    cblk = lambda i, j, c: (rows(i, j, c), (D_INNER + SSM_GROUPS * D_STATE) // bw + j)
    dtblk = lambda i, j, c: (rows(i, j, c), 0)
    st = lambda i, j, c: (i, j, 0, 0)
    const = lambda i, j, c: (0, 0)
    gconst = lambda i, j, c: (0, j)
    state_block = (n_seq, 2 * gps, LANES, D_STATE)
    o, s_out = pl.pallas_call(
        functools.partial(_ssd_kernel, n_seq=n_seq, seq_len=seq_len, gps=gps),
        grid=(n_batch // n_seq, SSM_GROUPS // gps, n_chunks),
        in_specs=[pl.BlockSpec((ROWS, xw), xblk), pl.BlockSpec((ROWS, bw), bblk), pl.BlockSpec((ROWS, bw), cblk),
                  pl.BlockSpec((ROWS, xw), xblk), pl.BlockSpec((ROWS, LANES), dtblk),
                  pl.BlockSpec(state_block, st),
                  pl.BlockSpec((1, LANES), const), pl.BlockSpec((1, xw), gconst), pl.BlockSpec((1, xw), gconst),
                  pl.BlockSpec((ROWS, ROWS), const), pl.BlockSpec((ROWS, ROWS), const), pl.BlockSpec((ROWS, ROWS), const)],
        out_specs=[pl.BlockSpec((ROWS, xw), xblk), pl.BlockSpec(state_block, st)],
        out_shape=[jax.ShapeDtypeStruct((n_batch * n_tok, D_INNER), F32), jax.ShapeDtypeStruct(s0p.shape, F32)],
        scratch_shapes=[pltpu.VMEM(state_block, F32), pltpu.VMEM((ROWS, xw), F32)],
        compiler_params=_cparams("parallel", "parallel", "arbitrary"),
        name="ssd_scan",
    )(xc, xc, xc, z, dts, s0p, a_row, dsk_row, ng_row, jnp.asarray(mask, BF16), jnp.asarray(mask, F32),
      jnp.asarray(elast, BF16))
    return o, s_out.reshape(s0.shape)


def _rope_tables(pos, min_rows):
    half = MOBA_HD // 2
    inv = jnp.power(ROPE_THETA, -jnp.arange(half, dtype=F32) * 2.0 / MOBA_HD)
    ang = jnp.asarray(pos, F32)[:, None] * inv[None, :]
    cos, sin = jnp.cos(ang), jnp.sin(ang)
    reps = (max(1, min_rows // ang.shape[0]), LANES // MOBA_HD)
    return jnp.tile(jnp.concatenate([cos, cos], axis=1), reps), jnp.tile(jnp.concatenate([-sin, sin], axis=1), reps)


def _pad_cols(w, width):
    return jnp.pad(w, ((0, 0), (0, width - w.shape[1])))


TOKEN_TILE = 256
OUTPROJ_TILE = 512


def _even_layer(h, nb, nt, pos, ret0, cache, norm_g, w_in, w_out, q_g, k_g):
    tm = TOKEN_TILE
    cos, sin = _rope_tables(pos, tm)
    head_tile = lambda v: jnp.tile(v, MOBA_HEADS)[None, :]
    width = MOBA_HEADS * MOBA_HD
    blockdiag = jnp.asarray(np.arange(width)[:, None] // MOBA_HD == np.arange(width)[None, :] // MOBA_HD, BF16)
    outs = _even_inproj(h, norm_g[None, :], w_in.astype(BF16), cos, sin, head_tile(q_g), head_tile(k_g), blockdiag,
                        tm, nt, key_minor=cache is None)
    rq, rk, rv, rg = outs[:4]
    if cache is None:
        qt, kb, kt, vt, vtb, kbar = outs[4:]
        mo = _moba_prompt(qt, kb, vtb, kbar.reshape(-1, width), nb, nt)
        mk, mv = (a.reshape(nb, MOBA_HEADS, MOBA_HD, nt).transpose(0, 3, 1, 2) for a in (kt, vt))
    else:
        mq, mk, mv = outs[4:]
        mo = _moba_sample(mq, mk, mv, *cache, nb, nt)
        mk, mv = (a.reshape(nb, nt, MOBA_HEADS, MOBA_HD) for a in (mk, mv))
    ro, ret_state = _retention(rq, rk, rv, rg, ret0, nb, nt)
    w_out = w_out.astype(BF16)
    h = _outproj(h, (ro, mo), (w_out[:RET_HEADS * RET_DV], w_out[RET_HEADS * RET_DV:]), min(OUTPROJ_TILE, h.shape[0]))
    return h, ret_state, mk, mv


def _odd_layer(h, nb, nt, conv0, ssm0, norm_g, w_in, conv_w, conv_b, dt_bias, a_log, d_skip, ssd_norm, w_out):
    tm = TOKEN_TILE
    w_in = w_in.astype(BF16)
    weights = (w_in[:, :D_INNER], w_in[:, D_INNER:D_INNER + CONV_DIM], _pad_cols(w_in[:, D_INNER + CONV_DIM:], LANES))
    conv_params = (conv_w, conv_b[None, :], _pad_cols(dt_bias[None, :], LANES))
    if nt % tm == 0:
        z, xc, dts, conv_state = _odd_inproj_conv(h, norm_g[None, :], *weights, conv0, *conv_params, tm, nt)
    else:
        z, xbc, dt = _odd_inproj(h, norm_g[None, :], *weights, tm)
        xc, dts, conv_state = _conv(xbc, dt, conv0, *conv_params, nb, nt, nt)
    a_row = _pad_cols(-jnp.exp(a_log)[None, :], LANES)
    dsk_row = jnp.repeat(d_skip, D_INNER // SSM_HEADS)[None, :]
    yzn, ssm_state = _ssd(xc, z, dts, ssm0, a_row, dsk_row, ssd_norm[None, :], nb, nt)
    h = _outproj(h, (yzn,), (w_out.astype(BF16),), min(OUTPROJ_TILE, h.shape[0]))
    return h, conv_state, ssm_state


def _moe_layer(h, norm_g, w_rg, w_re, w_g, w_u, w_d, layer):
    wr = _pad_cols(jnp.concatenate([w_rg, w_re], axis=1), LANES)
    return _hier_moe(h, norm_g[None, :], wr, w_g, w_u, w_d, layer)


def kernel(x_prompt, x_sample, state_ret, cache_k, cache_v, page_table, state_ssm, state_conv, norm_mix, norm_ffn,
           w_in_even, w_out_even, q_norm, k_norm, w_in_odd, conv_w, conv_b, dt_bias, a_log, d_skip, ssd_norm, w_out_odd,
           w_router_group, w_router_expert, w_expert_gate, w_expert_up, w_expert_down):
    bp, seq, _ = x_prompt.shape
    bs, dec_seq, _ = x_sample.shape
    past_len = page_table.shape[1] * PAGE_SIZE
    groups = ((x_prompt.reshape(bp * seq, D_MODEL), bp, seq, np.arange(seq, dtype=np.float32)),
              (x_sample.reshape(bs * dec_seq, D_MODEL), bs, dec_seq, past_len + np.arange(dec_seq, dtype=np.float32)))

    def moe(h, layer):
        return _moe_layer(h, norm_ffn[layer], w_router_group[layer], w_router_expert[layer],
                          w_expert_gate, w_expert_up, w_expert_down, layer)

    outs = []
    for gi, (h, nb, nt, pos) in enumerate(groups):
        if gi == 0:
            ret0 = jnp.zeros((nb, RET_HEADS, RET_DK, RET_DV), F32)
            cache = None
            conv0 = jnp.zeros((nb, CONV_W - 1, CONV_DIM), F32)
            ssm0 = jnp.zeros((nb, SSM_HEADS, D_INNER // SSM_HEADS, D_STATE), F32)
        else:
            ret0 = state_ret[0]
            cache = (cache_k[0].transpose(0, 2, 3, 1), cache_v[0].transpose(0, 2, 3, 1), page_table)
            conv0, ssm0 = state_conv[0], state_ssm[0]
        h, ret_state, mk, mv = _even_layer(h, nb, nt, pos, ret0, cache, norm_mix[0], w_in_even[0], w_out_even[0],
                                           q_norm[0], k_norm[0])
        h = moe(h, 0)
        h, conv_state, ssm_state = _odd_layer(h, nb, nt, conv0, ssm0, norm_mix[1], w_in_odd[0], conv_w[0], conv_b[0],
                                              dt_bias[0], a_log[0], d_skip[0], ssd_norm[0], w_out_odd[0])
        h = moe(h, 1)
        outs.append(dict(h=h.reshape(nb, nt, D_MODEL), ret=ret_state[None], k=mk[None], v=mv[None],
                         ssm=ssm_state[None], conv=conv_state[None]))
    p, s = outs
    return (p["h"], s["h"], p["ret"], s["ret"], p["k"], p["v"], s["k"], s["v"], p["ssm"], s["ssm"], p["conv"], s["conv"])
```
